```python
import math
import jax
import jax.numpy as jnp
from jax import lax
import numpy as np

D_MODEL = 1024
BATCH = 8
SEQ = 4096
DEPTH = 2
DEC_BATCH = 128
DEC_SEQ = 1
PAST_LEN = 16384
PAGE_SIZE = 128

SWA_HEADS = 12
SWA_KV_HEADS = 4
SWA_GROUP = SWA_HEADS // SWA_KV_HEADS
HEAD_DIM = 64
WINDOW = 128
SWA_BLOCK = 128
ROPE_THETA = 10000.0
SWA_QW = SWA_HEADS * HEAD_DIM
SWA_KVW = SWA_KV_HEADS * HEAD_DIM
GDN_HEADS = 6
GDN_DK = 128
GDN_DV = 128
GDN_CONV_W = 4
GDN_CHUNK = 64
GDN_QK = GDN_HEADS * GDN_DK
GDN_VW = GDN_HEADS * GDN_DV
GDN_CONV_DIM = 2 * GDN_QK + GDN_VW
N_MEM = 256
MEM_HEADS = 4
MEM_HEAD_DIM = 64
MEM_WIDTH = MEM_HEADS * MEM_HEAD_DIM
MIX_WIDTH = SWA_QW + MEM_WIDTH
SWA_IN = SWA_QW + 2 * SWA_KVW + MEM_WIDTH
GDN_IN = GDN_CONV_DIM + GDN_VW + 2 * GDN_HEADS + MEM_WIDTH
D_FF = 3584
N_EXPERTS = 8
TOP_K = 2
N_SWA_LAYERS = (DEPTH + 1) // 2
N_GDN_LAYERS = DEPTH // 2
DEEPNORM_ALPHA = (2 * DEPTH) ** 0.25
DEEPNORM_BETA = (8 * DEPTH) ** -0.25
LN_EPS = 1e-5
RMS_EPS = 1e-6
L2_EPS = 1e-6

kernel_name = 'swa_gdn_hybrid_step'


def layer_norm(x, g, b):
    xf = x.astype(jnp.float32)
    mu = jnp.mean(xf, axis=-1, keepdims=True)
    var = jnp.mean(jnp.square(xf - mu), axis=-1, keepdims=True)
    return ((xf - mu) * lax.rsqrt(var + LN_EPS) * g.astype(jnp.float32) + b.astype(jnp.float32)).astype(x.dtype)


def deepnorm_residual(x, h, g, b):
    return layer_norm(DEEPNORM_ALPHA * x + h, g, b)


def rope(x, pos):
    half = x.shape[-1] // 2
    inv = ROPE_THETA ** (-jnp.arange(half, dtype=jnp.float32) / half)
    ang = pos.astype(jnp.float32)[:, None] * inv[None, :]
    cos = jnp.cos(ang)[None, :, None, :]
    sin = jnp.sin(ang)[None, :, None, :]
    xf = x.astype(jnp.float32)
    x1, x2 = xf[..., :half], xf[..., half:]
    return jnp.concatenate([x1 * cos - x2 * sin, x2 * cos + x1 * sin], axis=-1).astype(x.dtype)


def sink_attention(q, k, v, mask, sinks):
    s = jnp.einsum('...qkgd,...skd->...kgqs', q, k).astype(jnp.float32) * HEAD_DIM ** -0.5
    s = jnp.where(mask, s, -jnp.inf)
    sink = sinks.astype(jnp.float32).reshape(SWA_KV_HEADS, SWA_GROUP, 1, 1)
    m = jnp.maximum(jnp.max(s, axis=-1, keepdims=True), sink)
    e = jnp.exp(s - m)
    p = e / (jnp.sum(e, axis=-1, keepdims=True) + jnp.exp(sink - m))
    return jnp.einsum('...kgqs,...skd->...qkgd', p.astype(v.dtype), v)


def swa_project(x, w_in, pos):
    b, t = x.shape[:2]
    proj = jnp.einsum('btd,de->bte', x, w_in)
    q, k, v, qm = jnp.split(proj, [SWA_QW, SWA_QW + SWA_KVW, SWA_QW + 2 * SWA_KVW], axis=-1)
    q = rope(q.reshape(b, t, SWA_HEADS, HEAD_DIM), pos).reshape(b, t, SWA_KV_HEADS, SWA_GROUP, HEAD_DIM)
    k = rope(k.reshape(b, t, SWA_KV_HEADS, HEAD_DIM), pos)
    v = v.reshape(b, t, SWA_KV_HEADS, HEAD_DIM)
    return q, k, v, qm


def swa_banded(q, k, v, sinks):
    b, t = q.shape[:2]
    nb = t // SWA_BLOCK
    qb = q.reshape(b, nb, SWA_BLOCK, SWA_KV_HEADS, SWA_GROUP, HEAD_DIM)
    kb = k.reshape(b, nb, SWA_BLOCK, SWA_KV_HEADS, HEAD_DIM)
    vb = v.reshape(b, nb, SWA_BLOCK, SWA_KV_HEADS, HEAD_DIM)
    shift = ((0, 0), (1, 0), (0, 0), (0, 0), (0, 0))
    kc = jnp.concatenate([jnp.pad(kb, shift)[:, :-1], kb], axis=2)
    vc = jnp.concatenate([jnp.pad(vb, shift)[:, :-1], vb], axis=2)
    blk = jnp.arange(nb, dtype=jnp.int32)[:, None] * SWA_BLOCK
    qpos = blk + jnp.arange(SWA_BLOCK, dtype=jnp.int32)[None, :]
    kpos = blk - SWA_BLOCK + jnp.arange(2 * SWA_BLOCK, dtype=jnp.int32)[None, :]
    diff = qpos[:, :, None] - kpos[:, None, :]
    mask = (diff >= 0) & (diff <= WINDOW) & (kpos[:, None, :] >= 0)
    o = sink_attention(qb, kc, vc, mask[:, None, None], sinks)
    return o.reshape(b, t, SWA_QW)


def swa_cached(q, k, v, buf_k, buf_v, sinks):
    s = q.shape[1]
    l = buf_k.shape[1]
    kc = jnp.concatenate([buf_k.astype(k.dtype), k], axis=1)
    vc = jnp.concatenate([buf_v.astype(v.dtype), v], axis=1)
    qpos = PAST_LEN + jnp.arange(s, dtype=jnp.int32)
    kpos = PAST_LEN - l + jnp.arange(l + s, dtype=jnp.int32)
    diff = qpos[:, None] - kpos[None, :]
    mask = (diff >= 0) & (diff <= WINDOW)
    o = sink_attention(q, kc, vc, mask, sinks)
    return o.reshape(q.shape[0], s, SWA_QW), kc[:, -l:], vc[:, -l:]


def causal_conv(xp, w):
    t = xp.shape[1] - (GDN_CONV_W - 1)
    out = xp[:, 0:t] * w[0]
    for j in range(1, GDN_CONV_W):
        out = out + xp[:, j:j + t] * w[j]
    return out


def l2norm(x):
    return x * lax.rsqrt(jnp.sum(x * x, axis=-1, keepdims=True) + L2_EPS)


def gdn_chunked(q, k, v, g, beta, s0):
    b, t = q.shape[:2]
    n = t // GDN_CHUNK
    c = GDN_CHUNK
    qc = q.reshape(b, n, c, GDN_HEADS, GDN_DK).transpose(1, 0, 3, 2, 4)
    kc = k.reshape(b, n, c, GDN_HEADS, GDN_DK).transpose(1, 0, 3, 2, 4)
    vc = v.reshape(b, n, c, GDN_HEADS, GDN_DV).transpose(1, 0, 3, 2, 4)
    gc = jnp.cumsum(g.reshape(b, n, c, GDN_HEADS).transpose(1, 0, 3, 2), axis=-1)
    bc = beta.reshape(b, n, c, GDN_HEADS).transpose(1, 0, 3, 2)
    idx = jnp.arange(c)
    causal = idx[:, None] >= idx[None, :]
    strict = idx[:, None] > idx[None, :]
    decay = jnp.exp(jnp.where(causal, gc[..., :, None] - gc[..., None, :], -jnp.inf))
    kb = kc * bc[..., None]
    a = jnp.where(strict, jnp.einsum('nbhik,nbhjk->nbhij', kb, kc) * decay, 0.0)
    eye = jnp.eye(c, dtype=jnp.float32)
    tinv = lax.linalg.triangular_solve(a + eye, jnp.broadcast_to(eye, a.shape), left_side=True, lower=True, unit_diagonal=True)
    u = jnp.einsum('nbhij,nbhjv->nbhiv', tinv, vc * bc[..., None])
    w = jnp.einsum('nbhij,nbhjk->nbhik', tinv, kb * jnp.exp(gc)[..., None])
    qk = jnp.where(causal, jnp.einsum('nbhik,nbhjk->nbhij', qc, kc) * decay, 0.0)

    def step(state, xs):
        q_c, k_c, u_c, w_c, g_c, qk_c = xs
        v_new = u_c - jnp.einsum('bhik,bhkv->bhiv', w_c, state)
        o_c = jnp.einsum('bhik,bhkv->bhiv', q_c * jnp.exp(g_c)[..., None], state) + jnp.einsum('bhij,bhjv->bhiv', qk_c, v_new)
        g_last = g_c[..., -1:]
        state = state * jnp.exp(g_last)[..., None] + jnp.einsum('bhik,bhiv->bhkv', k_c * jnp.exp(g_last - g_c)[..., None], v_new)
        return state, o_c

    s_final, o = lax.scan(step, s0, (qc, kc, u, w, gc, qk))
    return o.transpose(1, 0, 3, 2, 4).reshape(b, t, GDN_HEADS, GDN_DV), s_final


def gdn_recurrent(q, k, v, g, beta, s0):
    def step(state, xs):
        q_t, k_t, v_t, g_t, b_t = xs
        state = state * jnp.exp(g_t)[..., None, None]
        v_new = (v_t - jnp.einsum('bhk,bhkv->bhv', k_t, state)) * b_t[..., None]
        state = state + jnp.einsum('bhk,bhv->bhkv', k_t, v_new)
        return state, jnp.einsum('bhk,bhkv->bhv', q_t, state)

    xs = (jnp.moveaxis(q, 1, 0), jnp.moveaxis(k, 1, 0), jnp.moveaxis(v, 1, 0), jnp.moveaxis(g, 1, 0), jnp.moveaxis(beta, 1, 0))
    s_final, o = lax.scan(step, s0, xs)
    return jnp.moveaxis(o, 0, 1), s_final


def gdn_mixer(x, conv_state, rec_state, w_in, conv_w, a_log, dt_bias, norm_w, chunked):
    b, t = x.shape[:2]
    proj = jnp.einsum('btd,de->bte', x, w_in)
    o1 = GDN_CONV_DIM
    o2 = o1 + GDN_VW
    o3 = o2 + GDN_HEADS
    o4 = o3 + GDN_HEADS
    qkv, z, a, bb, qm = jnp.split(proj, [o1, o2, o3, o4], axis=-1)
    xp = jnp.concatenate([conv_state.astype(qkv.dtype), qkv], axis=1)
    c = jax.nn.silu(causal_conv(xp, conv_w)).astype(jnp.float32)
    q, k, v = jnp.split(c, [GDN_QK, 2 * GDN_QK], axis=-1)
    q = l2norm(q.reshape(b, t, GDN_HEADS, GDN_DK)) * GDN_DK ** -0.5
    k = l2norm(k.reshape(b, t, GDN_HEADS, GDN_DK))
    v = v.reshape(b, t, GDN_HEADS, GDN_DV)
    g = -jnp.exp(a_log.astype(jnp.float32)) * jax.nn.softplus(a.astype(jnp.float32) + dt_bias.astype(jnp.float32))
    beta = jax.nn.sigmoid(bb.astype(jnp.float32))
    s0 = rec_state.astype(jnp.float32)
    if chunked:
        o, s_new = gdn_chunked(q, k, v, g, beta, s0)
    else:
        o, s_new = gdn_recurrent(q, k, v, g, beta, s0)
    on = o * lax.rsqrt(jnp.mean(o * o, axis=-1, keepdims=True) + RMS_EPS) * norm_w.astype(jnp.float32)
    h = (on * jax.nn.silu(z.reshape(b, t, GDN_HEADS, GDN_DV).astype(jnp.float32))).reshape(b, t, GDN_VW).astype(x.dtype)
    return h, qm, xp[:, -(GDN_CONV_W - 1):], s_new.astype(x.dtype)


def mem_project(mem, w_kv):
    b, m = mem.shape[:2]
    k, v = jnp.split(jnp.einsum('bmd,de->bme', mem, w_kv), 2, axis=-1)
    return k.reshape(b, m, MEM_HEADS, MEM_HEAD_DIM), v.reshape(b, m, MEM_HEADS, MEM_HEAD_DIM)


def mem_attention(qm, mem_k, mem_v):
    b, t = qm.shape[:2]
    q = qm.reshape(b, t, MEM_HEADS, MEM_HEAD_DIM)
    s = jnp.einsum('bthd,bmhd->bhtm', q, mem_k.astype(q.dtype)).astype(jnp.float32) * MEM_HEAD_DIM ** -0.5
    p = jax.nn.softmax(s, axis=-1).astype(q.dtype)
    return jnp.einsum('bhtm,bmhd->bthd', p, mem_v.astype(q.dtype)).reshape(b, t, MEM_WIDTH)


def finish_mixer(x, h, qm, mem_k, mem_v, w_out_l, g, b):
    mix = jnp.concatenate([h, mem_attention(qm, mem_k, mem_v).astype(h.dtype)], axis=-1)
    return deepnorm_residual(x, jnp.einsum('bte,ed->btd', mix, w_out_l), g, b)


def swiglu(x, w_gu, w_down):
    gt, up = jnp.split(jnp.einsum('btd,df->btf', x, w_gu), 2, axis=-1)
    return jnp.einsum('btf,fd->btd', jax.nn.silu(gt) * up, w_down)


def moe_swiglu(x, router_w, w_gu, w_down):
    logits = jnp.einsum('btd,de->bte', x, router_w).astype(jnp.float32)
    probs = jax.nn.softmax(logits, axis=-1)
    top_p, top_i = lax.top_k(probs, TOP_K)
    top_p = top_p / jnp.sum(top_p, axis=-1, keepdims=True)
    gate = jnp.sum(jax.nn.one_hot(top_i, N_EXPERTS, dtype=jnp.float32) * top_p[..., None], axis=-2)
    y = jnp.zeros_like(x)
    for e in range(N_EXPERTS):
        y = y + gate[..., e:e + 1].astype(x.dtype) * swiglu(x, w_gu[e], w_down[e])
    return y


def setup_inputs(seed: int = 0) -> dict:
    key = jax.random.key(seed)
    ks = jax.random.split(key, 32)
    f32 = jnp.float32

    def nrm(k, shape, scale):
        return jax.random.normal(k, shape, f32) * scale

    win_buf = min(WINDOW, PAST_LEN)
    dt = jnp.exp(jax.random.uniform(ks[20], (N_GDN_LAYERS, GDN_HEADS), f32, math.log(1e-3), math.log(1e-1)))
    return {
        'x_prompt': nrm(ks[0], (BATCH, SEQ, D_MODEL), 1.0),
        'x_sample': nrm(ks[1], (DEC_BATCH, DEC_SEQ, D_MODEL), 1.0),
        'cache_swa_k': nrm(ks[2], (N_SWA_LAYERS, DEC_BATCH, win_buf, SWA_KV_HEADS, HEAD_DIM), 1.0),
        'cache_swa_v': nrm(ks[3], (N_SWA_LAYERS, DEC_BATCH, win_buf, SWA_KV_HEADS, HEAD_DIM), 1.0),
        'state_gdn_conv': nrm(ks[4], (N_GDN_LAYERS, DEC_BATCH, GDN_CONV_W - 1, GDN_CONV_DIM), 1.0),
        'state_gdn_rec': nrm(ks[5], (N_GDN_LAYERS, DEC_BATCH, GDN_HEADS, GDN_DK, GDN_DV), 0.3),
        'cache_mem_k': nrm(ks[6], (DEPTH, DEC_BATCH, N_MEM, MEM_HEADS, MEM_HEAD_DIM), 1.0),
        'cache_mem_v': nrm(ks[7], (DEPTH, DEC_BATCH, N_MEM, MEM_HEADS, MEM_HEAD_DIM), 1.0),
        'mem_prompt': nrm(ks[8], (BATCH, N_MEM, D_MODEL), 1.0),
        'w_in_swa': nrm(ks[9], (N_SWA_LAYERS, D_MODEL, SWA_IN), D_MODEL ** -0.5),
        'swa_sinks': nrm(ks[10], (N_SWA_LAYERS, SWA_HEADS), 0.5),
        'w_in_gdn': nrm(ks[11], (N_GDN_LAYERS, D_MODEL, GDN_IN), D_MODEL ** -0.5),
        'gdn_conv_w': nrm(ks[12], (N_GDN_LAYERS, GDN_CONV_W, GDN_CONV_DIM), GDN_CONV_W ** -0.5),
        'gdn_a_log': jnp.log(jax.random.uniform(ks[13], (N_GDN_LAYERS, GDN_HEADS), f32, 1.0, 16.0)),
        'gdn_dt_bias': dt + jnp.log(-jnp.expm1(-dt)),
        'gdn_norm_w': 1.0 + nrm(ks[14], (N_GDN_LAYERS, GDN_DV), 0.1),
        'w_mem_kv': nrm(ks[15], (DEPTH, D_MODEL, 2 * MEM_WIDTH), D_MODEL ** -0.5),
        'w_out': nrm(ks[16], (DEPTH, MIX_WIDTH, D_MODEL), MIX_WIDTH ** -0.5 * DEEPNORM_BETA),
        'ln1_g': 1.0 + nrm(ks[17], (DEPTH, D_MODEL), 0.1),
        'ln1_b': nrm(ks[18], (DEPTH, D_MODEL), 0.1),
        'ln2_g': 1.0 + nrm(ks[19], (DEPTH, D_MODEL), 0.1),
        'ln2_b': nrm(ks[21], (DEPTH, D_MODEL), 0.1),
        'ffn_w_gu': nrm(ks[22], (N_SWA_LAYERS, D_MODEL, 2 * D_FF), D_MODEL ** -0.5),
        'ffn_w_down': nrm(ks[23], (N_SWA_LAYERS, D_FF, D_MODEL), D_FF ** -0.5 * DEEPNORM_BETA),
        'router_w': nrm(ks[24], (N_GDN_LAYERS, D_MODEL, N_EXPERTS), D_MODEL ** -0.5),
        'moe_w_gu': nrm(ks[25], (N_GDN_LAYERS, N_EXPERTS, D_MODEL, 2 * D_FF), D_MODEL ** -0.5),
        'moe_w_down': nrm(ks[26], (N_GDN_LAYERS, N_EXPERTS, D_FF, D_MODEL), D_FF ** -0.5 * DEEPNORM_BETA),
    }


def reference(x_prompt, x_sample, cache_swa_k, cache_swa_v, state_gdn_conv, state_gdn_rec, cache_mem_k, cache_mem_v, mem_prompt, w_in_swa, swa_sinks, w_in_gdn, gdn_conv_w, gdn_a_log, gdn_dt_bias, gdn_norm_w, w_mem_kv, w_out, ln1_g, ln1_b, ln2_g, ln2_b, ffn_w_gu, ffn_w_down, router_w, moe_w_gu, moe_w_down):
    b = x_prompt.shape[0]
    pos_p = jnp.arange(x_prompt.shape[1], dtype=jnp.int32)
    pos_s = PAST_LEN + jnp.arange(x_sample.shape[1], dtype=jnp.int32)
    yp, ys = x_prompt, x_sample
    swa_kp, swa_vp, swa_ks, swa_vs = [], [], [], []
    conv_p, rec_p, conv_s, rec_s = [], [], [], []
    mem_kp, mem_vp = [], []
    for i in range(DEPTH):
        j = i // 2
        mk_p, mv_p = mem_project(mem_prompt, w_mem_kv[i])
        mem_kp.append(mk_p)
        mem_vp.append(mv_p)
        if i % 2 == 0:
            q, k, v, qm_p = swa_project(yp, w_in_swa[j], pos_p)
            h_p = swa_banded(q, k, v, swa_sinks[j])
            swa_kp.append(k[:, -WINDOW:])
            swa_vp.append(v[:, -WINDOW:])
            q, k, v, qm_s = swa_project(ys, w_in_swa[j], pos_s)
            h_s, nk, nv = swa_cached(q, k, v, cache_swa_k[j], cache_swa_v[j], swa_sinks[j])
            swa_ks.append(nk)
            swa_vs.append(nv)
        else:
            zero_conv = jnp.zeros((b, GDN_CONV_W - 1, GDN_CONV_DIM), yp.dtype)
            zero_rec = jnp.zeros((b, GDN_HEADS, GDN_DK, GDN_DV), jnp.float32)
            h_p, qm_p, c_new, r_new = gdn_mixer(yp, zero_conv, zero_rec, w_in_gdn[j], gdn_conv_w[j], gdn_a_log[j], gdn_dt_bias[j], gdn_norm_w[j], True)
            conv_p.append(c_new)
            rec_p.append(r_new)
            h_s, qm_s, c_new, r_new = gdn_mixer(ys, state_gdn_conv[j], state_gdn_rec[j], w_in_gdn[j], gdn_conv_w[j], gdn_a_log[j], gdn_dt_bias[j], gdn_norm_w[j], False)
            conv_s.append(c_new)
            rec_s.append(r_new)
        yp = finish_mixer(yp, h_p, qm_p, mk_p, mv_p, w_out[i], ln1_g[i], ln1_b[i])
        ys = finish_mixer(ys, h_s, qm_s, cache_mem_k[i], cache_mem_v[i], w_out[i], ln1_g[i], ln1_b[i])
        if i % 2 == 0:
            yp = deepnorm_residual(yp, swiglu(yp, ffn_w_gu[j], ffn_w_down[j]), ln2_g[i], ln2_b[i])
            ys = deepnorm_residual(ys, swiglu(ys, ffn_w_gu[j], ffn_w_down[j]), ln2_g[i], ln2_b[i])
        else:
            yp = deepnorm_residual(yp, moe_swiglu(yp, router_w[j], moe_w_gu[j], moe_w_down[j]), ln2_g[i], ln2_b[i])
            ys = deepnorm_residual(ys, moe_swiglu(ys, router_w[j], moe_w_gu[j], moe_w_down[j]), ln2_g[i], ln2_b[i])
    return (yp, ys, jnp.stack(swa_kp), jnp.stack(swa_vp), jnp.stack(conv_p), jnp.stack(rec_p), jnp.stack(mem_kp), jnp.stack(mem_vp), jnp.stack(swa_ks), jnp.stack(swa_vs), jnp.stack(conv_s), jnp.stack(rec_s))
```

```python
import functools

import jax
import jax.numpy as jnp
from jax import lax
from jax.experimental import pallas as pl
from jax.experimental.pallas import tpu as pltpu

F32 = jnp.float32
BF16 = jnp.bfloat16

D_MODEL = 1024
DEPTH = 2
PAST_LEN = 16384
SWA_HEADS = 12
SWA_KV_HEADS = 4
SWA_GROUP = SWA_HEADS // SWA_KV_HEADS
HEAD_DIM = 64
WINDOW = 128
SWA_BLOCK = 128
ROPE_THETA = 10000.0
SWA_QW = SWA_HEADS * HEAD_DIM
SWA_KVW = SWA_KV_HEADS * HEAD_DIM
GDN_HEADS = 6
GDN_DK = 128
GDN_DV = 128
GDN_CONV_W = 4
GDN_CHUNK = 64
GDN_QK = GDN_HEADS * GDN_DK
GDN_VW = GDN_HEADS * GDN_DV
GDN_CONV_DIM = 2 * GDN_QK + GDN_VW
N_MEM = 256
MEM_HEADS = 4
MEM_HEAD_DIM = 64
MEM_WIDTH = MEM_HEADS * MEM_HEAD_DIM
D_FF = 3584
N_EXPERTS = 8
DEEPNORM_ALPHA = (2 * DEPTH) ** 0.25
LN_EPS = 1e-5
RMS_EPS = 1e-6
L2_EPS = 1e-6

LANES = 128
SUBLANES = 8
GDN_AB_PAD = LANES
VMEM_LIMIT_MB = 56


def _cparams(sem, vmem_mb=VMEM_LIMIT_MB):
    return pltpu.CompilerParams(dimension_semantics=sem, vmem_limit_bytes=vmem_mb * 1024 * 1024)


def _bf16_round(x):
    return x.astype(BF16).astype(F32)


def _dot(a, b):
    return jnp.dot(a.astype(BF16), b.astype(BF16), preferred_element_type=F32)


def _dot_nt(a, b):
    return lax.dot_general(a.astype(BF16), b.astype(BF16), (((1,), (1,)), ((), ())),
                           preferred_element_type=F32)


def _dot_tn(a, b):
    return lax.dot_general(a.astype(BF16), b.astype(BF16), (((0,), (0,)), ((), ())),
                           preferred_element_type=F32)


def _dot_f32(a, b):
    return jnp.dot(a, b, preferred_element_type=F32, precision=lax.Precision.HIGHEST)


def _layer_norm(t, g, b):
    mu = jnp.mean(t, axis=-1, keepdims=True)
    d = t - mu
    var = jnp.mean(d * d, axis=-1, keepdims=True)
    return d * lax.rsqrt(var + LN_EPS) * g + b


def _mm_kernel(x_ref, w_ref, o_ref):
    o_ref[...] = _dot(x_ref[...], w_ref[...])


def _matmul(x, w, tm):
    m, k = x.shape
    n = w.shape[1]
    return pl.pallas_call(
        _mm_kernel,
        grid=(m // tm,),
        in_specs=[pl.BlockSpec((tm, k), lambda i: (i, 0)),
                  pl.BlockSpec((k, n), lambda i: (0, 0))],
        out_specs=pl.BlockSpec((tm, n), lambda i: (i, 0)),
        out_shape=jax.ShapeDtypeStruct((m, n), F32),
        compiler_params=_cparams(("parallel",)),
        name="matmul",
    )(x, w)


def _swa_proj_kernel(x_ref, w_ref, cos_ref, sin_ref, q_ref, k_ref, v_ref, qm_ref):
    proj = _dot(x_ref[...], w_ref[...])
    cos = cos_ref[...]
    sin = sin_ref[...]
    lane = lax.broadcasted_iota(jnp.int32, cos.shape, 1)
    first_half = (lane & (HEAD_DIM - 1)) < HEAD_DIM // 2

    def rope(xb):
        partner = jnp.where(first_half, pltpu.roll(xb, LANES - HEAD_DIM // 2, 1),
                            pltpu.roll(xb, HEAD_DIM // 2, 1))
        return xb * cos + partner * sin

    for j in range(SWA_QW // LANES):
        sl = slice(LANES * j, LANES * (j + 1))
        q_ref[:, sl] = (rope(proj[:, sl]) * HEAD_DIM ** -0.5).astype(q_ref.dtype)
    for j in range(SWA_KVW // LANES):
        k_ref[:, LANES * j:LANES * (j + 1)] = rope(proj[:, SWA_QW + LANES * j:SWA_QW + LANES * (j + 1)])
    v_ref[...] = proj[:, SWA_QW + SWA_KVW:SWA_QW + 2 * SWA_KVW]
    qm_ref[...] = proj[:, SWA_QW + 2 * SWA_KVW:].astype(qm_ref.dtype)


def _rope_tables(pos):
    half = HEAD_DIM // 2
    inv = ROPE_THETA ** (-jnp.arange(half, dtype=F32) / half)
    ang = pos.astype(F32)[:, None] * inv[None, :]
    cos = jnp.cos(ang)
    sin = jnp.sin(ang)
    reps = LANES // HEAD_DIM
    return jnp.tile(cos, (1, 2 * reps)), jnp.tile(jnp.concatenate([-sin, sin], axis=1), (1, reps))


def _swa_project(x, w, cos, sin, tm):
    m = x.shape[0]
    n_in = w.shape[1]
    tab_blocks = cos.shape[0] // tm
    row = lambda i: (i, 0)
    return pl.pallas_call(
        _swa_proj_kernel,
        grid=(m // tm,),
        in_specs=[pl.BlockSpec((tm, D_MODEL), row),
                  pl.BlockSpec((D_MODEL, n_in), lambda i: (0, 0)),
                  pl.BlockSpec((tm, LANES), lambda i: (i % tab_blocks, 0)),
                  pl.BlockSpec((tm, LANES), lambda i: (i % tab_blocks, 0))],
        out_specs=[pl.BlockSpec((tm, SWA_QW), row), pl.BlockSpec((tm, SWA_KVW), row),
                   pl.BlockSpec((tm, SWA_KVW), row), pl.BlockSpec((tm, MEM_WIDTH), row)],
        out_shape=[jax.ShapeDtypeStruct((m, SWA_QW), BF16),
                   jax.ShapeDtypeStruct((m, SWA_KVW), F32),
                   jax.ShapeDtypeStruct((m, SWA_KVW), F32),
                   jax.ShapeDtypeStruct((m, MEM_WIDTH), BF16)],
        compiler_params=_cparams(("parallel",)),
        name="swa_proj",
    )(x, w, cos, sin)


def _swa_attn_kernel(sink_ref, q_ref, kp_ref, kc_ref, vp_ref, vc_ref, o_ref):
    i = pl.program_id(1)
    blk = SWA_BLOCK
    q = q_ref[...]
    kcat = jnp.concatenate([kp_ref[...], kc_ref[...]], axis=0).astype(BF16)
    vcat = jnp.concatenate([vp_ref[...], vc_ref[...]], axis=0).astype(BF16)
    qi = lax.broadcasted_iota(jnp.int32, (blk, 2 * blk), 0)
    kj = lax.broadcasted_iota(jnp.int32, (blk, 2 * blk), 1)
    mask = (kj >= qi) & (kj <= qi + WINDOW) & ((kj >= blk) | (i > 0))
    mask = jnp.concatenate([mask] * SWA_GROUP, axis=0)
    outs = [None] * SWA_HEADS
    for kvh in range(SWA_KV_HEADS):
        kh = kcat[:, HEAD_DIM * kvh:HEAD_DIM * (kvh + 1)]
        vh = vcat[:, HEAD_DIM * kvh:HEAD_DIM * (kvh + 1)]
        heads = [SWA_GROUP * kvh + g for g in range(SWA_GROUP)]
        qg = jnp.concatenate([q[:, HEAD_DIM * h:HEAD_DIM * (h + 1)] for h in heads], axis=0)
        sink = jnp.concatenate([jnp.full((blk, 1), sink_ref[h], F32) for h in heads], axis=0)
        s = jnp.where(mask, _dot_nt(qg, kh), -jnp.inf)
        m = jnp.maximum(jnp.max(s, axis=-1, keepdims=True), sink)
        e = jnp.exp(s - m)
        den = jnp.sum(e, axis=-1, keepdims=True) + jnp.exp(sink - m)
        o = _dot(e * (1.0 / den), vh)
        for g, h in enumerate(heads):
            outs[h] = o[blk * g:blk * (g + 1)]
    o_ref[...] = jnp.concatenate(outs, axis=1).astype(o_ref.dtype)


def _swa_attention(q, k, v, sinks, batch, seq):
    nb = seq // SWA_BLOCK
    cur = lambda b, i: (b * nb + i, 0)
    prev = lambda b, i: (b * nb + jnp.maximum(i - 1, 0), 0)
    return pl.pallas_call(
        _swa_attn_kernel,
        grid=(batch, nb),
        in_specs=[pl.BlockSpec(memory_space=pltpu.SMEM),
                  pl.BlockSpec((SWA_BLOCK, SWA_QW), cur),
                  pl.BlockSpec((SWA_BLOCK, SWA_KVW), prev),
                  pl.BlockSpec((SWA_BLOCK, SWA_KVW), cur),
                  pl.BlockSpec((SWA_BLOCK, SWA_KVW), prev),
                  pl.BlockSpec((SWA_BLOCK, SWA_KVW), cur)],
        out_specs=pl.BlockSpec((SWA_BLOCK, SWA_QW), cur),
        out_shape=jax.ShapeDtypeStruct((batch * seq, SWA_QW), BF16),
        compiler_params=_cparams(("parallel", "parallel")),
        name="swa_attn",
    )(sinks, q, k, k, v, v)


def _mem_attn_kernel(q_ref, k_ref, v_ref, o_ref):
    q = q_ref[...]
    k = k_ref[0].astype(BF16)
    v = v_ref[0].astype(BF16)
    outs = []
    for h in range(MEM_HEADS):
        sl = slice(MEM_HEAD_DIM * h, MEM_HEAD_DIM * (h + 1))
        s = _dot_nt(q[:, sl], k[:, sl]) * MEM_HEAD_DIM ** -0.5
        e = jnp.exp(s - jnp.max(s, axis=-1, keepdims=True))
        p = e * (1.0 / jnp.sum(e, axis=-1, keepdims=True))
        outs.append(_dot(p, v[:, sl]))
    o_ref[...] = jnp.concatenate(outs, axis=1).astype(o_ref.dtype)


def _mem_attention(qm, mem_k, mem_v, batch, seq, tq):
    nq = seq // tq
    return pl.pallas_call(
        _mem_attn_kernel,
        grid=(batch, nq),
        in_specs=[pl.BlockSpec((tq, MEM_WIDTH), lambda b, i: (b * nq + i, 0)),
                  pl.BlockSpec((1, N_MEM, MEM_WIDTH), lambda b, i: (b, 0, 0)),
                  pl.BlockSpec((1, N_MEM, MEM_WIDTH), lambda b, i: (b, 0, 0))],
        out_specs=pl.BlockSpec((tq, MEM_WIDTH), lambda b, i: (b * nq + i, 0)),
        out_shape=jax.ShapeDtypeStruct((batch * seq, MEM_WIDTH), BF16),
        compiler_params=_cparams(("parallel", "parallel")),
        name="mem_attn",
    )(qm, mem_k, mem_v)


def _decode_attn_kernel(*refs, scale, with_self):
    if with_self:
        q_ref, k_ref, v_ref, seg_ref, kn_ref, vn_ref, sink_ref, o_ref, ko_ref, vo_ref = refs
    else:
        q_ref, k_ref, v_ref, seg_ref, o_ref = refs
    kc = k_ref[...]
    vc = v_ref[...]
    bs, n_keys, width = kc.shape
    seg = seg_ref[...]
    kc_r = _bf16_round(kc)
    vc_r = _bf16_round(vc)

    def head_sums(x2d):
        return _dot_f32(x2d, seg)

    for g in range(q_ref.shape[0]):
        qg = _bf16_round(q_ref[g].astype(F32) * scale)
        s = head_sums((kc_r * qg[:, None, :]).reshape(bs * n_keys, width)).reshape(bs, n_keys, width)
        m = jnp.max(s, axis=1)
        if with_self:
            kn = kn_ref[...]
            vn = vn_ref[...]
            s_self = head_sums(_bf16_round(kn) * qg)
            sink = sink_ref[g]
            m = jnp.maximum(jnp.maximum(m, s_self), sink)
        e = jnp.exp(s - m[:, None, :])
        den = jnp.sum(e, axis=1)
        if with_self:
            e_self = jnp.exp(s_self - m)
            den = den + e_self + jnp.exp(sink - m)
        inv = 1.0 / den
        o = jnp.sum(_bf16_round(e * inv[:, None, :]) * vc_r, axis=1)
        if with_self:
            o = o + _bf16_round(e_self * inv) * _bf16_round(vn)
        o_ref[g] = o
    if with_self:
        ko_ref[:, 0:n_keys - 1, :] = kc[:, 1:n_keys, :]
        ko_ref[:, n_keys - 1:n_keys, :] = kn[:, None, :]
        vo_ref[:, 0:n_keys - 1, :] = vc[:, 1:n_keys, :]
        vo_ref[:, n_keys - 1:n_keys, :] = vn[:, None, :]


def _head_segments(width, head_dim):
    lane = jnp.arange(width) // head_dim
    return (lane[:, None] == lane[None, :]).astype(F32)


def _decode_attention(q_groups, cache_k, cache_v, scale, head_dim, bs, new_k=None, new_v=None, sinks=None):
    n_groups, batch, width = q_groups.shape
    n_keys = cache_k.shape[1]
    with_self = new_k is not None
    seg = _head_segments(width, head_dim)
    blk3 = pl.BlockSpec((bs, n_keys, width), lambda i: (i, 0, 0))
    in_specs = [pl.BlockSpec((n_groups, bs, width), lambda i: (0, i, 0)), blk3, blk3,
                pl.BlockSpec((width, width), lambda i: (0, 0))]
    args = [q_groups, cache_k, cache_v, seg]
    out_specs = [pl.BlockSpec((n_groups, bs, width), lambda i: (0, i, 0))]
    out_shape = [jax.ShapeDtypeStruct((n_groups, batch, width), F32)]
    if with_self:
        in_specs += [pl.BlockSpec((bs, width), lambda i: (i, 0)), pl.BlockSpec((bs, width), lambda i: (i, 0)),
                     pl.BlockSpec((n_groups, 1, width), lambda i: (0, 0, 0))]
        args += [new_k, new_v, sinks]
        out_specs += [blk3, blk3]
        out_shape += [jax.ShapeDtypeStruct(cache_k.shape, F32), jax.ShapeDtypeStruct(cache_v.shape, F32)]
    return pl.pallas_call(
        functools.partial(_decode_attn_kernel, scale=scale, with_self=with_self),
        grid=(batch // bs,),
        in_specs=in_specs, out_specs=out_specs, out_shape=out_shape,
        compiler_params=_cparams(("parallel",)),
        name="decode_attn",
    )(*args)


def _outproj_kernel(x_ref, h_ref, m_ref, w1_ref, w2_ref, g_ref, b_ref, o_ref):
    t = _dot(h_ref[...], w1_ref[...]) + _dot(m_ref[...], w2_ref[...])
    o_ref[...] = _layer_norm(DEEPNORM_ALPHA * x_ref[...] + t, g_ref[...], b_ref[...])


def _out_project(x, h, mem, w_out, g, b, tm):
    m = x.shape[0]
    hw = h.shape[1]
    w1, w2 = w_out[:hw], w_out[hw:]
    row = lambda i: (i, 0)
    fixed = lambda i: (0, 0)
    return pl.pallas_call(
        _outproj_kernel,
        grid=(m // tm,),
        in_specs=[pl.BlockSpec((tm, D_MODEL), row), pl.BlockSpec((tm, hw), row),
                  pl.BlockSpec((tm, MEM_WIDTH), row), pl.BlockSpec(w1.shape, fixed),
                  pl.BlockSpec(w2.shape, fixed), pl.BlockSpec((1, D_MODEL), fixed),
                  pl.BlockSpec((1, D_MODEL), fixed)],
        out_specs=pl.BlockSpec((tm, D_MODEL), row),
        out_shape=jax.ShapeDtypeStruct((m, D_MODEL), F32),
        compiler_params=_cparams(("parallel",)),
        name="out_proj",
    )(x, h, mem, w1, w2, g.reshape(1, -1), b.reshape(1, -1))


def _router_gates(xb, rw):
    logits = _dot(xb, rw)
    lane = lax.broadcasted_iota(jnp.int32, logits.shape, 1)
    valid = lane < N_EXPERTS
    lg = jnp.where(valid, logits, -jnp.inf)
    ex = jnp.exp(lg - jnp.max(lg, axis=-1, keepdims=True))
    probs = ex / jnp.sum(ex, axis=-1, keepdims=True)
    cand = jnp.where(valid, probs, -1.0)
    p1 = jnp.max(cand, axis=-1, keepdims=True)
    i1 = jnp.min(jnp.where(cand == p1, lane, LANES), axis=-1, keepdims=True)
    cand = jnp.where(lane == i1, -1.0, cand)
    p2 = jnp.max(cand, axis=-1, keepdims=True)
    i2 = jnp.min(jnp.where(cand == p2, lane, LANES), axis=-1, keepdims=True)
    tot = p1 + p2
    return jnp.where(lane == i1, p1 / tot, 0.0) + jnp.where(lane == i2, p2 / tot, 0.0)


def _ffn_kernel(*refs, moe):
    if moe:
        x_ref, wg_ref, wu_ref, wd_ref, rw_ref, g_ref, b_ref, o_ref, xb_ref, acc_ref, gate_ref = refs
    else:
        x_ref, wg_ref, wu_ref, wd_ref, g_ref, b_ref, o_ref, xb_ref, acc_ref = refs
    e = pl.program_id(1)
    f = pl.program_id(2)

    @pl.when((e == 0) & (f == 0))
    def _():
        xb_ref[...] = x_ref[...].astype(xb_ref.dtype)
        acc_ref[...] = jnp.zeros_like(acc_ref)
        if moe:
            gate_ref[...] = _router_gates(xb_ref[...], rw_ref[...])

    xb = xb_ref[...]
    gt = _dot(xb, wg_ref[0])
    up = _dot(xb, wu_ref[0])
    y = _dot(jax.nn.silu(gt) * up, wd_ref[0])
    if moe:
        lane = lax.broadcasted_iota(jnp.int32, gate_ref.shape, 1)
        y = y * jnp.sum(jnp.where(lane == e, gate_ref[...], 0.0), axis=-1, keepdims=True)
    acc_ref[...] += y

    @pl.when((e == pl.num_programs(1) - 1) & (f == pl.num_programs(2) - 1))
    def _():
        o_ref[...] = _layer_norm(DEEPNORM_ALPHA * x_ref[...] + acc_ref[...], g_ref[...], b_ref[...])


def _ffn(x, w_gu, w_down, g, b, tm, tf, router_w=None):
    m = x.shape[0]
    n_exp, _, two_f = w_gu.shape
    nf = two_f // 2 // tf
    moe = router_w is not None
    row = lambda i, e, f: (i, 0)
    fixed = lambda i, e, f: (0, 0)
    in_specs = [pl.BlockSpec((tm, D_MODEL), row),
                pl.BlockSpec((1, D_MODEL, tf), lambda i, e, f: (e, 0, f)),
                pl.BlockSpec((1, D_MODEL, tf), lambda i, e, f: (e, 0, nf + f)),
                pl.BlockSpec((1, tf, D_MODEL), lambda i, e, f: (e, f, 0))]
    args = [x, w_gu, w_gu, w_down]
    scratch = [pltpu.VMEM((tm, D_MODEL), BF16), pltpu.VMEM((tm, D_MODEL), F32)]
    if moe:
        rw = jnp.pad(router_w, ((0, 0), (0, LANES - N_EXPERTS))).astype(BF16)
        in_specs.append(pl.BlockSpec((D_MODEL, LANES), fixed))
        args.append(rw)
        scratch.append(pltpu.VMEM((tm, LANES), F32))
    in_specs += [pl.BlockSpec((1, D_MODEL), fixed), pl.BlockSpec((1, D_MODEL), fixed)]
    args += [g.reshape(1, -1), b.reshape(1, -1)]
    return pl.pallas_call(
        functools.partial(_ffn_kernel, moe=moe),
        grid=(m // tm, n_exp, nf),
        in_specs=in_specs,
        out_specs=pl.BlockSpec((tm, D_MODEL), row),
        out_shape=jax.ShapeDtypeStruct((m, D_MODEL), F32),
        scratch_shapes=scratch,
        compiler_params=_cparams(("parallel", "arbitrary", "arbitrary")),
        name="moe_ffn" if moe else "ffn",
    )(*args)


def _gdn_post(conv_out, proj, alog_ref, dtb_ref, q_ref, k_ref, v_ref, z_ref, gb_ref, qm_ref):
    c = jax.nn.silu(conv_out)
    for h in range(GDN_HEADS):
        sl = slice(GDN_DK * h, GDN_DK * (h + 1))
        qh = c[:, sl]
        kh = c[:, GDN_QK + GDN_DK * h:GDN_QK + GDN_DK * (h + 1)]
        q_ref[:, sl] = qh * lax.rsqrt(jnp.sum(qh * qh, axis=-1, keepdims=True) + L2_EPS) * GDN_DK ** -0.5
        k_ref[:, sl] = kh * lax.rsqrt(jnp.sum(kh * kh, axis=-1, keepdims=True) + L2_EPS)
    v_ref[...] = c[:, 2 * GDN_QK:]
    z_ref[...] = proj[:, GDN_CONV_DIM:GDN_CONV_DIM + GDN_VW]
    ab = proj[:, GDN_CONV_DIM + GDN_VW:GDN_CONV_DIM + GDN_VW + GDN_AB_PAD]
    lane = lax.broadcasted_iota(jnp.int32, ab.shape, 1)
    decay = -jnp.exp(alog_ref[...]) * jax.nn.softplus(ab + dtb_ref[...])
    gb_ref[...] = jnp.where(lane < GDN_HEADS, decay, jax.nn.sigmoid(ab))
    qm_ref[...] = proj[:, GDN_CONV_DIM + GDN_VW + GDN_AB_PAD:].astype(qm_ref.dtype)


def _gdn_proj_kernel(x_ref, w_ref, cw_ref, alog_ref, dtb_ref,
                     q_ref, k_ref, v_ref, z_ref, gb_ref, qm_ref, cs_ref, buf_ref):
    i = pl.program_id(1)
    tm = x_ref.shape[0]
    pad = SUBLANES
    proj = _dot(x_ref[...], w_ref[...])
    qkv = proj[:, :GDN_CONV_DIM]

    @pl.when(i == 0)
    def _():
        buf_ref[0:pad, :] = jnp.zeros((pad, GDN_CONV_DIM), F32)

    buf_ref[pad:pad + tm, :] = qkv
    cw = cw_ref[...]
    conv = buf_ref[pad - 3:pad - 3 + tm, :] * cw[0:1]
    conv = conv + buf_ref[pad - 2:pad - 2 + tm, :] * cw[1:2]
    conv = conv + buf_ref[pad - 1:pad - 1 + tm, :] * cw[2:3]
    conv = conv + qkv * cw[3:4]
    tail = buf_ref[tm:tm + pad, :]
    buf_ref[0:pad, :] = tail
    cs_ref[0] = tail
    _gdn_post(conv, proj, alog_ref, dtb_ref, q_ref, k_ref, v_ref, z_ref, gb_ref, qm_ref)


def _gdn_gate_params(a_log, dt_bias):
    padv = lambda v: jnp.pad(v.astype(F32), (0, GDN_AB_PAD - GDN_HEADS)).reshape(1, GDN_AB_PAD)
    return padv(a_log), padv(dt_bias)


def _gdn_pad_w_in(w_in):
    o3 = GDN_CONV_DIM + GDN_VW
    o4 = o3 + 2 * GDN_HEADS
    ab = jnp.pad(w_in[:, o3:o4], ((0, 0), (0, GDN_AB_PAD - 2 * GDN_HEADS)))
    return jnp.concatenate([w_in[:, :o3], ab, w_in[:, o4:]], axis=1)


def _gdn_project(x, w_pad, conv_w, a_log, dt_bias, batch, seq, tm):
    m = x.shape[0]
    nt = seq // tm
    row = lambda b, i: (b * nt + i, 0)
    fixed = lambda b, i: (0, 0)
    alog, dtb = _gdn_gate_params(a_log, dt_bias)
    wide = jax.ShapeDtypeStruct((m, GDN_QK), F32)
    return pl.pallas_call(
        _gdn_proj_kernel,
        grid=(batch, nt),
        in_specs=[pl.BlockSpec((tm, D_MODEL), row), pl.BlockSpec(w_pad.shape, fixed),
                  pl.BlockSpec(conv_w.shape, fixed), pl.BlockSpec(alog.shape, fixed),
                  pl.BlockSpec(dtb.shape, fixed)],
        out_specs=[pl.BlockSpec((tm, GDN_QK), row), pl.BlockSpec((tm, GDN_QK), row),
                   pl.BlockSpec((tm, GDN_VW), row), pl.BlockSpec((tm, GDN_VW), row),
                   pl.BlockSpec((tm, GDN_AB_PAD), row), pl.BlockSpec((tm, MEM_WIDTH), row),
                   pl.BlockSpec((1, SUBLANES, GDN_CONV_DIM), lambda b, i: (b, 0, 0))],
        out_shape=[wide, wide, wide, wide, jax.ShapeDtypeStruct((m, GDN_AB_PAD), F32),
                   jax.ShapeDtypeStruct((m, MEM_WIDTH), BF16),
                   jax.ShapeDtypeStruct((batch, SUBLANES, GDN_CONV_DIM), F32)],
        scratch_shapes=[pltpu.VMEM((tm + SUBLANES, GDN_CONV_DIM), F32)],
        compiler_params=_cparams(("parallel", "arbitrary")),
        name="gdn_proj",
    )(x, w_pad, conv_w, alog, dtb)


def _gdn_step_proj_kernel(x_ref, w_ref, cs_ref, cw_ref, alog_ref, dtb_ref,
                          pre_ref, q_ref, k_ref, v_ref, z_ref, gb_ref, qm_ref):
    proj = _dot(x_ref[...], w_ref[...])
    qkv = proj[:, :GDN_CONV_DIM]
    pre_ref[...] = qkv
    cw = cw_ref[...]
    conv = cs_ref[0] * cw[0:1]
    conv = conv + cs_ref[1] * cw[1:2]
    conv = conv + cs_ref[2] * cw[2:3]
    conv = conv + qkv * cw[3:4]
    _gdn_post(conv, proj, alog_ref, dtb_ref, q_ref, k_ref, v_ref, z_ref, gb_ref, qm_ref)


def _gdn_step_project(x, w_pad, conv_state, conv_w, a_log, dt_bias):
    m = x.shape[0]
    alog, dtb = _gdn_gate_params(a_log, dt_bias)
    cs = jnp.transpose(conv_state, (1, 0, 2))
    wide = jax.ShapeDtypeStruct((m, GDN_QK), F32)
    return pl.pallas_call(
        _gdn_step_proj_kernel,
        out_shape=[jax.ShapeDtypeStruct((m, GDN_CONV_DIM), F32), wide, wide, wide, wide,
                   jax.ShapeDtypeStruct((m, GDN_AB_PAD), F32), jax.ShapeDtypeStruct((m, MEM_WIDTH), F32)],
        compiler_params=pltpu.CompilerParams(vmem_limit_bytes=VMEM_LIMIT_MB * 1024 * 1024),
        name="gdn_step_proj",
    )(x, w_pad, cs, conv_w, alog, dtb)


def _gated_out(o, z, nw):
    on = o * lax.rsqrt(jnp.mean(o * o, axis=-1, keepdims=True) + RMS_EPS) * nw
    return on * jax.nn.silu(z)


def _gdn_chunk_kernel(q_ref, k_ref, v_ref, z_ref, gb_ref, nw_ref, o_ref, sfin_ref, s_ref):
    n = pl.program_id(1)
    c = GDN_CHUNK

    @pl.when(n == 0)
    def _():
        s_ref[...] = jnp.zeros_like(s_ref)

    r = lax.broadcasted_iota(jnp.int32, (c, c), 0)
    col = lax.broadcasted_iota(jnp.int32, (c, c), 1)
    causal = r >= col
    strict = r > col
    tri = causal.astype(F32)
    eye = (r == col).astype(F32)
    gb = gb_ref[...]
    gc_cols = _dot_f32(tri, gb)
    gc_rows = lax.dot_general(gb, tri, (((0,), (1,)), ((), ())), preferred_element_type=F32,
                              precision=lax.Precision.HIGHEST)
    nw = nw_ref[...]
    for h in range(GDN_HEADS):
        sl = slice(GDN_DK * h, GDN_DK * (h + 1))
        q = q_ref[:, sl]
        k = k_ref[:, sl]
        v = v_ref[:, sl]
        gcol = gc_cols[:, h:h + 1]
        grow = gc_rows[h:h + 1, :]
        beta = gb[:, GDN_HEADS + h:GDN_HEADS + h + 1]
        decay = jnp.exp(jnp.where(causal, gcol - grow, -jnp.inf))
        kb = k * beta
        a = jnp.where(strict, _dot_nt(kb, k) * decay, 0.0)
        tinv = eye - a
        apow = a
        for _ in range(5):
            apow = _dot(apow, apow)
            tinv = _dot(tinv, eye + apow)
        eg = jnp.exp(gcol)
        u = _dot(tinv, v * beta)
        w = _dot(tinv, kb * eg)
        qk = jnp.where(causal, _dot_nt(q, k) * decay, 0.0)
        state = s_ref[h]
        v_new = u - _dot(w, state)
        o = _dot(q * eg, state) + _dot(qk, v_new)
        g_last = gcol[c - 1:c, :]
        s_ref[h] = state * jnp.exp(g_last) + _dot_tn(k * jnp.exp(g_last - gcol), v_new)
        o_ref[:, sl] = _gated_out(o, z_ref[:, sl], nw).astype(o_ref.dtype)

    @pl.when(n == pl.num_programs(1) - 1)
    def _():
        sfin_ref[0] = s_ref[...]


def _gdn_chunked(q, k, v, z, gb, norm_w, batch, seq):
    nc = seq // GDN_CHUNK
    row = lambda b, n: (b * nc + n, 0)
    wide = pl.BlockSpec((GDN_CHUNK, GDN_QK), row)
    return pl.pallas_call(
        _gdn_chunk_kernel,
        grid=(batch, nc),
        in_specs=[wide, wide, wide, wide, pl.BlockSpec((GDN_CHUNK, GDN_AB_PAD), row),
                  pl.BlockSpec((1, GDN_DV), lambda b, n: (0, 0))],
        out_specs=[wide, pl.BlockSpec((1, GDN_HEADS, GDN_DK, GDN_DV), lambda b, n: (b, 0, 0, 0))],
        out_shape=[jax.ShapeDtypeStruct((batch * seq, GDN_VW), BF16),
                   jax.ShapeDtypeStruct((batch, GDN_HEADS, GDN_DK, GDN_DV), F32)],
        scratch_shapes=[pltpu.VMEM((GDN_HEADS, GDN_DK, GDN_DV), F32)],
        compiler_params=_cparams(("parallel", "arbitrary")),
        name="gdn_chunk",
    )(q, k, v, z, gb, norm_w.reshape(1, -1))


def _gdn_recurrent_kernel(q_ref, k_ref, v_ref, z_ref, gb_ref, nw_ref, s_ref, o_ref, so_ref):
    bs = q_ref.shape[0]
    gb = gb_ref[...]
    nw = nw_ref[...]
    for h in range(GDN_HEADS):
        sl = slice(GDN_DK * h, GDN_DK * (h + 1))
        q_t = q_ref[:, sl].T
        k_t = k_ref[:, sl].T
        v = v_ref[:, sl]
        rows = []
        for b in range(bs):
            state = s_ref[b, h] * jnp.exp(gb[b:b + 1, h:h + 1])
            kcol = k_t[:, b:b + 1]
            v_new = (v[b:b + 1, :] - jnp.sum(kcol * state, axis=0, keepdims=True)) \
                * gb[b:b + 1, GDN_HEADS + h:GDN_HEADS + h + 1]
            state = state + kcol * v_new
            so_ref[b, h] = state
            rows.append(jnp.sum(q_t[:, b:b + 1] * state, axis=0, keepdims=True))
        o_ref[:, sl] = _gated_out(jnp.concatenate(rows, axis=0), z_ref[:, sl], nw).astype(o_ref.dtype)


def _gdn_recurrent(q, k, v, z, gb, norm_w, state, bs):
    batch = q.shape[0]
    row = lambda i: (i, 0)
    wide = pl.BlockSpec((bs, GDN_QK), row)
    st = pl.BlockSpec((bs, GDN_HEADS, GDN_DK, GDN_DV), lambda i: (i, 0, 0, 0))
    return pl.pallas_call(
        _gdn_recurrent_kernel,
        grid=(batch // bs,),
        in_specs=[wide, wide, wide, wide, pl.BlockSpec((bs, GDN_AB_PAD), row),
                  pl.BlockSpec((1, GDN_DV), lambda i: (0, 0)), st],
        out_specs=[wide, st],
        out_shape=[jax.ShapeDtypeStruct((batch, GDN_VW), F32), jax.ShapeDtypeStruct(state.shape, F32)],
        compiler_params=_cparams(("parallel",)),
        name="gdn_recurrent",
    )(q, k, v, z, gb, norm_w.reshape(1, -1), state)


PROMPT_TM = 512
GDN_TM = 256
FFN_TM = 1024
FFN_TF = 512
DECODE_BS = 8


def kernel(x_prompt, x_sample, cache_swa_k, cache_swa_v, state_gdn_conv, state_gdn_rec, cache_mem_k, cache_mem_v, mem_prompt, w_in_swa, swa_sinks, w_in_gdn, gdn_conv_w, gdn_a_log, gdn_dt_bias, gdn_norm_w, w_mem_kv, w_out, ln1_g, ln1_b, ln2_g, ln2_b, ffn_w_gu, ffn_w_down, router_w, moe_w_gu, moe_w_down):
    batch, seq, _ = x_prompt.shape
    dec = x_sample.shape[0]
    assert x_sample.shape[1] == 1
    yp = x_prompt.reshape(batch * seq, D_MODEL)
    ys = x_sample.reshape(dec, D_MODEL)
    mem_flat = mem_prompt.reshape(batch * N_MEM, D_MODEL)
    mem3 = lambda a: a.reshape(-1, N_MEM, MEM_WIDTH)

    w_in0 = w_in_swa[0].astype(BF16)
    w_out0 = w_out[0].astype(BF16)
    w_gu0 = ffn_w_gu[0:1].astype(BF16)
    w_dn0 = ffn_w_down[0:1].astype(BF16)
    mkv = _matmul(mem_flat, w_mem_kv[0].astype(BF16), PROMPT_TM)
    mk0, mv0 = mkv[:, :MEM_WIDTH], mkv[:, MEM_WIDTH:]
    cos_p, sin_p = _rope_tables(jnp.arange(seq, dtype=jnp.int32))
    q, k, v, qm = _swa_project(yp, w_in0, cos_p, sin_p, PROMPT_TM)
    h = _swa_attention(q, k, v, swa_sinks[0], batch, seq)
    ma = _mem_attention(qm, mem3(mk0), mem3(mv0), batch, seq, PROMPT_TM)
    swa_kp = k.reshape(batch, seq, SWA_KV_HEADS, HEAD_DIM)[:, -WINDOW:]
    swa_vp = v.reshape(batch, seq, SWA_KV_HEADS, HEAD_DIM)[:, -WINDOW:]
    yp = _out_project(yp, h, ma, w_out0, ln1_g[0], ln1_b[0], PROMPT_TM)
    yp = _ffn(yp, w_gu0, w_dn0, ln2_g[0], ln2_b[0], FFN_TM, FFN_TF)

    cos_s, sin_s = _rope_tables(jnp.full((dec,), PAST_LEN, jnp.int32))
    q, k, v, qm = _swa_project(ys, w_in0, cos_s, sin_s, dec)
    win = cache_swa_k.shape[2]
    qg = q.reshape(dec, SWA_KV_HEADS, SWA_GROUP, HEAD_DIM).transpose(2, 0, 1, 3).reshape(SWA_GROUP, dec, SWA_KVW)
    sink_g = jnp.repeat(swa_sinks[0].reshape(SWA_KV_HEADS, SWA_GROUP).T, HEAD_DIM, axis=1).reshape(
        SWA_GROUP, 1, SWA_KVW)
    og, swa_ks, swa_vs = _decode_attention(
        qg, cache_swa_k[0].reshape(dec, win, SWA_KVW), cache_swa_v[0].reshape(dec, win, SWA_KVW),
        1.0, HEAD_DIM, DECODE_BS, new_k=k, new_v=v, sinks=sink_g)
    h = og.reshape(SWA_GROUP, dec, SWA_KV_HEADS, HEAD_DIM).transpose(1, 2, 0, 3).reshape(dec, SWA_QW)
    ma, = _decode_attention(qm.reshape(1, dec, MEM_WIDTH), mem3(cache_mem_k[0]), mem3(cache_mem_v[0]),
                            MEM_HEAD_DIM ** -0.5, MEM_HEAD_DIM, DECODE_BS)
    ys = _out_project(ys, h, ma[0], w_out0, ln1_g[0], ln1_b[0], dec)
    ys = _ffn(ys, w_gu0, w_dn0, ln2_g[0], ln2_b[0], dec, FFN_TF)

    w_in1 = _gdn_pad_w_in(w_in_gdn[0]).astype(BF16)
    w_out1 = w_out[1].astype(BF16)
    w_gu1 = moe_w_gu[0].astype(BF16)
    w_dn1 = moe_w_down[0].astype(BF16)
    mkv = _matmul(mem_flat, w_mem_kv[1].astype(BF16), PROMPT_TM)
    mk1, mv1 = mkv[:, :MEM_WIDTH], mkv[:, MEM_WIDTH:]
    q, k, v, z, gb, qm, conv_tail = _gdn_project(yp, w_in1, gdn_conv_w[0], gdn_a_log[0], gdn_dt_bias[0],
                                                 batch, seq, GDN_TM)
    h, rec_p = _gdn_chunked(q, k, v, z, gb, gdn_norm_w[0], batch, seq)
    ma = _mem_attention(qm, mem3(mk1), mem3(mv1), batch, seq, PROMPT_TM)
    yp = _out_project(yp, h, ma, w_out1, ln1_g[1], ln1_b[1], PROMPT_TM)
    yp = _ffn(yp, w_gu1, w_dn1, ln2_g[1], ln2_b[1], FFN_TM, FFN_TF, router_w=router_w[0])

    pre, q, k, v, z, gb, qm = _gdn_step_project(ys, w_in1, state_gdn_conv[0], gdn_conv_w[0],
                                                gdn_a_log[0], gdn_dt_bias[0])
    h, rec_s = _gdn_recurrent(q, k, v, z, gb, gdn_norm_w[0], state_gdn_rec[0], DECODE_BS)
    ma, = _decode_attention(qm.reshape(1, dec, MEM_WIDTH), mem3(cache_mem_k[1]), mem3(cache_mem_v[1]),
                            MEM_HEAD_DIM ** -0.5, MEM_HEAD_DIM, DECODE_BS)
    ys = _out_project(ys, h, ma[0], w_out1, ln1_g[1], ln1_b[1], dec)
    ys = _ffn(ys, w_gu1, w_dn1, ln2_g[1], ln2_b[1], dec, FFN_TF, router_w=router_w[0])

    mem_shape = (batch, N_MEM, MEM_HEADS, MEM_HEAD_DIM)
    conv_s = jnp.concatenate([state_gdn_conv[0][:, 1:], pre[:, None, :]], axis=1)
    return (yp.reshape(batch, seq, D_MODEL), ys.reshape(dec, 1, D_MODEL),
            swa_kp[None], swa_vp[None],
            conv_tail[None, :, SUBLANES - (GDN_CONV_W - 1):, :], rec_p[None],
            jnp.stack([mk0.reshape(mem_shape), mk1.reshape(mem_shape)]),
            jnp.stack([mv0.reshape(mem_shape), mv1.reshape(mem_shape)]),
            swa_ks.reshape(1, dec, win, SWA_KV_HEADS, HEAD_DIM), swa_vs.reshape(1, dec, win, SWA_KV_HEADS, HEAD_DIM),
            conv_s[None], rec_s[None])
```

```python
import functools

import jax
import jax.numpy as jnp
from jax import lax
from jax.experimental import pallas as pl
from jax.experimental.pallas import tpu as pltpu

F32 = jnp.float32
BF16 = jnp.bfloat16

D_MODEL = 1024
DEPTH = 2
PAST_LEN = 16384
SWA_HEADS = 12
SWA_KV_HEADS = 4
SWA_GROUP = SWA_HEADS // SWA_KV_HEADS
HEAD_DIM = 64
WINDOW = 128
SWA_BLOCK = 128
ROPE_THETA = 10000.0
SWA_QW = SWA_HEADS * HEAD_DIM
SWA_KVW = SWA_KV_HEADS * HEAD_DIM
GDN_HEADS = 6
GDN_DK = 128
GDN_DV = 128
GDN_CONV_W = 4
GDN_CHUNK = 64
GDN_QK = GDN_HEADS * GDN_DK
GDN_VW = GDN_HEADS * GDN_DV
GDN_CONV_DIM = 2 * GDN_QK + GDN_VW
N_MEM = 256
MEM_HEADS = 4
MEM_HEAD_DIM = 64
MEM_WIDTH = MEM_HEADS * MEM_HEAD_DIM
D_FF = 3584
N_EXPERTS = 8
DEEPNORM_ALPHA = (2 * DEPTH) ** 0.25
LN_EPS = 1e-5
RMS_EPS = 1e-6
L2_EPS = 1e-6

LANES = 128
SUBLANES = 8
GDN_AB_PAD = LANES
VMEM_LIMIT_MB = 56


def _cparams(sem, vmem_mb=VMEM_LIMIT_MB):
    return pltpu.CompilerParams(dimension_semantics=sem, vmem_limit_bytes=vmem_mb * 1024 * 1024)


def _bf16_round(x):
    return x.astype(BF16).astype(F32)


def _dot(a, b):
    return jnp.dot(a.astype(BF16), b.astype(BF16), preferred_element_type=F32)


def _dot_nt(a, b):
    return lax.dot_general(a.astype(BF16), b.astype(BF16), (((1,), (1,)), ((), ())),
                           preferred_element_type=F32)


def _dot_tn(a, b):
    return lax.dot_general(a.astype(BF16), b.astype(BF16), (((0,), (0,)), ((), ())),
                           preferred_element_type=F32)


def _dot_f32(a, b):
    return jnp.dot(a, b, preferred_element_type=F32, precision=lax.Precision.HIGHEST)


def _layer_norm(t, g, b):
    mu = jnp.mean(t, axis=-1, keepdims=True)
    d = t - mu
    var = jnp.mean(d * d, axis=-1, keepdims=True)
    return d * lax.rsqrt(var + LN_EPS) * g + b


def _mm_kernel(x_ref, w_ref, o_ref):
    o_ref[...] = _dot(x_ref[...], w_ref[...])


def _matmul(x, w, tm):
    m, k = x.shape
    n = w.shape[1]
    return pl.pallas_call(
        _mm_kernel,
        grid=(m // tm,),
        in_specs=[pl.BlockSpec((tm, k), lambda i: (i, 0)),
                  pl.BlockSpec((k, n), lambda i: (0, 0))],
        out_specs=pl.BlockSpec((tm, n), lambda i: (i, 0)),
        out_shape=jax.ShapeDtypeStruct((m, n), F32),
        compiler_params=_cparams(("parallel",)),
        name="matmul",
    )(x, w)


def _swa_proj_kernel(x_ref, w_ref, cos_ref, sin_ref, q_ref, k_ref, v_ref, qm_ref):
    proj = _dot(x_ref[...], w_ref[...])
    cos = cos_ref[...]
    sin = sin_ref[...]
    lane = lax.broadcasted_iota(jnp.int32, cos.shape, 1)
    first_half = (lane & (HEAD_DIM - 1)) < HEAD_DIM // 2

    def rope(xb):
        partner = jnp.where(first_half, pltpu.roll(xb, LANES - HEAD_DIM // 2, 1),
                            pltpu.roll(xb, HEAD_DIM // 2, 1))
        return xb * cos + partner * sin

    for j in range(SWA_QW // LANES):
        sl = slice(LANES * j, LANES * (j + 1))
        q_ref[:, sl] = (rope(proj[:, sl]) * HEAD_DIM ** -0.5).astype(q_ref.dtype)
    for j in range(SWA_KVW // LANES):
        k_ref[:, LANES * j:LANES * (j + 1)] = rope(proj[:, SWA_QW + LANES * j:SWA_QW + LANES * (j + 1)])
    v_ref[...] = proj[:, SWA_QW + SWA_KVW:SWA_QW + 2 * SWA_KVW]
    qm_ref[...] = proj[:, SWA_QW + 2 * SWA_KVW:].astype(qm_ref.dtype)


def _rope_tables(pos):
    half = HEAD_DIM // 2
    inv = ROPE_THETA ** (-jnp.arange(half, dtype=F32) / half)
    ang = pos.astype(F32)[:, None] * inv[None, :]
    cos = jnp.cos(ang)
    sin = jnp.sin(ang)
    reps = LANES // HEAD_DIM
    return jnp.tile(cos, (1, 2 * reps)), jnp.tile(jnp.concatenate([-sin, sin], axis=1), (1, reps))


def _swa_project(x, w, cos, sin, tm):
    m = x.shape[0]
    n_in = w.shape[1]
    tab_blocks = cos.shape[0] // tm
    row = lambda i: (i, 0)
    return pl.pallas_call(
        _swa_proj_kernel,
        grid=(m // tm,),
        in_specs=[pl.BlockSpec((tm, D_MODEL), row),
                  pl.BlockSpec((D_MODEL, n_in), lambda i: (0, 0)),
                  pl.BlockSpec((tm, LANES), lambda i: (i % tab_blocks, 0)),
                  pl.BlockSpec((tm, LANES), lambda i: (i % tab_blocks, 0))],
        out_specs=[pl.BlockSpec((tm, SWA_QW), row), pl.BlockSpec((tm, SWA_KVW), row),
                   pl.BlockSpec((tm, SWA_KVW), row), pl.BlockSpec((tm, MEM_WIDTH), row)],
        out_shape=[jax.ShapeDtypeStruct((m, SWA_QW), BF16),
                   jax.ShapeDtypeStruct((m, SWA_KVW), F32),
                   jax.ShapeDtypeStruct((m, SWA_KVW), F32),
                   jax.ShapeDtypeStruct((m, MEM_WIDTH), BF16)],
        compiler_params=_cparams(("parallel",)),
        name="swa_proj",
    )(x, w, cos, sin)


def _swa_attn_kernel(sink_ref, q_ref, kp_ref, kc_ref, vp_ref, vc_ref, o_ref):
    i = pl.program_id(1)
    blk = SWA_BLOCK
    q = q_ref[...]
    kcat = jnp.concatenate([kp_ref[...], kc_ref[...]], axis=0).astype(BF16)
    vcat = jnp.concatenate([vp_ref[...], vc_ref[...]], axis=0).astype(BF16)
    qi = lax.broadcasted_iota(jnp.int32, (blk, 2 * blk), 0)
    kj = lax.broadcasted_iota(jnp.int32, (blk, 2 * blk), 1)
    mask = (kj >= qi) & (kj <= qi + WINDOW) & ((kj >= blk) | (i > 0))
    mask = jnp.concatenate([mask] * SWA_GROUP, axis=0)
    kvs = range(SWA_KV_HEADS)
    heads = [[SWA_GROUP * kvh + g for g in range(SWA_GROUP)] for kvh in kvs]
    kh = [kcat[:, HEAD_DIM * kvh:HEAD_DIM * (kvh + 1)] for kvh in kvs]
    vh = [vcat[:, HEAD_DIM * kvh:HEAD_DIM * (kvh + 1)] for kvh in kvs]
    qg = [jnp.concatenate([q[:, HEAD_DIM * h:HEAD_DIM * (h + 1)] for h in heads[kvh]], axis=0) for kvh in kvs]
    sink = [jnp.concatenate([jnp.full((blk, 1), sink_ref[h], F32) for h in heads[kvh]], axis=0) for kvh in kvs]
    s = [jnp.where(mask, _dot_nt(qg[kvh], kh[kvh]), -jnp.inf) for kvh in kvs]
    m = [jnp.maximum(jnp.max(s[kvh], axis=-1, keepdims=True), sink[kvh]) for kvh in kvs]
    e = [jnp.exp(s[kvh] - m[kvh]) for kvh in kvs]
    den = [jnp.sum(e[kvh], axis=-1, keepdims=True) + jnp.exp(sink[kvh] - m[kvh]) for kvh in kvs]
    o = [_dot(e[kvh] * (1.0 / den[kvh]), vh[kvh]) for kvh in kvs]
    outs = [o[kvh][blk * g:blk * (g + 1)] for kvh in kvs for g in range(SWA_GROUP)]
    o_ref[...] = jnp.concatenate(outs, axis=1).astype(o_ref.dtype)


def _swa_attention(q, k, v, sinks, batch, seq):
    nb = seq // SWA_BLOCK
    cur = lambda b, i: (b * nb + i, 0)
    prev = lambda b, i: (b * nb + jnp.maximum(i - 1, 0), 0)
    return pl.pallas_call(
        _swa_attn_kernel,
        grid=(batch, nb),
        in_specs=[pl.BlockSpec(memory_space=pltpu.SMEM),
                  pl.BlockSpec((SWA_BLOCK, SWA_QW), cur),
                  pl.BlockSpec((SWA_BLOCK, SWA_KVW), prev),
                  pl.BlockSpec((SWA_BLOCK, SWA_KVW), cur),
                  pl.BlockSpec((SWA_BLOCK, SWA_KVW), prev),
                  pl.BlockSpec((SWA_BLOCK, SWA_KVW), cur)],
        out_specs=pl.BlockSpec((SWA_BLOCK, SWA_QW), cur),
        out_shape=jax.ShapeDtypeStruct((batch * seq, SWA_QW), BF16),
        compiler_params=_cparams(("parallel", "parallel")),
        name="swa_attn",
    )(sinks, q, k, k, v, v)


def _mem_attn_kernel(q_ref, k_ref, v_ref, o_ref):
    q = q_ref[...]
    k = k_ref[0].astype(BF16)
    v = v_ref[0].astype(BF16)
    heads = range(MEM_HEADS)
    sls = [slice(MEM_HEAD_DIM * h, MEM_HEAD_DIM * (h + 1)) for h in heads]
    s = [_dot_nt(q[:, sl], k[:, sl]) * MEM_HEAD_DIM ** -0.5 for sl in sls]
    e = [jnp.exp(s[h] - jnp.max(s[h], axis=-1, keepdims=True)) for h in heads]
    p = [e[h] * (1.0 / jnp.sum(e[h], axis=-1, keepdims=True)) for h in heads]
    outs = [_dot(p[h], v[:, sls[h]]) for h in heads]
    o_ref[...] = jnp.concatenate(outs, axis=1).astype(o_ref.dtype)


def _mem_attention(qm, mem_k, mem_v, batch, seq, tq):
    nq = seq // tq
    return pl.pallas_call(
        _mem_attn_kernel,
        grid=(batch, nq),
        in_specs=[pl.BlockSpec((tq, MEM_WIDTH), lambda b, i: (b * nq + i, 0)),
                  pl.BlockSpec((1, N_MEM, MEM_WIDTH), lambda b, i: (b, 0, 0)),
                  pl.BlockSpec((1, N_MEM, MEM_WIDTH), lambda b, i: (b, 0, 0))],
        out_specs=pl.BlockSpec((tq, MEM_WIDTH), lambda b, i: (b * nq + i, 0)),
        out_shape=jax.ShapeDtypeStruct((batch * seq, MEM_WIDTH), BF16),
        compiler_params=_cparams(("parallel", "parallel")),
        name="mem_attn",
    )(qm, mem_k, mem_v)


def _decode_attn_kernel(*refs, scale, with_self):
    if with_self:
        q_ref, k_ref, v_ref, seg_ref, kn_ref, vn_ref, sink_ref, o_ref, ko_ref, vo_ref = refs
    else:
        q_ref, k_ref, v_ref, seg_ref, o_ref = refs
    kc = k_ref[...]
    vc = v_ref[...]
    bs, n_keys, width = kc.shape
    seg = seg_ref[...]
    kc_r = _bf16_round(kc)
    vc_r = _bf16_round(vc)

    def head_sums(x2d):
        return _dot_f32(x2d, seg)

    for g in range(q_ref.shape[0]):
        qg = _bf16_round(q_ref[g].astype(F32) * scale)
        s = head_sums((kc_r * qg[:, None, :]).reshape(bs * n_keys, width)).reshape(bs, n_keys, width)
        m = jnp.max(s, axis=1)
        if with_self:
            kn = kn_ref[...]
            vn = vn_ref[...]
            s_self = head_sums(_bf16_round(kn) * qg)
            sink = sink_ref[g]
            m = jnp.maximum(jnp.maximum(m, s_self), sink)
        e = jnp.exp(s - m[:, None, :])
        den = jnp.sum(e, axis=1)
        if with_self:
            e_self = jnp.exp(s_self - m)
            den = den + e_self + jnp.exp(sink - m)
        inv = 1.0 / den
        o = jnp.sum(_bf16_round(e * inv[:, None, :]) * vc_r, axis=1)
        if with_self:
            o = o + _bf16_round(e_self * inv) * _bf16_round(vn)
        o_ref[g] = o
    if with_self:
        ko_ref[:, 0:n_keys - 1, :] = kc[:, 1:n_keys, :]
        ko_ref[:, n_keys - 1:n_keys, :] = kn[:, None, :]
        vo_ref[:, 0:n_keys - 1, :] = vc[:, 1:n_keys, :]
        vo_ref[:, n_keys - 1:n_keys, :] = vn[:, None, :]


def _head_segments(width, head_dim):
    lane = jnp.arange(width) // head_dim
    return (lane[:, None] == lane[None, :]).astype(F32)


def _decode_attention(q_groups, cache_k, cache_v, scale, head_dim, bs, new_k=None, new_v=None, sinks=None):
    n_groups, batch, width = q_groups.shape
    n_keys = cache_k.shape[1]
    with_self = new_k is not None
    seg = _head_segments(width, head_dim)
    blk3 = pl.BlockSpec((bs, n_keys, width), lambda i: (i, 0, 0))
    in_specs = [pl.BlockSpec((n_groups, bs, width), lambda i: (0, i, 0)), blk3, blk3,
                pl.BlockSpec((width, width), lambda i: (0, 0))]
    args = [q_groups, cache_k, cache_v, seg]
    out_specs = [pl.BlockSpec((n_groups, bs, width), lambda i: (0, i, 0))]
    out_shape = [jax.ShapeDtypeStruct((n_groups, batch, width), F32)]
    if with_self:
        in_specs += [pl.BlockSpec((bs, width), lambda i: (i, 0)), pl.BlockSpec((bs, width), lambda i: (i, 0)),
                     pl.BlockSpec((n_groups, 1, width), lambda i: (0, 0, 0))]
        args += [new_k, new_v, sinks]
        out_specs += [blk3, blk3]
        out_shape += [jax.ShapeDtypeStruct(cache_k.shape, F32), jax.ShapeDtypeStruct(cache_v.shape, F32)]
    return pl.pallas_call(
        functools.partial(_decode_attn_kernel, scale=scale, with_self=with_self),
        grid=(batch // bs,),
        in_specs=in_specs, out_specs=out_specs, out_shape=out_shape,
        compiler_params=_cparams(("parallel",)),
        name="decode_attn",
    )(*args)


def _outproj_kernel(x_ref, h_ref, m_ref, w1_ref, w2_ref, g_ref, b_ref, o_ref):
    t = _dot(h_ref[...], w1_ref[...]) + _dot(m_ref[...], w2_ref[...])
    o_ref[...] = _layer_norm(DEEPNORM_ALPHA * x_ref[...] + t, g_ref[...], b_ref[...])


def _out_project(x, h, mem, w_out, g, b, tm):
    m = x.shape[0]
    hw = h.shape[1]
    w1, w2 = w_out[:hw], w_out[hw:]
    row = lambda i: (i, 0)
    fixed = lambda i: (0, 0)
    return pl.pallas_call(
        _outproj_kernel,
        grid=(m // tm,),
        in_specs=[pl.BlockSpec((tm, D_MODEL), row), pl.BlockSpec((tm, hw), row),
                  pl.BlockSpec((tm, MEM_WIDTH), row), pl.BlockSpec(w1.shape, fixed),
                  pl.BlockSpec(w2.shape, fixed), pl.BlockSpec((1, D_MODEL), fixed),
                  pl.BlockSpec((1, D_MODEL), fixed)],
        out_specs=pl.BlockSpec((tm, D_MODEL), row),
        out_shape=jax.ShapeDtypeStruct((m, D_MODEL), F32),
        compiler_params=_cparams(("parallel",)),
        name="out_proj",
    )(x, h, mem, w1, w2, g.reshape(1, -1), b.reshape(1, -1))


def _top2(logits):
    lane = lax.broadcasted_iota(jnp.int32, logits.shape, 1)
    valid = lane < N_EXPERTS
    lg = jnp.where(valid, logits, -jnp.inf)
    ex = jnp.exp(lg - jnp.max(lg, axis=-1, keepdims=True))
    probs = ex / jnp.sum(ex, axis=-1, keepdims=True)
    cand = jnp.where(valid, probs, -1.0)
    p1 = jnp.max(cand, axis=-1, keepdims=True)
    i1 = jnp.min(jnp.where(cand == p1, lane, LANES), axis=-1, keepdims=True)
    cand = jnp.where(lane == i1, -1.0, cand)
    p2 = jnp.max(cand, axis=-1, keepdims=True)
    i2 = jnp.min(jnp.where(cand == p2, lane, LANES), axis=-1, keepdims=True)
    tot = p1 + p2
    return p1 / tot, i1, p2 / tot, i2


def _router_gates(xb, rw):
    g1, i1, g2, i2 = _top2(_dot(xb, rw))
    lane = lax.broadcasted_iota(jnp.int32, (xb.shape[0], LANES), 1)
    return jnp.where(lane == i1, g1, 0.0) + jnp.where(lane == i2, g2, 0.0)


def _pad_router(router_w):
    return jnp.pad(router_w, ((0, 0), (0, LANES - N_EXPERTS))).astype(BF16)


def _ffn_kernel(*refs, moe):
    if moe:
        x_ref, wg_ref, wu_ref, wd_ref, rw_ref, g_ref, b_ref, o_ref, xb_ref, acc_ref, gate_ref = refs
    else:
        x_ref, wg_ref, wu_ref, wd_ref, g_ref, b_ref, o_ref, xb_ref, acc_ref = refs
    e = pl.program_id(1)
    f = pl.program_id(2)

    @pl.when((e == 0) & (f == 0))
    def _():
        xb_ref[...] = x_ref[...].astype(xb_ref.dtype)
        acc_ref[...] = jnp.zeros_like(acc_ref)
        if moe:
            gate_ref[...] = _router_gates(xb_ref[...], rw_ref[...])

    xb = xb_ref[...]
    gt = _dot(xb, wg_ref[0])
    up = _dot(xb, wu_ref[0])
    y = _dot(jax.nn.silu(gt) * up, wd_ref[0])
    if moe:
        lane = lax.broadcasted_iota(jnp.int32, gate_ref.shape, 1)
        y = y * jnp.sum(jnp.where(lane == e, gate_ref[...], 0.0), axis=-1, keepdims=True)
    acc_ref[...] += y

    @pl.when((e == pl.num_programs(1) - 1) & (f == pl.num_programs(2) - 1))
    def _():
        o_ref[...] = _layer_norm(DEEPNORM_ALPHA * x_ref[...] + acc_ref[...], g_ref[...], b_ref[...])


def _ffn(x, w_gu, w_down, g, b, tm, tf, router_w=None):
    m = x.shape[0]
    n_exp, _, two_f = w_gu.shape
    nf = two_f // 2 // tf
    moe = router_w is not None
    row = lambda i, e, f: (i, 0)
    fixed = lambda i, e, f: (0, 0)
    in_specs = [pl.BlockSpec((tm, D_MODEL), row),
                pl.BlockSpec((1, D_MODEL, tf), lambda i, e, f: (e, 0, f)),
                pl.BlockSpec((1, D_MODEL, tf), lambda i, e, f: (e, 0, nf + f)),
                pl.BlockSpec((1, tf, D_MODEL), lambda i, e, f: (e, f, 0))]
    args = [x, w_gu, w_gu, w_down]
    scratch = [pltpu.VMEM((tm, D_MODEL), BF16), pltpu.VMEM((tm, D_MODEL), F32)]
    if moe:
        in_specs.append(pl.BlockSpec((D_MODEL, LANES), fixed))
        args.append(_pad_router(router_w))
        scratch.append(pltpu.VMEM((tm, LANES), F32))
    in_specs += [pl.BlockSpec((1, D_MODEL), fixed), pl.BlockSpec((1, D_MODEL), fixed)]
    args += [g.reshape(1, -1), b.reshape(1, -1)]
    return pl.pallas_call(
        functools.partial(_ffn_kernel, moe=moe),
        grid=(m // tm, n_exp, nf),
        in_specs=in_specs,
        out_specs=pl.BlockSpec((tm, D_MODEL), row),
        out_shape=jax.ShapeDtypeStruct((m, D_MODEL), F32),
        scratch_shapes=scratch,
        compiler_params=_cparams(("parallel", "arbitrary", "arbitrary")),
        name="moe_ffn" if moe else "ffn",
    )(*args)


META_E1, META_E2, META_R1, META_R2 = 0, 1, 2, 3
ZERO_ROWS = 256


def _route_kernel(y_ref, rw_ref, gtop_ref, meta_ref, cnt_ref, carry_ref):
    i = pl.program_id(0)
    tm = y_ref.shape[0]

    @pl.when(i == 0)
    def _():
        carry_ref[...] = jnp.zeros_like(carry_ref)

    g1, i1, g2, i2 = _top2(_dot(y_ref[...], rw_ref[...]))
    lane = lax.broadcasted_iota(jnp.int32, (tm, LANES), 1)
    onehot = ((lane == i1) | (lane == i2)).astype(F32)
    r = lax.broadcasted_iota(jnp.int32, (tm, tm), 0)
    c = lax.broadcasted_iota(jnp.int32, (tm, tm), 1)
    before = _dot((r > c).astype(F32), onehot) + carry_ref[...]
    rank1 = jnp.sum(jnp.where(lane == i1, before, 0.0), axis=-1, keepdims=True).astype(jnp.int32)
    rank2 = jnp.sum(jnp.where(lane == i2, before, 0.0), axis=-1, keepdims=True).astype(jnp.int32)
    gtop_ref[...] = jnp.where(lane == 0, g1, jnp.where(lane == 1, g2, 0.0))
    meta_ref[...] = jnp.where(lane == META_E1, i1, jnp.where(lane == META_E2, i2, jnp.where(
        lane == META_R1, rank1, jnp.where(lane == META_R2, rank2, 0))))
    carry_ref[...] += jnp.sum(onehot, axis=0, keepdims=True)
    cnt_ref[...] = jnp.broadcast_to(carry_ref[...], cnt_ref.shape).astype(jnp.int32)


def _route(y, router_w, tm):
    m = y.shape[0]
    row = lambda i: (i, 0)
    return pl.pallas_call(
        _route_kernel,
        grid=(m // tm,),
        in_specs=[pl.BlockSpec((tm, D_MODEL), row), pl.BlockSpec((D_MODEL, LANES), lambda i: (0, 0))],
        out_specs=[pl.BlockSpec((tm, LANES), row), pl.BlockSpec((tm, LANES), row),
                   pl.BlockSpec((SUBLANES, LANES), lambda i: (0, 0))],
        out_shape=[jax.ShapeDtypeStruct((m, LANES), F32), jax.ShapeDtypeStruct((m, LANES), jnp.int32),
                   jax.ShapeDtypeStruct((SUBLANES, LANES), jnp.int32)],
        scratch_shapes=[pltpu.VMEM((1, LANES), F32)],
        compiler_params=_cparams(("arbitrary",)),
        name="moe_route",
    )(y, _pad_router(router_w))


def _dispatch_plan(meta, cnt, tmg, n_tiles, tm):
    counts = cnt[0, :N_EXPERTS]
    padded = (counts + tmg - 1) // tmg * tmg
    gend = jnp.cumsum(padded)
    gstart = gend - padded
    pos1 = gstart[meta[:, META_E1]] + meta[:, META_R1]
    pos2 = gstart[meta[:, META_E2]] + meta[:, META_R2]
    pos = jnp.concatenate([pos1.reshape(-1, 1, tm), pos2.reshape(-1, 1, tm)], axis=2).astype(jnp.int32)
    n_used = (gend[-1] // tmg).astype(jnp.int32).reshape(1)
    tile_start = jnp.arange(n_tiles, dtype=jnp.int32) * tmg
    tile_expert = jnp.minimum(jnp.sum(tile_start[:, None] >= gend[None, :], axis=1), N_EXPERTS - 1)
    tail = jnp.stack([gend[-1], (n_tiles * tmg - gend[-1]) // ZERO_ROWS])
    pads = jnp.concatenate([jnp.stack([gstart + counts, padded - counts]), tail[:, None]], axis=1).astype(jnp.int32)
    return pos, tile_expert.astype(jnp.int32), n_used, pads


def _row_copy(src_ref, src_row, dst_ref, dst_row, sem):
    return pltpu.make_async_copy(src_ref.at[pl.ds(src_row, 1)], dst_ref.at[pl.ds(dst_row, 1)], sem)


def _dispatch_kernel(pads_ref, pos_ref, x_ref, xs_ref, zero_ref, sem):
    i = pl.program_id(0)
    tm = x_ref.shape[0]

    def scatter(r, k):
        return _row_copy(x_ref, r, xs_ref, pos_ref[0, 0, k * tm + r], sem)

    def start(r, carry):
        scatter(r, 0).start()
        scatter(r, 1).start()
        return carry

    def wait(r, carry):
        scatter(r, 0).wait()
        scatter(r, 1).wait()
        return carry

    lax.fori_loop(0, tm, start, 0, unroll=8)
    lax.fori_loop(0, tm, wait, 0, unroll=8)

    @pl.when(i == pl.num_programs(0) - 1)
    def _():
        zero_ref[...] = jnp.zeros_like(zero_ref)
        for e in range(N_EXPERTS):
            first = pads_ref[0, e]
            n_pad = pads_ref[1, e]
            fill = lambda r: _row_copy(zero_ref, 0, xs_ref, first + r, sem)
            lax.fori_loop(0, n_pad, lambda r, c: (fill(r).start(), c)[1], 0)
            lax.fori_loop(0, n_pad, lambda r, c: (fill(r).wait(), c)[1], 0)
        tail_first = pads_ref[0, N_EXPERTS]
        n_blocks = pads_ref[1, N_EXPERTS]
        fill_tail = lambda r: pltpu.make_async_copy(
            zero_ref, xs_ref.at[pl.ds(pl.multiple_of(tail_first + r * ZERO_ROWS, ZERO_ROWS), ZERO_ROWS)], sem)
        lax.fori_loop(0, n_blocks, lambda r, c: (fill_tail(r).start(), c)[1], 0)
        lax.fori_loop(0, n_blocks, lambda r, c: (fill_tail(r).wait(), c)[1], 0)


def _dispatch(x, pos, pads, n_slots, tm):
    m = x.shape[0]
    return pl.pallas_call(
        _dispatch_kernel,
        grid_spec=pltpu.PrefetchScalarGridSpec(
            num_scalar_prefetch=1,
            grid=(m // tm,),
            in_specs=[pl.BlockSpec((1, 1, 2 * tm), lambda i, pads: (i, 0, 0), memory_space=pltpu.SMEM),
                      pl.BlockSpec((tm, D_MODEL), lambda i, pads: (i, 0))],
            out_specs=pl.BlockSpec(memory_space=pl.ANY),
            scratch_shapes=[pltpu.VMEM((ZERO_ROWS, D_MODEL), F32), pltpu.SemaphoreType.DMA(())]),
        out_shape=jax.ShapeDtypeStruct((n_slots, D_MODEL), F32),
        compiler_params=_cparams(("arbitrary",)),
        name="moe_dispatch",
    )(pads, pos, x)


def _grouped_ffn_kernel(te_ref, nu_ref, xs_ref, wg_ref, wu_ref, wd_ref, o_ref, xb_ref):
    j = pl.program_id(0)
    f = pl.program_id(1)
    used = j < nu_ref[0]

    @pl.when(used)
    def _():
        @pl.when(f == 0)
        def _():
            xb_ref[...] = xs_ref[...].astype(BF16)

        xb = xb_ref[...]
        y = _dot(jax.nn.silu(_dot(xb, wg_ref[0])) * _dot(xb, wu_ref[0]), wd_ref[0])

        @pl.when(f == 0)
        def _():
            o_ref[...] = y

        @pl.when(f > 0)
        def _():
            o_ref[...] += y

    @pl.when(jnp.logical_not(used) & (f == 0))
    def _():
        o_ref[...] = jnp.zeros_like(o_ref)


def _grouped_ffn(xs, w_gu, w_down, tile_expert, n_used, tmg, tf):
    n_slots = xs.shape[0]
    nf = w_down.shape[1] // tf
    tile = lambda j, f, te, nu: (jnp.minimum(j, nu[0] - 1), 0)
    chunk = lambda j, f, nu: jnp.where(j < nu[0], f, nf - 1)
    return pl.pallas_call(
        _grouped_ffn_kernel,
        grid_spec=pltpu.PrefetchScalarGridSpec(
            num_scalar_prefetch=2,
            grid=(n_slots // tmg, nf),
            in_specs=[pl.BlockSpec((tmg, D_MODEL), tile),
                      pl.BlockSpec((1, D_MODEL, tf), lambda j, f, te, nu: (te[j], 0, chunk(j, f, nu))),
                      pl.BlockSpec((1, D_MODEL, tf), lambda j, f, te, nu: (te[j], 0, nf + chunk(j, f, nu))),
                      pl.BlockSpec((1, tf, D_MODEL), lambda j, f, te, nu: (te[j], chunk(j, f, nu), 0))],
            out_specs=pl.BlockSpec((tmg, D_MODEL), lambda j, f, te, nu: (j, 0)),
            scratch_shapes=[pltpu.VMEM((tmg, D_MODEL), BF16)]),
        out_shape=jax.ShapeDtypeStruct((n_slots, D_MODEL), F32),
        compiler_params=_cparams(("arbitrary", "arbitrary")),
        name="moe_grouped",
    )(tile_expert, n_used, xs, w_gu, w_gu, w_down)


def _combine_kernel(pos_ref, posn_ref, x_ref, gt_ref, ys_ref, g_ref, b_ref, o_ref, ybuf, sem):
    i = pl.program_id(0)
    n = pl.num_programs(0)
    tm = x_ref.shape[0]
    slot = i % 2

    def gather(p_ref, s, r, k):
        return _row_copy(ys_ref, p_ref[0, 0, k * tm + r], ybuf.at[s, k], r, sem.at[s])

    def issue(p_ref, s):
        def body(r, carry):
            gather(p_ref, s, r, 0).start()
            gather(p_ref, s, r, 1).start()
            return carry
        lax.fori_loop(0, tm, body, 0, unroll=8)

    @pl.when(i == 0)
    def _():
        issue(pos_ref, 0)

    @pl.when(i + 1 < n)
    def _():
        issue(posn_ref, 1 - slot)

    def wait(r, carry):
        gather(pos_ref, slot, r, 0).wait()
        gather(pos_ref, slot, r, 1).wait()
        return carry

    lax.fori_loop(0, tm, wait, 0, unroll=8)
    gt = gt_ref[...]
    y = gt[:, 0:1] * ybuf[slot, 0] + gt[:, 1:2] * ybuf[slot, 1]
    o_ref[...] = _layer_norm(DEEPNORM_ALPHA * x_ref[...] + y, g_ref[...], b_ref[...])


def _combine(x, gtop, ys, pos, g, b, tm):
    m = x.shape[0]
    n = m // tm
    row = lambda i: (i, 0)
    fixed = lambda i: (0, 0)
    return pl.pallas_call(
        _combine_kernel,
        grid=(n,),
        in_specs=[pl.BlockSpec((1, 1, 2 * tm), lambda i: (i, 0, 0), memory_space=pltpu.SMEM),
                  pl.BlockSpec((1, 1, 2 * tm), lambda i: (jnp.minimum(i + 1, n - 1), 0, 0),
                               memory_space=pltpu.SMEM),
                  pl.BlockSpec((tm, D_MODEL), row), pl.BlockSpec((tm, LANES), row),
                  pl.BlockSpec(memory_space=pl.ANY),
                  pl.BlockSpec((1, D_MODEL), fixed), pl.BlockSpec((1, D_MODEL), fixed)],
        out_specs=pl.BlockSpec((tm, D_MODEL), row),
        out_shape=jax.ShapeDtypeStruct((m, D_MODEL), F32),
        scratch_shapes=[pltpu.VMEM((2, 2, tm, D_MODEL), F32), pltpu.SemaphoreType.DMA((2,))],
        compiler_params=_cparams(("arbitrary",)),
        name="moe_combine",
    )(pos, pos, x, gtop, ys, g.reshape(1, -1), b.reshape(1, -1))


def _moe(x, router_w, w_gu, w_down, g, b, tmg, tf, tm):
    m = x.shape[0]
    n_tiles = -(-(2 * m + N_EXPERTS * (tmg - 1)) // tmg)
    gtop, meta, cnt = _route(x, router_w, tm)
    pos, tile_expert, n_used, pads = _dispatch_plan(meta, cnt, tmg, n_tiles, tm)
    xs = _dispatch(x, pos, pads, n_tiles * tmg, tm)
    ys = _grouped_ffn(xs, w_gu, w_down, tile_expert, n_used, tmg, tf)
    return _combine(x, gtop, ys, pos, g, b, tm)


def _gdn_post(conv_out, proj, alog_ref, dtb_ref, q_ref, k_ref, v_ref, z_ref, gb_ref, qm_ref):
    c = jax.nn.silu(conv_out)
    for h in range(GDN_HEADS):
        sl = slice(GDN_DK * h, GDN_DK * (h + 1))
        qh = c[:, sl]
        kh = c[:, GDN_QK + GDN_DK * h:GDN_QK + GDN_DK * (h + 1)]
        q_ref[:, sl] = qh * lax.rsqrt(jnp.sum(qh * qh, axis=-1, keepdims=True) + L2_EPS) * GDN_DK ** -0.5
        k_ref[:, sl] = kh * lax.rsqrt(jnp.sum(kh * kh, axis=-1, keepdims=True) + L2_EPS)
    v_ref[...] = c[:, 2 * GDN_QK:]
    z_ref[...] = proj[:, GDN_CONV_DIM:GDN_CONV_DIM + GDN_VW]
    ab = proj[:, GDN_CONV_DIM + GDN_VW:GDN_CONV_DIM + GDN_VW + GDN_AB_PAD]
    lane = lax.broadcasted_iota(jnp.int32, ab.shape, 1)
    decay = -jnp.exp(alog_ref[...]) * jax.nn.softplus(ab + dtb_ref[...])
    gb_ref[...] = jnp.where(lane < GDN_HEADS, decay, jax.nn.sigmoid(ab))
    qm_ref[...] = proj[:, GDN_CONV_DIM + GDN_VW + GDN_AB_PAD:].astype(qm_ref.dtype)


def _gdn_proj_kernel(x_ref, w_ref, cw_ref, alog_ref, dtb_ref,
                     q_ref, k_ref, v_ref, z_ref, gb_ref, qm_ref, cs_ref, buf_ref):
    i = pl.program_id(1)
    tm = x_ref.shape[0]
    pad = SUBLANES
    proj = _dot(x_ref[...], w_ref[...])
    qkv = proj[:, :GDN_CONV_DIM]

    @pl.when(i == 0)
    def _():
        buf_ref[0:pad, :] = jnp.zeros((pad, GDN_CONV_DIM), F32)

    buf_ref[pad:pad + tm, :] = qkv
    cw = cw_ref[...]
    conv = buf_ref[pad - 3:pad - 3 + tm, :] * cw[0:1]
    conv = conv + buf_ref[pad - 2:pad - 2 + tm, :] * cw[1:2]
    conv = conv + buf_ref[pad - 1:pad - 1 + tm, :] * cw[2:3]
    conv = conv + qkv * cw[3:4]
    tail = buf_ref[tm:tm + pad, :]
    buf_ref[0:pad, :] = tail
    cs_ref[0] = tail
    _gdn_post(conv, proj, alog_ref, dtb_ref, q_ref, k_ref, v_ref, z_ref, gb_ref, qm_ref)


def _gdn_gate_params(a_log, dt_bias):
    padv = lambda v: jnp.pad(v.astype(F32), (0, GDN_AB_PAD - GDN_HEADS)).reshape(1, GDN_AB_PAD)
    return padv(a_log), padv(dt_bias)


def _gdn_pad_w_in(w_in):
    o3 = GDN_CONV_DIM + GDN_VW
    o4 = o3 + 2 * GDN_HEADS
    ab = jnp.pad(w_in[:, o3:o4], ((0, 0), (0, GDN_AB_PAD - 2 * GDN_HEADS)))
    return jnp.concatenate([w_in[:, :o3], ab, w_in[:, o4:]], axis=1)


def _gdn_project(x, w_pad, conv_w, a_log, dt_bias, batch, seq, tm):
    m = x.shape[0]
    nt = seq // tm
    row = lambda b, i: (b * nt + i, 0)
    fixed = lambda b, i: (0, 0)
    alog, dtb = _gdn_gate_params(a_log, dt_bias)
    wide = jax.ShapeDtypeStruct((m, GDN_QK), F32)
    return pl.pallas_call(
        _gdn_proj_kernel,
        grid=(batch, nt),
        in_specs=[pl.BlockSpec((tm, D_MODEL), row), pl.BlockSpec(w_pad.shape, fixed),
                  pl.BlockSpec(conv_w.shape, fixed), pl.BlockSpec(alog.shape, fixed),
                  pl.BlockSpec(dtb.shape, fixed)],
        out_specs=[pl.BlockSpec((tm, GDN_QK), row), pl.BlockSpec((tm, GDN_QK), row),
                   pl.BlockSpec((tm, GDN_VW), row), pl.BlockSpec((tm, GDN_VW), row),
                   pl.BlockSpec((tm, GDN_AB_PAD), row), pl.BlockSpec((tm, MEM_WIDTH), row),
                   pl.BlockSpec((1, SUBLANES, GDN_CONV_DIM), lambda b, i: (b, 0, 0))],
        out_shape=[wide, wide, wide, wide, jax.ShapeDtypeStruct((m, GDN_AB_PAD), F32),
                   jax.ShapeDtypeStruct((m, MEM_WIDTH), BF16),
                   jax.ShapeDtypeStruct((batch, SUBLANES, GDN_CONV_DIM), F32)],
        scratch_shapes=[pltpu.VMEM((tm + SUBLANES, GDN_CONV_DIM), F32)],
        compiler_params=_cparams(("parallel", "arbitrary")),
        name="gdn_proj",
    )(x, w_pad, conv_w, alog, dtb)


def _gdn_step_proj_kernel(x_ref, w_ref, cs_ref, cw_ref, alog_ref, dtb_ref,
                          pre_ref, q_ref, k_ref, v_ref, z_ref, gb_ref, qm_ref):
    proj = _dot(x_ref[...], w_ref[...])
    qkv = proj[:, :GDN_CONV_DIM]
    pre_ref[...] = qkv
    cw = cw_ref[...]
    conv = cs_ref[0] * cw[0:1]
    conv = conv + cs_ref[1] * cw[1:2]
    conv = conv + cs_ref[2] * cw[2:3]
    conv = conv + qkv * cw[3:4]
    _gdn_post(conv, proj, alog_ref, dtb_ref, q_ref, k_ref, v_ref, z_ref, gb_ref, qm_ref)


def _gdn_step_project(x, w_pad, conv_state, conv_w, a_log, dt_bias):
    m = x.shape[0]
    alog, dtb = _gdn_gate_params(a_log, dt_bias)
    cs = jnp.transpose(conv_state, (1, 0, 2))
    wide = jax.ShapeDtypeStruct((m, GDN_QK), F32)
    return pl.pallas_call(
        _gdn_step_proj_kernel,
        out_shape=[jax.ShapeDtypeStruct((m, GDN_CONV_DIM), F32), wide, wide, wide, wide,
                   jax.ShapeDtypeStruct((m, GDN_AB_PAD), F32), jax.ShapeDtypeStruct((m, MEM_WIDTH), F32)],
        compiler_params=pltpu.CompilerParams(vmem_limit_bytes=VMEM_LIMIT_MB * 1024 * 1024),
        name="gdn_step_proj",
    )(x, w_pad, cs, conv_w, alog, dtb)


def _gated_out(o, z, nw):
    on = o * lax.rsqrt(jnp.mean(o * o, axis=-1, keepdims=True) + RMS_EPS) * nw
    return on * jax.nn.silu(z)


GDN_SUPER = 2 * GDN_CHUNK
GDN_INTRA_ROWS = 256


def _gdn_intra_kernel(q_ref, k_ref, v_ref, gb_ref, u_ref, w_ref, qg_ref, kg_ref, qk_ref, egl_ref):
    n = GDN_SUPER
    c = GDN_CHUNK
    shift = c.bit_length() - 1
    r = lax.broadcasted_iota(jnp.int32, (n, n), 0)
    col = lax.broadcasted_iota(jnp.int32, (n, n), 1)
    same = (r >> shift) == (col >> shift)
    causal = same & (r >= col)
    strict = same & (r > col)
    eye = (r == col).astype(F32)
    tri = causal.astype(F32)
    ones = same.astype(F32)
    groups = [slice(n * j, n * (j + 1)) for j in range(q_ref.shape[0] // n)]
    gb = [gb_ref[rows, :] for rows in groups]
    gc_cols = [_dot_f32(tri, x) for x in gb]
    gc_rows = [lax.dot_general(x, tri, (((0,), (1,)), ((), ())), preferred_element_type=F32,
                               precision=lax.Precision.HIGHEST) for x in gb]
    gl_cols = [_dot_f32(ones, x) for x in gb]
    for j, rows in enumerate(groups):
        egl_ref[rows, :] = jnp.exp(gl_cols[j])
    parts = [(j, h) for j in range(len(groups)) for h in range(GDN_HEADS)]
    idx = [(groups[j], slice(GDN_DK * h, GDN_DK * (h + 1))) for j, h in parts]
    ps = range(len(parts))
    q = [q_ref[i] for i in idx]
    k = [k_ref[i] for i in idx]
    gcol = [gc_cols[j][:, h:h + 1] for j, h in parts]
    beta = [gb[j][:, GDN_HEADS + h:GDN_HEADS + h + 1] for j, h in parts]
    decay = [jnp.exp(jnp.where(causal, gcol[p] - gc_rows[j][h:h + 1, :], -jnp.inf)) for p, (j, h) in enumerate(parts)]
    kb = [k[p] * beta[p] for p in ps]
    a = [jnp.where(strict, _dot_nt(kb[p], k[p]) * decay[p], 0.0) for p in ps]
    qk = [jnp.where(causal, _dot_nt(q[p], k[p]) * decay[p], 0.0) for p in ps]
    qk = [jnp.concatenate([m[c * i:c * (i + 1), c * i:c * (i + 1)] for i in range(n // c)], axis=0) for m in qk]
    for j, rows in enumerate(groups):
        qk_ref[rows, :] = jnp.concatenate(qk[GDN_HEADS * j:GDN_HEADS * (j + 1)], axis=1).astype(qk_ref.dtype)
    tinv = [eye - a[p] for p in ps]
    apow = a
    for _ in range(shift - 1):
        apow = [_dot(apow[p], apow[p]) for p in ps]
        tinv = [_dot(tinv[p], eye + apow[p]) for p in ps]
    eg = [jnp.exp(gcol[p]) for p in ps]
    for p in ps:
        u_ref[idx[p]] = _dot(tinv[p], v_ref[idx[p]] * beta[p])
    for p in ps:
        w_ref[idx[p]] = _dot(tinv[p], kb[p] * eg[p]).astype(w_ref.dtype)
    for p, (j, h) in enumerate(parts):
        qg_ref[idx[p]] = (q[p] * eg[p]).astype(qg_ref.dtype)
        kg_ref[idx[p]] = (k[p] * jnp.exp(gl_cols[j][:, h:h + 1] - gcol[p])).astype(kg_ref.dtype)


def _gdn_scan_kernel(u_ref, w_ref, qg_ref, kg_ref, qk_ref, egl_ref, z_ref, nw_ref, o_ref, sfin_ref, s_ref):
    n = pl.program_id(0)
    c = GDN_CHUNK

    @pl.when(n == 0)
    def _():
        s_ref[...] = jnp.zeros_like(s_ref)

    nw = nw_ref[...]

    def per_sequence(b, carry):
        egl = egl_ref[b, 0:1, :]
        heads = range(GDN_HEADS)
        sls = [slice(GDN_DK * h, GDN_DK * (h + 1)) for h in heads]
        state = [s_ref[b, h] for h in heads]
        ws_qs = [_dot(jnp.concatenate([w_ref[b, :, sls[h]], qg_ref[b, :, sls[h]]], axis=0), state[h])
                 for h in heads]
        v_new = [(u_ref[b, :, sls[h]] - ws_qs[h][:c]).astype(BF16) for h in heads]
        pairs = [qk_ref[b, :, LANES * j:LANES * (j + 1)] for j in range(GDN_HEADS // 2)]
        o = [ws_qs[h][c:] + _dot(pairs[h // 2][:, c * (h % 2):c * (h % 2 + 1)], v_new[h]) for h in heads]
        upd = [_dot_tn(kg_ref[b, :, sls[h]], v_new[h]) for h in heads]
        for h in heads:
            s_ref[b, h] = state[h] * egl[:, h:h + 1] + upd[h]
        for h in heads:
            o_ref[b, :, sls[h]] = _gated_out(o[h], z_ref[b, :, sls[h]], nw).astype(o_ref.dtype)
        return carry

    lax.fori_loop(0, u_ref.shape[0], per_sequence, 0)

    @pl.when(n == pl.num_programs(0) - 1)
    def _():
        sfin_ref[...] = s_ref[...]


def _gdn_chunked(q, k, v, z, gb, norm_w, batch, seq):
    m = batch * seq
    row = lambda i: (i, 0)
    wide = pl.BlockSpec((GDN_INTRA_ROWS, GDN_QK), row)
    qk_w = GDN_HEADS * GDN_CHUNK
    u, w, qg, kg, qk, egl = pl.pallas_call(
        _gdn_intra_kernel,
        grid=(m // GDN_INTRA_ROWS,),
        in_specs=[wide, wide, wide, pl.BlockSpec((GDN_INTRA_ROWS, GDN_AB_PAD), row)],
        out_specs=[wide, wide, wide, wide, pl.BlockSpec((GDN_INTRA_ROWS, qk_w), row),
                   pl.BlockSpec((GDN_INTRA_ROWS, LANES), row)],
        out_shape=[jax.ShapeDtypeStruct((m, GDN_VW), F32), jax.ShapeDtypeStruct((m, GDN_QK), BF16),
                   jax.ShapeDtypeStruct((m, GDN_QK), BF16), jax.ShapeDtypeStruct((m, GDN_QK), BF16),
                   jax.ShapeDtypeStruct((m, qk_w), BF16), jax.ShapeDtypeStruct((m, LANES), F32)],
        compiler_params=_cparams(("parallel",)),
        name="gdn_intra",
    )(q, k, v, gb)
    per_seq = lambda a: a.reshape(batch, seq, a.shape[-1])
    chunk = lambda width: pl.BlockSpec((batch, GDN_CHUNK, width), lambda n: (0, n, 0))
    state_spec = pl.BlockSpec((batch, GDN_HEADS, GDN_DK, GDN_DV), lambda n: (0, 0, 0, 0))
    h, s_fin = pl.pallas_call(
        _gdn_scan_kernel,
        grid=(seq // GDN_CHUNK,),
        in_specs=[chunk(GDN_VW), chunk(GDN_QK), chunk(GDN_QK), chunk(GDN_QK), chunk(qk_w), chunk(LANES),
                  chunk(GDN_VW), pl.BlockSpec((1, GDN_DV), lambda n: (0, 0))],
        out_specs=[chunk(GDN_VW), state_spec],
        out_shape=[jax.ShapeDtypeStruct((batch, seq, GDN_VW), BF16),
                   jax.ShapeDtypeStruct((batch, GDN_HEADS, GDN_DK, GDN_DV), F32)],
        scratch_shapes=[pltpu.VMEM((batch, GDN_HEADS, GDN_DK, GDN_DV), F32)],
        compiler_params=_cparams(("arbitrary",)),
        name="gdn_scan",
    )(per_seq(u), per_seq(w), per_seq(qg), per_seq(kg), per_seq(qk), per_seq(egl), per_seq(z),
      norm_w.reshape(1, -1))
    return h.reshape(m, GDN_VW), s_fin


def _gdn_recurrent_kernel(q_ref, k_ref, v_ref, z_ref, gb_ref, nw_ref, s_ref, o_ref, so_ref):
    bs = q_ref.shape[0]
    gb = gb_ref[...]
    nw = nw_ref[...]
    for h in range(GDN_HEADS):
        sl = slice(GDN_DK * h, GDN_DK * (h + 1))
        q_t = q_ref[:, sl].T
        k_t = k_ref[:, sl].T
        v = v_ref[:, sl]
        rows = []
        for b in range(bs):
            state = s_ref[b, h] * jnp.exp(gb[b:b + 1, h:h + 1])
            kcol = k_t[:, b:b + 1]
            v_new = (v[b:b + 1, :] - jnp.sum(kcol * state, axis=0, keepdims=True)) \
                * gb[b:b + 1, GDN_HEADS + h:GDN_HEADS + h + 1]
            state = state + kcol * v_new
            so_ref[b, h] = state
            rows.append(jnp.sum(q_t[:, b:b + 1] * state, axis=0, keepdims=True))
        o_ref[:, sl] = _gated_out(jnp.concatenate(rows, axis=0), z_ref[:, sl], nw).astype(o_ref.dtype)


def _gdn_recurrent(q, k, v, z, gb, norm_w, state, bs):
    batch = q.shape[0]
    row = lambda i: (i, 0)
    wide = pl.BlockSpec((bs, GDN_QK), row)
    st = pl.BlockSpec((bs, GDN_HEADS, GDN_DK, GDN_DV), lambda i: (i, 0, 0, 0))
    return pl.pallas_call(
        _gdn_recurrent_kernel,
        grid=(batch // bs,),
        in_specs=[wide, wide, wide, wide, pl.BlockSpec((bs, GDN_AB_PAD), row),
                  pl.BlockSpec((1, GDN_DV), lambda i: (0, 0)), st],
        out_specs=[wide, st],
        out_shape=[jax.ShapeDtypeStruct((batch, GDN_VW), F32), jax.ShapeDtypeStruct(state.shape, F32)],
        compiler_params=_cparams(("parallel",)),
        name="gdn_recurrent",
    )(q, k, v, z, gb, norm_w.reshape(1, -1), state)


PROMPT_TM = 512
GDN_TM = 256
FFN_TM = 1024
FFN_TF = 512
MOE_TMG = 1024
MOE_TM = 512
DECODE_BS = 8


def kernel(x_prompt, x_sample, cache_swa_k, cache_swa_v, state_gdn_conv, state_gdn_rec, cache_mem_k, cache_mem_v, mem_prompt, w_in_swa, swa_sinks, w_in_gdn, gdn_conv_w, gdn_a_log, gdn_dt_bias, gdn_norm_w, w_mem_kv, w_out, ln1_g, ln1_b, ln2_g, ln2_b, ffn_w_gu, ffn_w_down, router_w, moe_w_gu, moe_w_down):
    batch, seq, _ = x_prompt.shape
    dec = x_sample.shape[0]
    assert x_sample.shape[1] == 1
    yp = x_prompt.reshape(batch * seq, D_MODEL)
    ys = x_sample.reshape(dec, D_MODEL)
    mem_flat = mem_prompt.reshape(batch * N_MEM, D_MODEL)
    mem3 = lambda a: a.reshape(-1, N_MEM, MEM_WIDTH)

    w_in0 = w_in_swa[0].astype(BF16)
    w_out0 = w_out[0].astype(BF16)
    w_gu0 = ffn_w_gu[0:1].astype(BF16)
    w_dn0 = ffn_w_down[0:1].astype(BF16)
    mkv = _matmul(mem_flat, w_mem_kv[0].astype(BF16), PROMPT_TM)
    mk0, mv0 = mkv[:, :MEM_WIDTH], mkv[:, MEM_WIDTH:]
    cos_p, sin_p = _rope_tables(jnp.arange(seq, dtype=jnp.int32))
    q, k, v, qm = _swa_project(yp, w_in0, cos_p, sin_p, PROMPT_TM)
    h = _swa_attention(q, k, v, swa_sinks[0], batch, seq)
    ma = _mem_attention(qm, mem3(mk0), mem3(mv0), batch, seq, PROMPT_TM)
    swa_kp = k.reshape(batch, seq, SWA_KV_HEADS, HEAD_DIM)[:, -WINDOW:]
    swa_vp = v.reshape(batch, seq, SWA_KV_HEADS, HEAD_DIM)[:, -WINDOW:]
    yp = _out_project(yp, h, ma, w_out0, ln1_g[0], ln1_b[0], PROMPT_TM)
    yp = _ffn(yp, w_gu0, w_dn0, ln2_g[0], ln2_b[0], FFN_TM, FFN_TF)

    cos_s, sin_s = _rope_tables(jnp.full((dec,), PAST_LEN, jnp.int32))
    q, k, v, qm = _swa_project(ys, w_in0, cos_s, sin_s, dec)
    win = cache_swa_k.shape[2]
    qg = q.reshape(dec, SWA_KV_HEADS, SWA_GROUP, HEAD_DIM).transpose(2, 0, 1, 3).reshape(SWA_GROUP, dec, SWA_KVW)
    sink_g = jnp.repeat(swa_sinks[0].reshape(SWA_KV_HEADS, SWA_GROUP).T, HEAD_DIM, axis=1).reshape(
        SWA_GROUP, 1, SWA_KVW)
    og, swa_ks, swa_vs = _decode_attention(
        qg, cache_swa_k[0].reshape(dec, win, SWA_KVW), cache_swa_v[0].reshape(dec, win, SWA_KVW),
        1.0, HEAD_DIM, DECODE_BS, new_k=k, new_v=v, sinks=sink_g)
    h = og.reshape(SWA_GROUP, dec, SWA_KV_HEADS, HEAD_DIM).transpose(1, 2, 0, 3).reshape(dec, SWA_QW)
    ma, = _decode_attention(qm.reshape(1, dec, MEM_WIDTH), mem3(cache_mem_k[0]), mem3(cache_mem_v[0]),
                            MEM_HEAD_DIM ** -0.5, MEM_HEAD_DIM, DECODE_BS)
    ys = _out_project(ys, h, ma[0], w_out0, ln1_g[0], ln1_b[0], dec)
    ys = _ffn(ys, w_gu0, w_dn0, ln2_g[0], ln2_b[0], dec, FFN_TF)

    w_in1 = _gdn_pad_w_in(w_in_gdn[0]).astype(BF16)
    w_out1 = w_out[1].astype(BF16)
    w_gu1 = moe_w_gu[0].astype(BF16)
    w_dn1 = moe_w_down[0].astype(BF16)
    mkv = _matmul(mem_flat, w_mem_kv[1].astype(BF16), PROMPT_TM)
    mk1, mv1 = mkv[:, :MEM_WIDTH], mkv[:, MEM_WIDTH:]
    q, k, v, z, gb, qm, conv_tail = _gdn_project(yp, w_in1, gdn_conv_w[0], gdn_a_log[0], gdn_dt_bias[0],
                                                 batch, seq, GDN_TM)
    h, rec_p = _gdn_chunked(q, k, v, z, gb, gdn_norm_w[0], batch, seq)
    ma = _mem_attention(qm, mem3(mk1), mem3(mv1), batch, seq, PROMPT_TM)
    yp = _out_project(yp, h, ma, w_out1, ln1_g[1], ln1_b[1], PROMPT_TM)
    yp = _moe(yp, router_w[0], w_gu1, w_dn1, ln2_g[1], ln2_b[1], MOE_TMG, FFN_TF, MOE_TM)

    pre, q, k, v, z, gb, qm = _gdn_step_project(ys, w_in1, state_gdn_conv[0], gdn_conv_w[0],
                                                gdn_a_log[0], gdn_dt_bias[0])
    h, rec_s = _gdn_recurrent(q, k, v, z, gb, gdn_norm_w[0], state_gdn_rec[0], DECODE_BS)
    ma, = _decode_attention(qm.reshape(1, dec, MEM_WIDTH), mem3(cache_mem_k[1]), mem3(cache_mem_v[1]),
                            MEM_HEAD_DIM ** -0.5, MEM_HEAD_DIM, DECODE_BS)
    ys = _out_project(ys, h, ma[0], w_out1, ln1_g[1], ln1_b[1], dec)
    ys = _ffn(ys, w_gu1, w_dn1, ln2_g[1], ln2_b[1], dec, FFN_TF, router_w=router_w[0])

    mem_shape = (batch, N_MEM, MEM_HEADS, MEM_HEAD_DIM)
    conv_s = jnp.concatenate([state_gdn_conv[0][:, 1:], pre[:, None, :]], axis=1)
    return (yp.reshape(batch, seq, D_MODEL), ys.reshape(dec, 1, D_MODEL),
            swa_kp[None], swa_vp[None],
            conv_tail[None, :, SUBLANES - (GDN_CONV_W - 1):, :], rec_p[None],
            jnp.stack([mk0.reshape(mem_shape), mk1.reshape(mem_shape)]),
            jnp.stack([mv0.reshape(mem_shape), mv1.reshape(mem_shape)]),
            swa_ks.reshape(1, dec, win, SWA_KV_HEADS, HEAD_DIM), swa_vs.reshape(1, dec, win, SWA_KV_HEADS, HEAD_DIM),
            conv_s[None], rec_s[None])
```

```python
import functools

import jax
import jax.numpy as jnp
import numpy as np
from jax import lax
from jax.experimental import pallas as pl
from jax.experimental.pallas import tpu as pltpu

F32 = jnp.float32
BF16 = jnp.bfloat16

D_MODEL = 1024
DEPTH = 2
PAST_LEN = 16384
SWA_HEADS = 12
SWA_KV_HEADS = 4
SWA_GROUP = SWA_HEADS // SWA_KV_HEADS
HEAD_DIM = 64
WINDOW = 128
SWA_BLOCK = 128
ROPE_THETA = 10000.0
SWA_QW = SWA_HEADS * HEAD_DIM
SWA_KVW = SWA_KV_HEADS * HEAD_DIM
GDN_HEADS = 6
GDN_DK = 128
GDN_DV = 128
GDN_CONV_W = 4
GDN_CHUNK = 64
GDN_QK = GDN_HEADS * GDN_DK
GDN_VW = GDN_HEADS * GDN_DV
GDN_CONV_DIM = 2 * GDN_QK + GDN_VW
N_MEM = 256
MEM_HEADS = 4
MEM_HEAD_DIM = 64
MEM_WIDTH = MEM_HEADS * MEM_HEAD_DIM
D_FF = 3584
N_EXPERTS = 8
DEEPNORM_ALPHA = (2 * DEPTH) ** 0.25
LN_EPS = 1e-5
RMS_EPS = 1e-6
L2_EPS = 1e-6

LANES = 128
SUBLANES = 8
GDN_AB_PAD = LANES
VMEM_LIMIT_MB = 56


def _cparams(sem, vmem_mb=VMEM_LIMIT_MB):
    return pltpu.CompilerParams(dimension_semantics=sem, vmem_limit_bytes=vmem_mb * 1024 * 1024)


def _bf16_round(x):
    return x.astype(BF16).astype(F32)


def _dot(a, b):
    return jnp.dot(a.astype(BF16), b.astype(BF16), preferred_element_type=F32)


def _dot_nt(a, b):
    return lax.dot_general(a.astype(BF16), b.astype(BF16), (((1,), (1,)), ((), ())),
                           preferred_element_type=F32)


def _dot_tn(a, b):
    return lax.dot_general(a.astype(BF16), b.astype(BF16), (((0,), (0,)), ((), ())),
                           preferred_element_type=F32)


def _dot_f32(a, b):
    return jnp.dot(a, b, preferred_element_type=F32, precision=lax.Precision.HIGHEST)


def _layer_norm(t, g, b):
    mu = jnp.mean(t, axis=-1, keepdims=True)
    d = t - mu
    var = jnp.mean(d * d, axis=-1, keepdims=True)
    return d * lax.rsqrt(var + LN_EPS) * g + b


def _mm_kernel(x_ref, w_ref, o_ref):
    o_ref[...] = _dot(x_ref[...], w_ref[...])


def _matmul(x, w, tm):
    m, k = x.shape
    n = w.shape[1]
    return pl.pallas_call(
        _mm_kernel,
        grid=(m // tm,),
        in_specs=[pl.BlockSpec((tm, k), lambda i: (i, 0)),
                  pl.BlockSpec((k, n), lambda i: (0, 0))],
        out_specs=pl.BlockSpec((tm, n), lambda i: (i, 0)),
        out_shape=jax.ShapeDtypeStruct((m, n), F32),
        compiler_params=_cparams(("parallel",)),
        name="matmul",
    )(x, w)


def _swa_proj_kernel(x_ref, w_ref, cos_ref, sin_ref, q_ref, k_ref, v_ref, qm_ref, kt_ref, vt_ref):
    proj = _dot(x_ref[...], w_ref[...])
    cos = cos_ref[...]
    sin = sin_ref[...]
    lane = lax.broadcasted_iota(jnp.int32, cos.shape, 1)
    first_half = (lane & (HEAD_DIM - 1)) < HEAD_DIM // 2

    def rope(xb):
        partner = jnp.where(first_half, pltpu.roll(xb, LANES - HEAD_DIM // 2, 1),
                            pltpu.roll(xb, HEAD_DIM // 2, 1))
        return xb * cos + partner * sin

    for j in range(SWA_QW // LANES):
        sl = slice(LANES * j, LANES * (j + 1))
        q_ref[:, sl] = (rope(proj[:, sl]) * HEAD_DIM ** -0.5).astype(q_ref.dtype)
    for j in range(SWA_KVW // LANES):
        k_ref[:, LANES * j:LANES * (j + 1)] = rope(proj[:, SWA_QW + LANES * j:SWA_QW + LANES * (j + 1)])
    v_ref[...] = proj[:, SWA_QW + SWA_KVW:SWA_QW + 2 * SWA_KVW]
    qm_ref[...] = proj[:, SWA_QW + 2 * SWA_KVW:].astype(qm_ref.dtype)
    tm = x_ref.shape[0]
    kt_ref[...] = k_ref[tm - WINDOW:, :]
    vt_ref[...] = v_ref[tm - WINDOW:, :]


def _rope_tables(pos):
    half = HEAD_DIM // 2
    inv = ROPE_THETA ** (-jnp.arange(half, dtype=F32) / half)
    ang = pos.astype(F32)[:, None] * inv[None, :]
    cos = jnp.cos(ang)
    sin = jnp.sin(ang)
    reps = LANES // HEAD_DIM
    return jnp.tile(cos, (1, 2 * reps)), jnp.tile(jnp.concatenate([-sin, sin], axis=1), (1, reps))


def _swa_project(x, w, cos, sin, tm):
    m = x.shape[0]
    n_in = w.shape[1]
    tab_blocks = cos.shape[0] // tm
    n_seq = m // (tm * tab_blocks)
    assert tm >= WINDOW
    row = lambda i: (i, 0)
    tail = lambda i: (i // tab_blocks, 0)
    return pl.pallas_call(
        _swa_proj_kernel,
        grid=(m // tm,),
        in_specs=[pl.BlockSpec((tm, D_MODEL), row),
                  pl.BlockSpec((D_MODEL, n_in), lambda i: (0, 0)),
                  pl.BlockSpec((tm, LANES), lambda i: (i % tab_blocks, 0)),
                  pl.BlockSpec((tm, LANES), lambda i: (i % tab_blocks, 0))],
        out_specs=[pl.BlockSpec((tm, SWA_QW), row), pl.BlockSpec((tm, SWA_KVW), row),
                   pl.BlockSpec((tm, SWA_KVW), row), pl.BlockSpec((tm, MEM_WIDTH), row),
                   pl.BlockSpec((WINDOW, SWA_KVW), tail), pl.BlockSpec((WINDOW, SWA_KVW), tail)],
        out_shape=[jax.ShapeDtypeStruct((m, SWA_QW), BF16),
                   jax.ShapeDtypeStruct((m, SWA_KVW), F32),
                   jax.ShapeDtypeStruct((m, SWA_KVW), F32),
                   jax.ShapeDtypeStruct((m, MEM_WIDTH), BF16),
                   jax.ShapeDtypeStruct((n_seq * WINDOW, SWA_KVW), F32),
                   jax.ShapeDtypeStruct((n_seq * WINDOW, SWA_KVW), F32)],
        compiler_params=_cparams(("arbitrary",)),
        name="swa_proj",
    )(x, w, cos, sin)


def _swa_pair_heads():
    return [(SWA_GROUP * (2 * p) + g, SWA_GROUP * (2 * p + 1) + g)
            for p in range(SWA_KV_HEADS // 2) for g in range(SWA_GROUP)]


def _swa_pair_perm():
    cols = []
    for a, b in _swa_pair_heads():
        cols += list(range(HEAD_DIM * a, HEAD_DIM * (a + 1))) + list(range(HEAD_DIM * b, HEAD_DIM * (b + 1)))
    return np.asarray(cols, np.int32)


SWA_QBLOCKS = 4


def _swa_attn_kernel(sink_ref, q_ref, kp_ref, kc_ref, vp_ref, vc_ref, o_ref):
    i = pl.program_id(1)
    blk = SWA_BLOCK
    nq = q_ref.shape[0] // blk
    kwin = jnp.concatenate([kp_ref[...], kc_ref[...]], axis=0).astype(BF16)
    vwin = jnp.concatenate([vp_ref[...], vc_ref[...]], axis=0).astype(BF16)
    qi = lax.broadcasted_iota(jnp.int32, (blk, 2 * blk), 0)
    kj = lax.broadcasted_iota(jnp.int32, (blk, 2 * blk), 1)
    band = (kj >= qi) & (kj <= qi + WINDOW)
    lo = lax.broadcasted_iota(jnp.int32, (blk, LANES), 1) < HEAD_DIM
    pair_heads = _swa_pair_heads()
    zero = jnp.zeros((blk, LANES), q_ref.dtype)
    ones = jnp.ones((2 * blk, LANES), BF16)
    parts = [(j, p) for j in range(nq) for p in range(SWA_KV_HEADS // 2)]
    ps = range(len(parts))
    qs, sink, mask, kslab, vslab = [], [], [], [], []
    for j, p in parts:
        ms = range(SWA_GROUP * p, SWA_GROUP * (p + 1))
        blocks = [q_ref[blk * j:blk * (j + 1), LANES * m:LANES * (m + 1)] for m in ms]
        qs.append(jnp.concatenate([jnp.where(lo, x, zero) for x in blocks]
                                  + [jnp.where(lo, zero, x) for x in blocks], axis=0))
        heads = [pair_heads[m][0] for m in ms] + [pair_heads[m][1] for m in ms]
        sink.append(jnp.concatenate([jnp.full((blk, 1), sink_ref[h], F32) for h in heads], axis=0))
        valid = band & ((kj >= blk) | (i > 0)) if j == 0 else band
        mask.append(jnp.concatenate([valid] * (2 * SWA_GROUP), axis=0))
        kslab.append(kwin[blk * j:blk * (j + 2), LANES * p:LANES * (p + 1)])
        vslab.append(jnp.concatenate([vwin[blk * j:blk * (j + 2), LANES * p:LANES * (p + 1)], ones], axis=1))
    s = [jnp.where(mask[t], _dot_nt(qs[t], kslab[t]), -jnp.inf) for t in ps]
    m = [jnp.maximum(jnp.max(s[t], axis=-1, keepdims=True), sink[t]) for t in ps]
    e = [jnp.exp(s[t] - m[t]) for t in ps]
    ov = [_dot(e[t], vslab[t]) for t in ps]
    o = [ov[t][:, :LANES] * (1.0 / (ov[t][:, LANES:] + jnp.exp(sink[t] - m[t]))) for t in ps]
    for j in range(nq):
        outs = [jnp.where(lo, o[t][blk * g:blk * (g + 1)], o[t][blk * (SWA_GROUP + g):blk * (SWA_GROUP + g + 1)])
                for t in ps if parts[t][0] == j for g in range(SWA_GROUP)]
        o_ref[blk * j:blk * (j + 1), :] = jnp.concatenate(outs, axis=1).astype(o_ref.dtype)


def _swa_attention(q, k, v, sinks, batch, seq):
    step = SWA_QBLOCKS * SWA_BLOCK
    ns = seq // step
    cur = lambda b, i: (b * ns + i, 0)
    prev = lambda b, i: (b * ns * SWA_QBLOCKS + jnp.maximum(i * SWA_QBLOCKS - 1, 0), 0)
    return pl.pallas_call(
        _swa_attn_kernel,
        grid=(batch, ns),
        in_specs=[pl.BlockSpec(memory_space=pltpu.SMEM),
                  pl.BlockSpec((step, SWA_QW), cur),
                  pl.BlockSpec((SWA_BLOCK, SWA_KVW), prev),
                  pl.BlockSpec((step, SWA_KVW), cur),
                  pl.BlockSpec((SWA_BLOCK, SWA_KVW), prev),
                  pl.BlockSpec((step, SWA_KVW), cur)],
        out_specs=pl.BlockSpec((step, SWA_QW), cur),
        out_shape=jax.ShapeDtypeStruct((batch * seq, SWA_QW), BF16),
        compiler_params=_cparams(("parallel", "parallel")),
        name="swa_attn",
    )(sinks, q, k, k, v, v)


def _mem_attn_kernel(q_ref, k_ref, v_ref, o_ref):
    q = q_ref[...]
    k = k_ref[0].astype(BF16)
    v = v_ref[0].astype(BF16)
    tq = q.shape[0]
    lo = lax.broadcasted_iota(jnp.int32, (tq, LANES), 1) < MEM_HEAD_DIM
    zero = jnp.zeros((tq, LANES), q.dtype)
    slabs = range(MEM_WIDTH // LANES)
    sls = [slice(LANES * j, LANES * (j + 1)) for j in slabs]
    qs = [jnp.concatenate([jnp.where(lo, q[:, sl], zero), jnp.where(lo, zero, q[:, sl])], axis=0) for sl in sls]
    s = [_dot_nt(qs[j], k[:, sls[j]]) * MEM_HEAD_DIM ** -0.5 for j in slabs]
    e = [jnp.exp(s[j] - jnp.max(s[j], axis=-1, keepdims=True)) for j in slabs]
    ones = jnp.ones((k.shape[0], LANES), BF16)
    ov = [_dot(e[j], jnp.concatenate([v[:, sls[j]], ones], axis=1)) for j in slabs]
    o = [ov[j][:, :LANES] * (1.0 / ov[j][:, LANES:]) for j in slabs]
    o_ref[...] = jnp.concatenate([jnp.where(lo, o[j][:tq], o[j][tq:]) for j in slabs], axis=1).astype(o_ref.dtype)


def _mem_attention(qm, mem_k, mem_v, batch, seq, tq):
    nq = seq // tq
    return pl.pallas_call(
        _mem_attn_kernel,
        grid=(batch, nq),
        in_specs=[pl.BlockSpec((tq, MEM_WIDTH), lambda b, i: (b * nq + i, 0)),
                  pl.BlockSpec((1, N_MEM, MEM_WIDTH), lambda b, i: (b, 0, 0)),
                  pl.BlockSpec((1, N_MEM, MEM_WIDTH), lambda b, i: (b, 0, 0))],
        out_specs=pl.BlockSpec((tq, MEM_WIDTH), lambda b, i: (b * nq + i, 0)),
        out_shape=jax.ShapeDtypeStruct((batch * seq, MEM_WIDTH), BF16),
        compiler_params=_cparams(("parallel", "parallel")),
        name="mem_attn",
    )(qm, mem_k, mem_v)


def _decode_attn_kernel(*refs, scale, with_self):
    if with_self:
        q_ref, k_ref, v_ref, seg_ref, kn_ref, vn_ref, sink_ref, o_ref, ko_ref, vo_ref = refs
    else:
        q_ref, k_ref, v_ref, seg_ref, o_ref = refs
    kc = k_ref[...]
    vc = v_ref[...]
    bs, n_keys, width = kc.shape
    seg = seg_ref[...]
    kc_r = _bf16_round(kc)
    vc_r = _bf16_round(vc)

    def head_sums(x2d):
        return _dot_f32(x2d, seg)

    for g in range(q_ref.shape[0]):
        qg = _bf16_round(q_ref[g].astype(F32) * scale)
        s = head_sums((kc_r * qg[:, None, :]).reshape(bs * n_keys, width)).reshape(bs, n_keys, width)
        m = jnp.max(s, axis=1)
        if with_self:
            kn = kn_ref[...]
            vn = vn_ref[...]
            s_self = head_sums(_bf16_round(kn) * qg)
            sink = sink_ref[g]
            m = jnp.maximum(jnp.maximum(m, s_self), sink)
        e = jnp.exp(s - m[:, None, :])
        den = jnp.sum(e, axis=1)
        if with_self:
            e_self = jnp.exp(s_self - m)
            den = den + e_self + jnp.exp(sink - m)
        inv = 1.0 / den
        o = jnp.sum(_bf16_round(e * inv[:, None, :]) * vc_r, axis=1)
        if with_self:
            o = o + _bf16_round(e_self * inv) * _bf16_round(vn)
        o_ref[g] = o
    if with_self:
        ko_ref[:, 0:n_keys - 1, :] = kc[:, 1:n_keys, :]
        ko_ref[:, n_keys - 1:n_keys, :] = kn[:, None, :]
        vo_ref[:, 0:n_keys - 1, :] = vc[:, 1:n_keys, :]
        vo_ref[:, n_keys - 1:n_keys, :] = vn[:, None, :]


def _head_segments(width, head_dim):
    lane = jnp.arange(width) // head_dim
    return (lane[:, None] == lane[None, :]).astype(F32)


def _decode_attention(q_groups, cache_k, cache_v, scale, head_dim, bs, layer=0, new_k=None, new_v=None, sinks=None):
    n_groups, batch, width = q_groups.shape
    first = layer * (batch // bs)
    n_keys = cache_k.shape[1]
    with_self = new_k is not None
    seg = _head_segments(width, head_dim)
    blk3 = pl.BlockSpec((bs, n_keys, width), lambda i: (i, 0, 0))
    cache3 = pl.BlockSpec((bs, n_keys, width), lambda i: (first + i, 0, 0))
    in_specs = [pl.BlockSpec((n_groups, bs, width), lambda i: (0, i, 0)), cache3, cache3,
                pl.BlockSpec((width, width), lambda i: (0, 0))]
    args = [q_groups, cache_k, cache_v, seg]
    out_specs = [pl.BlockSpec((n_groups, bs, width), lambda i: (0, i, 0))]
    out_shape = [jax.ShapeDtypeStruct((n_groups, batch, width), F32)]
    if with_self:
        in_specs += [pl.BlockSpec((bs, width), lambda i: (i, 0)), pl.BlockSpec((bs, width), lambda i: (i, 0)),
                     pl.BlockSpec((n_groups, 1, width), lambda i: (0, 0, 0))]
        args += [new_k, new_v, sinks]
        out_specs += [blk3, blk3]
        out_shape += [jax.ShapeDtypeStruct((batch, n_keys, width), F32)] * 2
    return pl.pallas_call(
        functools.partial(_decode_attn_kernel, scale=scale, with_self=with_self),
        grid=(batch // bs,),
        in_specs=in_specs, out_specs=out_specs, out_shape=out_shape,
        compiler_params=_cparams(("parallel",)),
        name="decode_attn",
    )(*args)


def _outproj_kernel(x_ref, h_ref, m_ref, w1_ref, w2_ref, g_ref, b_ref, o_ref):
    t = _dot(h_ref[...], w1_ref[...]) + _dot(m_ref[...], w2_ref[...])
    o_ref[...] = _layer_norm(DEEPNORM_ALPHA * x_ref[...] + t, g_ref[...], b_ref[...])


def _out_project(x, h, mem, w_out, g, b, tm):
    m = x.shape[0]
    hw = h.shape[1]
    w1, w2 = w_out[:hw], w_out[hw:]
    row = lambda i: (i, 0)
    fixed = lambda i: (0, 0)
    return pl.pallas_call(
        _outproj_kernel,
        grid=(m // tm,),
        in_specs=[pl.BlockSpec((tm, D_MODEL), row), pl.BlockSpec((tm, hw), row),
                  pl.BlockSpec((tm, MEM_WIDTH), row), pl.BlockSpec(w1.shape, fixed),
                  pl.BlockSpec(w2.shape, fixed), pl.BlockSpec((1, D_MODEL), fixed),
                  pl.BlockSpec((1, D_MODEL), fixed)],
        out_specs=pl.BlockSpec((tm, D_MODEL), row),
        out_shape=jax.ShapeDtypeStruct((m, D_MODEL), F32),
        compiler_params=_cparams(("parallel",)),
        name="out_proj",
    )(x, h, mem, w1, w2, g.reshape(1, -1), b.reshape(1, -1))


def _top2(logits):
    lane = lax.broadcasted_iota(jnp.int32, logits.shape, 1)
    valid = lane < N_EXPERTS
    lg = jnp.where(valid, logits, -jnp.inf)
    ex = jnp.exp(lg - jnp.max(lg, axis=-1, keepdims=True))
    probs = ex / jnp.sum(ex, axis=-1, keepdims=True)
    cand = jnp.where(valid, probs, -1.0)
    p1 = jnp.max(cand, axis=-1, keepdims=True)
    i1 = jnp.min(jnp.where(cand == p1, lane, LANES), axis=-1, keepdims=True)
    cand = jnp.where(lane == i1, -1.0, cand)
    p2 = jnp.max(cand, axis=-1, keepdims=True)
    i2 = jnp.min(jnp.where(cand == p2, lane, LANES), axis=-1, keepdims=True)
    tot = p1 + p2
    return p1 / tot, i1, p2 / tot, i2


def _router_gates(xb, rw):
    g1, i1, g2, i2 = _top2(_dot(xb, rw))
    lane = lax.broadcasted_iota(jnp.int32, (xb.shape[0], LANES), 1)
    return jnp.where(lane == i1, g1, 0.0) + jnp.where(lane == i2, g2, 0.0)


def _pad_router(router_w):
    return jnp.pad(router_w, ((0, 0), (0, LANES - N_EXPERTS))).astype(BF16)


def _ffn_kernel(*refs, moe):
    if moe:
        x_ref, wg_ref, wu_ref, wd_ref, rw_ref, g_ref, b_ref, o_ref, xb_ref, acc_ref, gate_ref = refs
    else:
        x_ref, wg_ref, wu_ref, wd_ref, g_ref, b_ref, o_ref, xb_ref, acc_ref = refs
    e = pl.program_id(1)
    f = pl.program_id(2)

    @pl.when((e == 0) & (f == 0))
    def _():
        xb_ref[...] = x_ref[...].astype(xb_ref.dtype)
        acc_ref[...] = jnp.zeros_like(acc_ref)
        if moe:
            gate_ref[...] = _router_gates(xb_ref[...], rw_ref[...])

    xb = xb_ref[...]
    gt = _dot(xb, wg_ref[0])
    up = _dot(xb, wu_ref[0])
    y = _dot(jax.nn.silu(gt) * up, wd_ref[0])
    if moe:
        lane = lax.broadcasted_iota(jnp.int32, gate_ref.shape, 1)
        y = y * jnp.sum(jnp.where(lane == e, gate_ref[...], 0.0), axis=-1, keepdims=True)
    acc_ref[...] += y

    @pl.when((e == pl.num_programs(1) - 1) & (f == pl.num_programs(2) - 1))
    def _():
        o_ref[...] = _layer_norm(DEEPNORM_ALPHA * x_ref[...] + acc_ref[...], g_ref[...], b_ref[...])


def _ffn(x, w_gu, w_down, g, b, tm, tf, router_w=None):
    m = x.shape[0]
    n_exp, _, two_f = w_gu.shape
    nf = two_f // 2 // tf
    moe = router_w is not None
    row = lambda i, e, f: (i, 0)
    fixed = lambda i, e, f: (0, 0)
    in_specs = [pl.BlockSpec((tm, D_MODEL), row),
                pl.BlockSpec((1, D_MODEL, tf), lambda i, e, f: (e, 0, f)),
                pl.BlockSpec((1, D_MODEL, tf), lambda i, e, f: (e, 0, nf + f)),
                pl.BlockSpec((1, tf, D_MODEL), lambda i, e, f: (e, f, 0))]
    args = [x, w_gu, w_gu, w_down]
    scratch = [pltpu.VMEM((tm, D_MODEL), BF16), pltpu.VMEM((tm, D_MODEL), F32)]
    if moe:
        in_specs.append(pl.BlockSpec((D_MODEL, LANES), fixed))
        args.append(_pad_router(router_w))
        scratch.append(pltpu.VMEM((tm, LANES), F32))
    in_specs += [pl.BlockSpec((1, D_MODEL), fixed), pl.BlockSpec((1, D_MODEL), fixed)]
    args += [g.reshape(1, -1), b.reshape(1, -1)]
    return pl.pallas_call(
        functools.partial(_ffn_kernel, moe=moe),
        grid=(m // tm, n_exp, nf),
        in_specs=in_specs,
        out_specs=pl.BlockSpec((tm, D_MODEL), row),
        out_shape=jax.ShapeDtypeStruct((m, D_MODEL), F32),
        scratch_shapes=scratch,
        compiler_params=_cparams(("parallel", "arbitrary", "arbitrary")),
        name="moe_ffn" if moe else "ffn",
    )(*args)


META_E1, META_E2, META_R1, META_R2 = 0, 1, 2, 3
ZERO_ROWS = 256


def _route_kernel(y_ref, rw_ref, gtop_ref, meta_ref, cnt_ref, carry_ref):
    i = pl.program_id(0)
    tm = y_ref.shape[0]

    @pl.when(i == 0)
    def _():
        carry_ref[...] = jnp.zeros_like(carry_ref)

    g1, i1, g2, i2 = _top2(_dot(y_ref[...], rw_ref[...]))
    lane = lax.broadcasted_iota(jnp.int32, (tm, LANES), 1)
    onehot = ((lane == i1) | (lane == i2)).astype(F32)
    r = lax.broadcasted_iota(jnp.int32, (tm, tm), 0)
    c = lax.broadcasted_iota(jnp.int32, (tm, tm), 1)
    before = _dot((r > c).astype(F32), onehot) + carry_ref[...]
    rank1 = jnp.sum(jnp.where(lane == i1, before, 0.0), axis=-1, keepdims=True).astype(jnp.int32)
    rank2 = jnp.sum(jnp.where(lane == i2, before, 0.0), axis=-1, keepdims=True).astype(jnp.int32)
    gtop_ref[...] = jnp.where(lane == 0, g1, jnp.where(lane == 1, g2, 0.0))
    meta_ref[...] = jnp.where(lane == META_E1, i1, jnp.where(lane == META_E2, i2, jnp.where(
        lane == META_R1, rank1, jnp.where(lane == META_R2, rank2, 0))))
    carry_ref[...] += jnp.sum(onehot, axis=0, keepdims=True)
    cnt_ref[...] = jnp.broadcast_to(carry_ref[...], cnt_ref.shape).astype(jnp.int32)


def _route(y, router_w, tm):
    m = y.shape[0]
    row = lambda i: (i, 0)
    return pl.pallas_call(
        _route_kernel,
        grid=(m // tm,),
        in_specs=[pl.BlockSpec((tm, D_MODEL), row), pl.BlockSpec((D_MODEL, LANES), lambda i: (0, 0))],
        out_specs=[pl.BlockSpec((tm, LANES), row), pl.BlockSpec((tm, LANES), row),
                   pl.BlockSpec((SUBLANES, LANES), lambda i: (0, 0))],
        out_shape=[jax.ShapeDtypeStruct((m, LANES), F32), jax.ShapeDtypeStruct((m, LANES), jnp.int32),
                   jax.ShapeDtypeStruct((SUBLANES, LANES), jnp.int32)],
        scratch_shapes=[pltpu.VMEM((1, LANES), F32)],
        compiler_params=_cparams(("arbitrary",)),
        name="moe_route",
    )(y, _pad_router(router_w))


def _dispatch_plan(meta, cnt, tmg, n_tiles, tm):
    counts = cnt[0, :N_EXPERTS]
    padded = (counts + tmg - 1) // tmg * tmg
    gend = jnp.cumsum(padded)
    gstart = gend - padded
    pos1 = gstart[meta[:, META_E1]] + meta[:, META_R1]
    pos2 = gstart[meta[:, META_E2]] + meta[:, META_R2]
    pos = jnp.concatenate([pos1.reshape(-1, 1, tm), pos2.reshape(-1, 1, tm)], axis=2).astype(jnp.int32)
    n_used = (gend[-1] // tmg).astype(jnp.int32).reshape(1)
    tile_start = jnp.arange(n_tiles, dtype=jnp.int32) * tmg
    tile_expert = jnp.minimum(jnp.sum(tile_start[:, None] >= gend[None, :], axis=1), N_EXPERTS - 1)
    tail = jnp.stack([gend[-1], (n_tiles * tmg - gend[-1]) // ZERO_ROWS])
    pads = jnp.concatenate([jnp.stack([gstart + counts, padded - counts]), tail[:, None]], axis=1).astype(jnp.int32)
    return pos, tile_expert.astype(jnp.int32), n_used, pads


def _row_copy(src_ref, src_row, dst_ref, dst_row, sem):
    return pltpu.make_async_copy(src_ref.at[pl.ds(src_row, 1)], dst_ref.at[pl.ds(dst_row, 1)], sem)


def _dispatch_kernel(pads_ref, pos_ref, x_ref, xs_ref, zero_ref, sem):
    i = pl.program_id(0)
    tm = x_ref.shape[0]

    def scatter(r, k):
        return _row_copy(x_ref, r, xs_ref, pos_ref[0, 0, k * tm + r], sem)

    def start(r, carry):
        scatter(r, 0).start()
        scatter(r, 1).start()
        return carry

    def wait(r, carry):
        scatter(r, 0).wait()
        scatter(r, 1).wait()
        return carry

    lax.fori_loop(0, tm, start, 0, unroll=8)
    lax.fori_loop(0, tm, wait, 0, unroll=8)

    @pl.when(i == pl.num_programs(0) - 1)
    def _():
        zero_ref[...] = jnp.zeros_like(zero_ref)
        for e in range(N_EXPERTS):
            first = pads_ref[0, e]
            n_pad = pads_ref[1, e]
            fill = lambda r: _row_copy(zero_ref, 0, xs_ref, first + r, sem)
            lax.fori_loop(0, n_pad, lambda r, c: (fill(r).start(), c)[1], 0)
            lax.fori_loop(0, n_pad, lambda r, c: (fill(r).wait(), c)[1], 0)
        tail_first = pads_ref[0, N_EXPERTS]
        n_blocks = pads_ref[1, N_EXPERTS]
        fill_tail = lambda r: pltpu.make_async_copy(
            zero_ref, xs_ref.at[pl.ds(pl.multiple_of(tail_first + r * ZERO_ROWS, ZERO_ROWS), ZERO_ROWS)], sem)
        lax.fori_loop(0, n_blocks, lambda r, c: (fill_tail(r).start(), c)[1], 0)
        lax.fori_loop(0, n_blocks, lambda r, c: (fill_tail(r).wait(), c)[1], 0)


def _dispatch(x, pos, pads, n_slots, tm):
    m = x.shape[0]
    return pl.pallas_call(
        _dispatch_kernel,
        grid_spec=pltpu.PrefetchScalarGridSpec(
            num_scalar_prefetch=1,
            grid=(m // tm,),
            in_specs=[pl.BlockSpec((1, 1, 2 * tm), lambda i, pads: (i, 0, 0), memory_space=pltpu.SMEM),
                      pl.BlockSpec((tm, D_MODEL), lambda i, pads: (i, 0))],
            out_specs=pl.BlockSpec(memory_space=pl.ANY),
            scratch_shapes=[pltpu.VMEM((ZERO_ROWS, D_MODEL), F32), pltpu.SemaphoreType.DMA(())]),
        out_shape=jax.ShapeDtypeStruct((n_slots, D_MODEL), F32),
        compiler_params=_cparams(("arbitrary",)),
        name="moe_dispatch",
    )(pads, pos, x)


def _grouped_ffn_kernel(te_ref, nu_ref, xs_ref, wg_ref, wu_ref, wd_ref, o_ref, xb_ref):
    j = pl.program_id(0)
    f = pl.program_id(1)
    used = j < nu_ref[0]

    @pl.when(used)
    def _():
        @pl.when(f == 0)
        def _():
            xb_ref[...] = xs_ref[...].astype(BF16)

        xb = xb_ref[...]
        y = _dot(jax.nn.silu(_dot(xb, wg_ref[0])) * _dot(xb, wu_ref[0]), wd_ref[0])

        @pl.when(f == 0)
        def _():
            o_ref[...] = y

        @pl.when(f > 0)
        def _():
            o_ref[...] += y

    @pl.when(jnp.logical_not(used) & (f == 0))
    def _():
        o_ref[...] = jnp.zeros_like(o_ref)


def _grouped_ffn(xs, w_gu, w_down, tile_expert, n_used, tmg, tf):
    n_slots = xs.shape[0]
    nf = w_down.shape[1] // tf
    tile = lambda j, f, te, nu: (jnp.minimum(j, nu[0] - 1), 0)
    chunk = lambda j, f, nu: jnp.where(j < nu[0], f, nf - 1)
    return pl.pallas_call(
        _grouped_ffn_kernel,
        grid_spec=pltpu.PrefetchScalarGridSpec(
            num_scalar_prefetch=2,
            grid=(n_slots // tmg, nf),
            in_specs=[pl.BlockSpec((tmg, D_MODEL), tile),
                      pl.BlockSpec((1, D_MODEL, tf), lambda j, f, te, nu: (te[j], 0, chunk(j, f, nu))),
                      pl.BlockSpec((1, D_MODEL, tf), lambda j, f, te, nu: (te[j], 0, nf + chunk(j, f, nu))),
                      pl.BlockSpec((1, tf, D_MODEL), lambda j, f, te, nu: (te[j], chunk(j, f, nu), 0))],
            out_specs=pl.BlockSpec((tmg, D_MODEL), lambda j, f, te, nu: (j, 0)),
            scratch_shapes=[pltpu.VMEM((tmg, D_MODEL), BF16)]),
        out_shape=jax.ShapeDtypeStruct((n_slots, D_MODEL), F32),
        compiler_params=_cparams(("arbitrary", "arbitrary")),
        name="moe_grouped",
    )(tile_expert, n_used, xs, w_gu, w_gu, w_down)


def _combine_kernel(pos_ref, posn_ref, x_ref, gt_ref, ys_ref, g_ref, b_ref, o_ref, ybuf, sem):
    i = pl.program_id(0)
    n = pl.num_programs(0)
    tm = x_ref.shape[0]
    slot = i % 2

    def gather(p_ref, s, r, k):
        return _row_copy(ys_ref, p_ref[0, 0, k * tm + r], ybuf.at[s, k], r, sem.at[s])

    def issue(p_ref, s):
        def body(r, carry):
            gather(p_ref, s, r, 0).start()
            gather(p_ref, s, r, 1).start()
            return carry
        lax.fori_loop(0, tm, body, 0, unroll=8)

    @pl.when(i == 0)
    def _():
        issue(pos_ref, 0)

    @pl.when(i + 1 < n)
    def _():
        issue(posn_ref, 1 - slot)

    def wait(r, carry):
        gather(pos_ref, slot, r, 0).wait()
        gather(pos_ref, slot, r, 1).wait()
        return carry

    lax.fori_loop(0, tm, wait, 0, unroll=8)
    gt = gt_ref[...]
    y = gt[:, 0:1] * ybuf[slot, 0] + gt[:, 1:2] * ybuf[slot, 1]
    o_ref[...] = _layer_norm(DEEPNORM_ALPHA * x_ref[...] + y, g_ref[...], b_ref[...])


def _combine(x, gtop, ys, pos, g, b, tm):
    m = x.shape[0]
    n = m // tm
    row = lambda i: (i, 0)
    fixed = lambda i: (0, 0)
    return pl.pallas_call(
        _combine_kernel,
        grid=(n,),
        in_specs=[pl.BlockSpec((1, 1, 2 * tm), lambda i: (i, 0, 0), memory_space=pltpu.SMEM),
                  pl.BlockSpec((1, 1, 2 * tm), lambda i: (jnp.minimum(i + 1, n - 1), 0, 0),
                               memory_space=pltpu.SMEM),
                  pl.BlockSpec((tm, D_MODEL), row), pl.BlockSpec((tm, LANES), row),
                  pl.BlockSpec(memory_space=pl.ANY),
                  pl.BlockSpec((1, D_MODEL), fixed), pl.BlockSpec((1, D_MODEL), fixed)],
        out_specs=pl.BlockSpec((tm, D_MODEL), row),
        out_shape=jax.ShapeDtypeStruct((m, D_MODEL), F32),
        scratch_shapes=[pltpu.VMEM((2, 2, tm, D_MODEL), F32), pltpu.SemaphoreType.DMA((2,))],
        compiler_params=_cparams(("arbitrary",)),
        name="moe_combine",
    )(pos, pos, x, gtop, ys, g.reshape(1, -1), b.reshape(1, -1))


def _moe(x, router_w, w_gu, w_down, g, b, tmg, tf, tm):
    m = x.shape[0]
    n_tiles = -(-(2 * m + N_EXPERTS * (tmg - 1)) // tmg)
    gtop, meta, cnt = _route(x, router_w, tm)
    pos, tile_expert, n_used, pads = _dispatch_plan(meta, cnt, tmg, n_tiles, tm)
    xs = _dispatch(x, pos, pads, n_tiles * tmg, tm)
    ys = _grouped_ffn(xs, w_gu, w_down, tile_expert, n_used, tmg, tf)
    return _combine(x, gtop, ys, pos, g, b, tm)


def _gdn_conv_post(conv, col0, q_ref, k_ref, v_ref):
    half = 0.5 * conv
    c = half + half * jnp.tanh(half)
    for j in range(conv.shape[1] // GDN_DK):
        col = col0 + GDN_DK * j
        x = c[:, GDN_DK * j:GDN_DK * (j + 1)]
        if col < GDN_QK:
            q_ref[:, col:col + GDN_DK] = x * lax.rsqrt(jnp.sum(x * x, axis=-1, keepdims=True) + L2_EPS) * GDN_DK ** -0.5
        elif col < 2 * GDN_QK:
            k_ref[:, col - GDN_QK:col - GDN_QK + GDN_DK] = x * lax.rsqrt(jnp.sum(x * x, axis=-1, keepdims=True) + L2_EPS)
        else:
            v_ref[:, col - 2 * GDN_QK:col - 2 * GDN_QK + GDN_DK] = x


def _gdn_gates(ab, alog_ref, dtb_ref, gb_ref):
    lane = lax.broadcasted_iota(jnp.int32, ab.shape, 1)
    decay = -jnp.exp(alog_ref[...]) * jax.nn.softplus(ab + dtb_ref[...])
    gb_ref[...] = jnp.where(lane < GDN_HEADS, decay, jax.nn.sigmoid(ab))


GDN_Z0 = GDN_CONV_DIM
GDN_AB0 = GDN_CONV_DIM + GDN_VW
GDN_QM0 = GDN_AB0 + GDN_AB_PAD
GDN_PROJ_GROUP = 2 * LANES


def _gdn_proj_kernel(x_ref, w_ref, cw_ref, alog_ref, dtb_ref,
                     q_ref, k_ref, v_ref, z_ref, gb_ref, qm_ref, cs_ref, buf_ref):
    i = pl.program_id(1)
    tm = x_ref.shape[0]
    pad = SUBLANES
    xb = x_ref[...].astype(BF16)

    @pl.when(i == 0)
    def _():
        buf_ref[0:pad, :] = jnp.zeros((pad, GDN_CONV_DIM), F32)

    def finish(c0, c1, d):
        cols = slice(c0, c1)
        if c1 <= GDN_CONV_DIM:
            buf_ref[pad:pad + tm, cols] = d
            conv = buf_ref[pad - 3:pad - 3 + tm, cols] * cw_ref[0:1, cols]
            conv = conv + buf_ref[pad - 2:pad - 2 + tm, cols] * cw_ref[1:2, cols]
            conv = conv + buf_ref[pad - 1:pad - 1 + tm, cols] * cw_ref[2:3, cols]
            conv = conv + d * cw_ref[3:4, cols]
            _gdn_conv_post(conv, c0, q_ref, k_ref, v_ref)
        elif c1 <= GDN_AB0:
            z_ref[:, c0 - GDN_Z0:c1 - GDN_Z0] = d
        else:
            _gdn_gates(d[:, :GDN_AB_PAD], alog_ref, dtb_ref, gb_ref)
            qm_ref[...] = d[:, GDN_AB_PAD:].astype(qm_ref.dtype)

    bounds = list(range(0, GDN_AB0, GDN_PROJ_GROUP)) + [GDN_AB0, w_ref.shape[1]]
    pending = None
    for c0, c1 in zip(bounds[:-1], bounds[1:]):
        d = _dot(xb, w_ref[:, c0:c1])
        if pending is not None:
            finish(*pending)
        pending = (c0, c1, d)
    finish(*pending)
    tail = buf_ref[tm:tm + pad, :]
    buf_ref[0:pad, :] = tail
    cs_ref[0] = tail


def _gdn_gate_params(a_log, dt_bias):
    padv = lambda v: jnp.pad(v.astype(F32), (0, GDN_AB_PAD - GDN_HEADS)).reshape(1, GDN_AB_PAD)
    return padv(a_log), padv(dt_bias)


def _gdn_pad_w_in(w_in):
    o3 = GDN_CONV_DIM + GDN_VW
    o4 = o3 + 2 * GDN_HEADS
    ab = jnp.pad(w_in[:, o3:o4], ((0, 0), (0, GDN_AB_PAD - 2 * GDN_HEADS)))
    return jnp.concatenate([w_in[:, :o3], ab, w_in[:, o4:]], axis=1)


def _gdn_project(x, w_pad, conv_w, a_log, dt_bias, batch, seq, tm):
    m = x.shape[0]
    nt = seq // tm
    row = lambda b, i: (b * nt + i, 0)
    fixed = lambda b, i: (0, 0)
    alog, dtb = _gdn_gate_params(a_log, dt_bias)
    wide = jax.ShapeDtypeStruct((m, GDN_QK), F32)
    return pl.pallas_call(
        _gdn_proj_kernel,
        grid=(batch, nt),
        in_specs=[pl.BlockSpec((tm, D_MODEL), row), pl.BlockSpec(w_pad.shape, fixed),
                  pl.BlockSpec(conv_w.shape, fixed), pl.BlockSpec(alog.shape, fixed),
                  pl.BlockSpec(dtb.shape, fixed)],
        out_specs=[pl.BlockSpec((tm, GDN_QK), row), pl.BlockSpec((tm, GDN_QK), row),
                   pl.BlockSpec((tm, GDN_VW), row), pl.BlockSpec((tm, GDN_VW), row),
                   pl.BlockSpec((tm, GDN_AB_PAD), row), pl.BlockSpec((tm, MEM_WIDTH), row),
                   pl.BlockSpec((1, SUBLANES, GDN_CONV_DIM), lambda b, i: (b, 0, 0))],
        out_shape=[wide, wide, wide, wide, jax.ShapeDtypeStruct((m, GDN_AB_PAD), F32),
                   jax.ShapeDtypeStruct((m, MEM_WIDTH), BF16),
                   jax.ShapeDtypeStruct((batch, SUBLANES, GDN_CONV_DIM), F32)],
        scratch_shapes=[pltpu.VMEM((tm + SUBLANES, GDN_CONV_DIM), F32)],
        compiler_params=_cparams(("parallel", "arbitrary")),
        name="gdn_proj",
    )(x, w_pad, conv_w, alog, dtb)


def _gdn_step_proj_kernel(x_ref, w_ref, cs_ref, cw_ref, alog_ref, dtb_ref,
                          pre_ref, q_ref, k_ref, v_ref, z_ref, gb_ref, qm_ref):
    proj = _dot(x_ref[...], w_ref[...])
    qkv = proj[:, :GDN_CONV_DIM]
    pre_ref[...] = qkv
    cw = cw_ref[...]
    conv = cs_ref[0] * cw[0:1]
    conv = conv + cs_ref[1] * cw[1:2]
    conv = conv + cs_ref[2] * cw[2:3]
    conv = conv + qkv * cw[3:4]
    _gdn_conv_post(conv, 0, q_ref, k_ref, v_ref)
    z_ref[...] = proj[:, GDN_Z0:GDN_AB0]
    _gdn_gates(proj[:, GDN_AB0:GDN_QM0], alog_ref, dtb_ref, gb_ref)
    qm_ref[...] = proj[:, GDN_QM0:].astype(qm_ref.dtype)


def _gdn_step_project(x, w_pad, conv_state, conv_w, a_log, dt_bias):
    m = x.shape[0]
    alog, dtb = _gdn_gate_params(a_log, dt_bias)
    cs = jnp.transpose(conv_state, (1, 0, 2))
    wide = jax.ShapeDtypeStruct((m, GDN_QK), F32)
    return pl.pallas_call(
        _gdn_step_proj_kernel,
        out_shape=[jax.ShapeDtypeStruct((m, GDN_CONV_DIM), F32), wide, wide, wide, wide,
                   jax.ShapeDtypeStruct((m, GDN_AB_PAD), F32), jax.ShapeDtypeStruct((m, MEM_WIDTH), F32)],
        compiler_params=pltpu.CompilerParams(vmem_limit_bytes=VMEM_LIMIT_MB * 1024 * 1024),
        name="gdn_step_proj",
    )(x, w_pad, cs, conv_w, alog, dtb)


def _gated_out(o, z, nw):
    on = o * lax.rsqrt(jnp.mean(o * o, axis=-1, keepdims=True) + RMS_EPS) * nw
    return on * jax.nn.silu(z)


GDN_SUPER = 2 * GDN_CHUNK
GDN_INTRA_ROWS = 256


def _gdn_intra_kernel(q_ref, k_ref, v_ref, gb_ref, u_ref, w_ref, qg_ref, kg_ref, qk_ref, egl_ref):
    n = GDN_SUPER
    c = GDN_CHUNK
    shift = c.bit_length() - 1
    r = lax.broadcasted_iota(jnp.int32, (n, n), 0)
    col = lax.broadcasted_iota(jnp.int32, (n, n), 1)
    same = (r >> shift) == (col >> shift)
    causal = same & (r >= col)
    strict = same & (r > col)
    eye = (r == col).astype(F32)
    tri = causal.astype(F32)
    ones = same.astype(F32)
    groups = [slice(n * j, n * (j + 1)) for j in range(q_ref.shape[0] // n)]
    gb = [gb_ref[rows, :] for rows in groups]
    gc_cols = [_dot_f32(tri, x) for x in gb]
    gc_rows = [lax.dot_general(x, tri, (((0,), (1,)), ((), ())), preferred_element_type=F32,
                               precision=lax.Precision.HIGHEST) for x in gb]
    gl_cols = [_dot_f32(ones, x) for x in gb]
    for j, rows in enumerate(groups):
        egl_ref[rows, :] = jnp.exp(gl_cols[j])
    parts = [(j, h) for j in range(len(groups)) for h in range(GDN_HEADS)]
    idx = [(groups[j], slice(GDN_DK * h, GDN_DK * (h + 1))) for j, h in parts]
    ps = range(len(parts))
    q = [q_ref[i] for i in idx]
    k = [k_ref[i] for i in idx]
    gcol = [gc_cols[j][:, h:h + 1] for j, h in parts]
    beta = [gb[j][:, GDN_HEADS + h:GDN_HEADS + h + 1] for j, h in parts]
    decay = [jnp.exp(jnp.where(causal, gcol[p] - gc_rows[j][h:h + 1, :], -jnp.inf)) for p, (j, h) in enumerate(parts)]
    kb = [k[p] * beta[p] for p in ps]
    a = [jnp.where(strict, _dot_nt(kb[p], k[p]) * decay[p], 0.0) for p in ps]
    qk = [jnp.where(causal, _dot_nt(q[p], k[p]) * decay[p], 0.0) for p in ps]
    qk = [jnp.concatenate([m[c * i:c * (i + 1), c * i:c * (i + 1)] for i in range(n // c)], axis=0) for m in qk]
    for j, rows in enumerate(groups):
        qk_ref[rows, :] = jnp.concatenate(qk[GDN_HEADS * j:GDN_HEADS * (j + 1)], axis=1).astype(qk_ref.dtype)
    tinv = [eye - a[p] for p in ps]
    apow = a
    for _ in range(shift - 1):
        apow = [_dot(apow[p], apow[p]) for p in ps]
        tinv = [_dot(tinv[p], eye + apow[p]) for p in ps]
    eg = [jnp.exp(gcol[p]) for p in ps]
    for p in ps:
        u_ref[idx[p]] = _dot(tinv[p], v_ref[idx[p]] * beta[p])
    for p in ps:
        w_ref[idx[p]] = _dot(tinv[p], kb[p] * eg[p]).astype(w_ref.dtype)
    for p, (j, h) in enumerate(parts):
        qg_ref[idx[p]] = (q[p] * eg[p]).astype(qg_ref.dtype)
        kg_ref[idx[p]] = (k[p] * jnp.exp(gl_cols[j][:, h:h + 1] - gcol[p])).astype(kg_ref.dtype)


def _gdn_scan_kernel(u_ref, w_ref, qg_ref, kg_ref, qk_ref, egl_ref, z_ref, nw_ref, o_ref, sfin_ref, s_ref):
    n = pl.program_id(0)
    c = GDN_CHUNK

    @pl.when(n == 0)
    def _():
        s_ref[...] = jnp.zeros_like(s_ref)

    nw = nw_ref[...]

    def per_sequence(b, carry):
        egl = egl_ref[b, 0:1, :]
        heads = range(GDN_HEADS)
        sls = [slice(GDN_DK * h, GDN_DK * (h + 1)) for h in heads]
        state = [s_ref[b, h] for h in heads]
        ws_qs = [_dot(jnp.concatenate([w_ref[b, :, sls[h]], qg_ref[b, :, sls[h]]], axis=0), state[h])
                 for h in heads]
        v_new = [(u_ref[b, :, sls[h]] - ws_qs[h][:c]).astype(BF16) for h in heads]
        pairs = [qk_ref[b, :, LANES * j:LANES * (j + 1)] for j in range(GDN_HEADS // 2)]
        o = [ws_qs[h][c:] + _dot(pairs[h // 2][:, c * (h % 2):c * (h % 2 + 1)], v_new[h]) for h in heads]
        upd = [_dot_tn(kg_ref[b, :, sls[h]], v_new[h]) for h in heads]
        for h in heads:
            s_ref[b, h] = state[h] * egl[:, h:h + 1] + upd[h]
        for h in heads:
            o_ref[b, :, sls[h]] = _gated_out(o[h], z_ref[b, :, sls[h]], nw).astype(o_ref.dtype)
        return carry

    lax.fori_loop(0, u_ref.shape[0], per_sequence, 0)

    @pl.when(n == pl.num_programs(0) - 1)
    def _():
        sfin_ref[...] = s_ref[...]


def _gdn_chunked(q, k, v, z, gb, norm_w, batch, seq):
    m = batch * seq
    row = lambda i: (i, 0)
    wide = pl.BlockSpec((GDN_INTRA_ROWS, GDN_QK), row)
    qk_w = GDN_HEADS * GDN_CHUNK
    u, w, qg, kg, qk, egl = pl.pallas_call(
        _gdn_intra_kernel,
        grid=(m // GDN_INTRA_ROWS,),
        in_specs=[wide, wide, wide, pl.BlockSpec((GDN_INTRA_ROWS, GDN_AB_PAD), row)],
        out_specs=[wide, wide, wide, wide, pl.BlockSpec((GDN_INTRA_ROWS, qk_w), row),
                   pl.BlockSpec((GDN_INTRA_ROWS, LANES), row)],
        out_shape=[jax.ShapeDtypeStruct((m, GDN_VW), F32), jax.ShapeDtypeStruct((m, GDN_QK), BF16),
                   jax.ShapeDtypeStruct((m, GDN_QK), BF16), jax.ShapeDtypeStruct((m, GDN_QK), BF16),
                   jax.ShapeDtypeStruct((m, qk_w), BF16), jax.ShapeDtypeStruct((m, LANES), F32)],
        compiler_params=_cparams(("parallel",)),
        name="gdn_intra",
    )(q, k, v, gb)
    per_seq = lambda a: a.reshape(batch, seq, a.shape[-1])
    chunk = lambda width: pl.BlockSpec((batch, GDN_CHUNK, width), lambda n: (0, n, 0))
    state_spec = pl.BlockSpec((batch, GDN_HEADS, GDN_DK, GDN_DV), lambda n: (0, 0, 0, 0))
    h, s_fin = pl.pallas_call(
        _gdn_scan_kernel,
        grid=(seq // GDN_CHUNK,),
        in_specs=[chunk(GDN_VW), chunk(GDN_QK), chunk(GDN_QK), chunk(GDN_QK), chunk(qk_w), chunk(LANES),
                  chunk(GDN_VW), pl.BlockSpec((1, GDN_DV), lambda n: (0, 0))],
        out_specs=[chunk(GDN_VW), state_spec],
        out_shape=[jax.ShapeDtypeStruct((batch, seq, GDN_VW), BF16),
                   jax.ShapeDtypeStruct((batch, GDN_HEADS, GDN_DK, GDN_DV), F32)],
        scratch_shapes=[pltpu.VMEM((batch, GDN_HEADS, GDN_DK, GDN_DV), F32)],
        compiler_params=_cparams(("arbitrary",)),
        name="gdn_scan",
    )(per_seq(u), per_seq(w), per_seq(qg), per_seq(kg), per_seq(qk), per_seq(egl), per_seq(z),
      norm_w.reshape(1, -1))
    return h.reshape(m, GDN_VW), s_fin


def _gdn_recurrent_kernel(q_ref, k_ref, v_ref, z_ref, gb_ref, nw_ref, s_ref, o_ref, so_ref):
    bs = q_ref.shape[0]
    gb = gb_ref[...]
    nw = nw_ref[...]
    for h in range(GDN_HEADS):
        sl = slice(GDN_DK * h, GDN_DK * (h + 1))
        q_t = q_ref[:, sl].T
        k_t = k_ref[:, sl].T
        v = v_ref[:, sl]
        rows = []
        for b in range(bs):
            state = s_ref[b, h] * jnp.exp(gb[b:b + 1, h:h + 1])
            kcol = k_t[:, b:b + 1]
            v_new = (v[b:b + 1, :] - jnp.sum(kcol * state, axis=0, keepdims=True)) \
                * gb[b:b + 1, GDN_HEADS + h:GDN_HEADS + h + 1]
            state = state + kcol * v_new
            so_ref[b, h] = state
            rows.append(jnp.sum(q_t[:, b:b + 1] * state, axis=0, keepdims=True))
        o_ref[:, sl] = _gated_out(jnp.concatenate(rows, axis=0), z_ref[:, sl], nw).astype(o_ref.dtype)


def _gdn_recurrent(q, k, v, z, gb, norm_w, state, bs):
    batch = q.shape[0]
    row = lambda i: (i, 0)
    wide = pl.BlockSpec((bs, GDN_QK), row)
    st = pl.BlockSpec((bs, GDN_HEADS, GDN_DK, GDN_DV), lambda i: (i, 0, 0, 0))
    return pl.pallas_call(
        _gdn_recurrent_kernel,
        grid=(batch // bs,),
        in_specs=[wide, wide, wide, wide, pl.BlockSpec((bs, GDN_AB_PAD), row),
                  pl.BlockSpec((1, GDN_DV), lambda i: (0, 0)), st],
        out_specs=[wide, st],
        out_shape=[jax.ShapeDtypeStruct((batch, GDN_VW), F32), jax.ShapeDtypeStruct(state.shape, F32)],
        compiler_params=_cparams(("parallel",)),
        name="gdn_recurrent",
    )(q, k, v, z, gb, norm_w.reshape(1, -1), state)


PROMPT_TM = 512
GDN_TM = 512
FFN_TM = 1024
FFN_TF = 512
MOE_TMG = 1024
MOE_TM = 512
DECODE_BS = 8


def kernel(x_prompt, x_sample, cache_swa_k, cache_swa_v, state_gdn_conv, state_gdn_rec, cache_mem_k, cache_mem_v, mem_prompt, w_in_swa, swa_sinks, w_in_gdn, gdn_conv_w, gdn_a_log, gdn_dt_bias, gdn_norm_w, w_mem_kv, w_out, ln1_g, ln1_b, ln2_g, ln2_b, ffn_w_gu, ffn_w_down, router_w, moe_w_gu, moe_w_down):
    batch, seq, _ = x_prompt.shape
    dec = x_sample.shape[0]
    assert x_sample.shape[1] == 1
    yp = x_prompt.reshape(batch * seq, D_MODEL)
    ys = x_sample.reshape(dec, D_MODEL)
    mem_flat = mem_prompt.reshape(batch * N_MEM, D_MODEL)
    mem3 = lambda a: a.reshape(-1, N_MEM, MEM_WIDTH)

    pair_perm = _swa_pair_perm()
    w_in0 = jnp.concatenate([w_in_swa[0][:, :SWA_QW][:, pair_perm], w_in_swa[0][:, SWA_QW:]], axis=1).astype(BF16)
    w_out0 = w_out[0].astype(BF16)
    w_out0_paired = jnp.concatenate([w_out0[:SWA_QW][pair_perm], w_out0[SWA_QW:]], axis=0)
    w_gu0 = ffn_w_gu[0:1].astype(BF16)
    w_dn0 = ffn_w_down[0:1].astype(BF16)
    mkv = _matmul(mem_flat, w_mem_kv[0].astype(BF16), PROMPT_TM)
    mk0, mv0 = mkv[:, :MEM_WIDTH], mkv[:, MEM_WIDTH:]
    cos_p, sin_p = _rope_tables(jnp.arange(seq, dtype=jnp.int32))
    q, k, v, qm, k_tail, v_tail = _swa_project(yp, w_in0, cos_p, sin_p, PROMPT_TM)
    h = _swa_attention(q, k, v, swa_sinks[0], batch, seq)
    ma = _mem_attention(qm, mem3(mk0), mem3(mv0), batch, seq, PROMPT_TM)
    swa_kp = k_tail.reshape(batch, WINDOW, SWA_KV_HEADS, HEAD_DIM)
    swa_vp = v_tail.reshape(batch, WINDOW, SWA_KV_HEADS, HEAD_DIM)
    yp = _out_project(yp, h, ma, w_out0_paired, ln1_g[0], ln1_b[0], PROMPT_TM)
    yp = _ffn(yp, w_gu0, w_dn0, ln2_g[0], ln2_b[0], FFN_TM, FFN_TF)

    cos_s, sin_s = _rope_tables(jnp.full((dec,), PAST_LEN, jnp.int32))
    q, k, v, qm, _, _ = _swa_project(ys, w_in0, cos_s, sin_s, dec)
    q = q[:, np.argsort(pair_perm)]
    win = cache_swa_k.shape[2]
    qg = q.reshape(dec, SWA_KV_HEADS, SWA_GROUP, HEAD_DIM).transpose(2, 0, 1, 3).reshape(SWA_GROUP, dec, SWA_KVW)
    sink_g = jnp.repeat(swa_sinks[0].reshape(SWA_KV_HEADS, SWA_GROUP).T, HEAD_DIM, axis=1).reshape(
        SWA_GROUP, 1, SWA_KVW)
    og, swa_ks, swa_vs = _decode_attention(
        qg, cache_swa_k[0].reshape(dec, win, SWA_KVW), cache_swa_v[0].reshape(dec, win, SWA_KVW),
        1.0, HEAD_DIM, DECODE_BS, new_k=k, new_v=v, sinks=sink_g)
    h = og.reshape(SWA_GROUP, dec, SWA_KV_HEADS, HEAD_DIM).transpose(1, 2, 0, 3).reshape(dec, SWA_QW)
    ma, = _decode_attention(qm.reshape(1, dec, MEM_WIDTH), mem3(cache_mem_k), mem3(cache_mem_v),
                            MEM_HEAD_DIM ** -0.5, MEM_HEAD_DIM, DECODE_BS, layer=0)
    ys = _out_project(ys, h, ma[0], w_out0, ln1_g[0], ln1_b[0], dec)
    ys = _ffn(ys, w_gu0, w_dn0, ln2_g[0], ln2_b[0], dec, FFN_TF)

    w_in1 = _gdn_pad_w_in(w_in_gdn[0]).astype(BF16)
    w_out1 = w_out[1].astype(BF16)
    w_gu1 = moe_w_gu[0].astype(BF16)
    w_dn1 = moe_w_down[0].astype(BF16)
    mkv = _matmul(mem_flat, w_mem_kv[1].astype(BF16), PROMPT_TM)
    mk1, mv1 = mkv[:, :MEM_WIDTH], mkv[:, MEM_WIDTH:]
    q, k, v, z, gb, qm, conv_tail = _gdn_project(yp, w_in1, gdn_conv_w[0], gdn_a_log[0], gdn_dt_bias[0],
                                                 batch, seq, GDN_TM)
    h, rec_p = _gdn_chunked(q, k, v, z, gb, gdn_norm_w[0], batch, seq)
    ma = _mem_attention(qm, mem3(mk1), mem3(mv1), batch, seq, PROMPT_TM)
    yp = _out_project(yp, h, ma, w_out1, ln1_g[1], ln1_b[1], PROMPT_TM)
    yp = _moe(yp, router_w[0], w_gu1, w_dn1, ln2_g[1], ln2_b[1], MOE_TMG, FFN_TF, MOE_TM)

    pre, q, k, v, z, gb, qm = _gdn_step_project(ys, w_in1, state_gdn_conv[0], gdn_conv_w[0],
                                                gdn_a_log[0], gdn_dt_bias[0])
    h, rec_s = _gdn_recurrent(q, k, v, z, gb, gdn_norm_w[0], state_gdn_rec[0], DECODE_BS)
    ma, = _decode_attention(qm.reshape(1, dec, MEM_WIDTH), mem3(cache_mem_k), mem3(cache_mem_v),
                            MEM_HEAD_DIM ** -0.5, MEM_HEAD_DIM, DECODE_BS, layer=1)
    ys = _out_project(ys, h, ma[0], w_out1, ln1_g[1], ln1_b[1], dec)
    ys = _ffn(ys, w_gu1, w_dn1, ln2_g[1], ln2_b[1], dec, FFN_TF, router_w=router_w[0])

    mem_shape = (batch, N_MEM, MEM_HEADS, MEM_HEAD_DIM)
    conv_s = jnp.concatenate([state_gdn_conv[0][:, 1:], pre[:, None, :]], axis=1)
    return (yp.reshape(batch, seq, D_MODEL), ys.reshape(dec, 1, D_MODEL),
            swa_kp[None], swa_vp[None],
            conv_tail[None, :, SUBLANES - (GDN_CONV_W - 1):, :], rec_p[None],
            jnp.stack([mk0.reshape(mem_shape), mk1.reshape(mem_shape)]),
            jnp.stack([mv0.reshape(mem_shape), mv1.reshape(mem_shape)]),
            swa_ks.reshape(1, dec, win, SWA_KV_HEADS, HEAD_DIM), swa_vs.reshape(1, dec, win, SWA_KV_HEADS, HEAD_DIM),
            conv_s[None], rec_s[None])
```

```python
import functools

import jax
import jax.numpy as jnp
import numpy as np
from jax import lax
from jax.experimental import pallas as pl
from jax.experimental.pallas import tpu as pltpu

F32 = jnp.float32
BF16 = jnp.bfloat16

D_MODEL = 1024
DEPTH = 2
PAST_LEN = 16384
SWA_HEADS = 12
SWA_KV_HEADS = 4
SWA_GROUP = SWA_HEADS // SWA_KV_HEADS
HEAD_DIM = 64
WINDOW = 128
SWA_BLOCK = 128
ROPE_THETA = 10000.0
SWA_QW = SWA_HEADS * HEAD_DIM
SWA_KVW = SWA_KV_HEADS * HEAD_DIM
GDN_HEADS = 6
GDN_DK = 128
GDN_DV = 128
GDN_CONV_W = 4
GDN_CHUNK = 64
GDN_QK = GDN_HEADS * GDN_DK
GDN_VW = GDN_HEADS * GDN_DV
GDN_CONV_DIM = 2 * GDN_QK + GDN_VW
N_MEM = 256
MEM_HEADS = 4
MEM_HEAD_DIM = 64
MEM_WIDTH = MEM_HEADS * MEM_HEAD_DIM
D_FF = 3584
N_EXPERTS = 8
DEEPNORM_ALPHA = (2 * DEPTH) ** 0.25
LN_EPS = 1e-5
RMS_EPS = 1e-6
L2_EPS = 1e-6

LANES = 128
SUBLANES = 8
GDN_AB_PAD = LANES
VMEM_LIMIT_MB = 56


def _cparams(sem, vmem_mb=VMEM_LIMIT_MB):
    return pltpu.CompilerParams(dimension_semantics=sem, vmem_limit_bytes=vmem_mb * 1024 * 1024)


def _bf16_round(x):
    return x.astype(BF16).astype(F32)


def _dot(a, b):
    return jnp.dot(a.astype(BF16), b.astype(BF16), preferred_element_type=F32)


def _dot_nt(a, b):
    return lax.dot_general(a.astype(BF16), b.astype(BF16), (((1,), (1,)), ((), ())),
                           preferred_element_type=F32)


def _dot_tn(a, b):
    return lax.dot_general(a.astype(BF16), b.astype(BF16), (((0,), (0,)), ((), ())),
                           preferred_element_type=F32)


def _dot_f32(a, b):
    return jnp.dot(a, b, preferred_element_type=F32, precision=lax.Precision.HIGHEST)


def _layer_norm(t, g, b):
    mu = jnp.mean(t, axis=-1, keepdims=True)
    d = t - mu
    var = jnp.mean(d * d, axis=-1, keepdims=True)
    return d * lax.rsqrt(var + LN_EPS) * g + b


def _mm_kernel(x_ref, w_ref, o_ref):
    o_ref[...] = _dot(x_ref[...], w_ref[...])


def _matmul(x, w, tm):
    m, k = x.shape
    n = w.shape[1]
    return pl.pallas_call(
        _mm_kernel,
        grid=(m // tm,),
        in_specs=[pl.BlockSpec((tm, k), lambda i: (i, 0)),
                  pl.BlockSpec((k, n), lambda i: (0, 0))],
        out_specs=pl.BlockSpec((tm, n), lambda i: (i, 0)),
        out_shape=jax.ShapeDtypeStruct((m, n), F32),
        compiler_params=_cparams(("parallel",)),
        name="matmul",
    )(x, w)


def _swa_proj_kernel(x_ref, w_ref, cos_ref, sin_ref, q_ref, k_ref, v_ref, qm_ref, kt_ref, vt_ref):
    proj = _dot(x_ref[...], w_ref[...])
    cos = cos_ref[...]
    sin = sin_ref[...]
    lane = lax.broadcasted_iota(jnp.int32, cos.shape, 1)
    first_half = (lane & (HEAD_DIM - 1)) < HEAD_DIM // 2

    def rope(xb):
        partner = jnp.where(first_half, pltpu.roll(xb, LANES - HEAD_DIM // 2, 1),
                            pltpu.roll(xb, HEAD_DIM // 2, 1))
        return xb * cos + partner * sin

    for j in range(SWA_QW // LANES):
        sl = slice(LANES * j, LANES * (j + 1))
        q_ref[:, sl] = (rope(proj[:, sl]) * HEAD_DIM ** -0.5).astype(q_ref.dtype)
    for j in range(SWA_KVW // LANES):
        k_ref[:, LANES * j:LANES * (j + 1)] = rope(proj[:, SWA_QW + LANES * j:SWA_QW + LANES * (j + 1)])
    v_ref[...] = proj[:, SWA_QW + SWA_KVW:SWA_QW + 2 * SWA_KVW]
    qm_ref[...] = proj[:, SWA_QW + 2 * SWA_KVW:].astype(qm_ref.dtype)
    tm = x_ref.shape[0]
    kt_ref[...] = k_ref[tm - WINDOW:, :]
    vt_ref[...] = v_ref[tm - WINDOW:, :]


def _rope_tables(pos):
    half = HEAD_DIM // 2
    inv = ROPE_THETA ** (-jnp.arange(half, dtype=F32) / half)
    ang = pos.astype(F32)[:, None] * inv[None, :]
    cos = jnp.cos(ang)
    sin = jnp.sin(ang)
    reps = LANES // HEAD_DIM
    return jnp.tile(cos, (1, 2 * reps)), jnp.tile(jnp.concatenate([-sin, sin], axis=1), (1, reps))


def _swa_project(x, w, cos, sin, tm):
    m = x.shape[0]
    n_in = w.shape[1]
    tab_blocks = cos.shape[0] // tm
    n_seq = m // (tm * tab_blocks)
    assert tm >= WINDOW
    row = lambda i: (i, 0)
    tail = lambda i: (i // tab_blocks, 0)
    return pl.pallas_call(
        _swa_proj_kernel,
        grid=(m // tm,),
        in_specs=[pl.BlockSpec((tm, D_MODEL), row),
                  pl.BlockSpec((D_MODEL, n_in), lambda i: (0, 0)),
                  pl.BlockSpec((tm, LANES), lambda i: (i % tab_blocks, 0)),
                  pl.BlockSpec((tm, LANES), lambda i: (i % tab_blocks, 0))],
        out_specs=[pl.BlockSpec((tm, SWA_QW), row), pl.BlockSpec((tm, SWA_KVW), row),
                   pl.BlockSpec((tm, SWA_KVW), row), pl.BlockSpec((tm, MEM_WIDTH), row),
                   pl.BlockSpec((WINDOW, SWA_KVW), tail), pl.BlockSpec((WINDOW, SWA_KVW), tail)],
        out_shape=[jax.ShapeDtypeStruct((m, SWA_QW), BF16),
                   jax.ShapeDtypeStruct((m, SWA_KVW), F32),
                   jax.ShapeDtypeStruct((m, SWA_KVW), F32),
                   jax.ShapeDtypeStruct((m, MEM_WIDTH), BF16),
                   jax.ShapeDtypeStruct((n_seq * WINDOW, SWA_KVW), F32),
                   jax.ShapeDtypeStruct((n_seq * WINDOW, SWA_KVW), F32)],
        compiler_params=_cparams(("arbitrary",)),
        name="swa_proj",
    )(x, w, cos, sin)


def _swa_pair_heads():
    return [(SWA_GROUP * (2 * p) + g, SWA_GROUP * (2 * p + 1) + g)
            for p in range(SWA_KV_HEADS // 2) for g in range(SWA_GROUP)]


def _swa_pair_perm():
    cols = []
    for a, b in _swa_pair_heads():
        cols += list(range(HEAD_DIM * a, HEAD_DIM * (a + 1))) + list(range(HEAD_DIM * b, HEAD_DIM * (b + 1)))
    return np.asarray(cols, np.int32)


SWA_QBLOCKS = 4


def _swa_attn_kernel(sink_ref, q_ref, kp_ref, kc_ref, vp_ref, vc_ref, o_ref):
    i = pl.program_id(1)
    blk = SWA_BLOCK
    nq = q_ref.shape[0] // blk
    kwin = jnp.concatenate([kp_ref[...], kc_ref[...]], axis=0).astype(BF16)
    vwin = jnp.concatenate([vp_ref[...], vc_ref[...]], axis=0).astype(BF16)
    qi = lax.broadcasted_iota(jnp.int32, (blk, 2 * blk), 0)
    kj = lax.broadcasted_iota(jnp.int32, (blk, 2 * blk), 1)
    band = (kj >= qi) & (kj <= qi + WINDOW)
    lo = lax.broadcasted_iota(jnp.int32, (blk, LANES), 1) < HEAD_DIM
    pair_heads = _swa_pair_heads()
    zero = jnp.zeros((blk, LANES), q_ref.dtype)
    ones = jnp.ones((2 * blk, LANES), BF16)
    parts = [(j, p) for j in range(nq) for p in range(SWA_KV_HEADS // 2)]
    ps = range(len(parts))
    qs, sink, mask, kslab, vslab = [], [], [], [], []
    for j, p in parts:
        ms = range(SWA_GROUP * p, SWA_GROUP * (p + 1))
        blocks = [q_ref[blk * j:blk * (j + 1), LANES * m:LANES * (m + 1)] for m in ms]
        qs.append(jnp.concatenate([jnp.where(lo, x, zero) for x in blocks]
                                  + [jnp.where(lo, zero, x) for x in blocks], axis=0))
        heads = [pair_heads[m][0] for m in ms] + [pair_heads[m][1] for m in ms]
        sink.append(jnp.concatenate([jnp.full((blk, 1), sink_ref[h], F32) for h in heads], axis=0))
        valid = band & ((kj >= blk) | (i > 0)) if j == 0 else band
        mask.append(jnp.concatenate([valid] * (2 * SWA_GROUP), axis=0))
        kslab.append(kwin[blk * j:blk * (j + 2), LANES * p:LANES * (p + 1)])
        vslab.append(jnp.concatenate([vwin[blk * j:blk * (j + 2), LANES * p:LANES * (p + 1)], ones], axis=1))
    s = [jnp.where(mask[t], _dot_nt(qs[t], kslab[t]), -jnp.inf) for t in ps]
    m = [jnp.maximum(jnp.max(s[t], axis=-1, keepdims=True), sink[t]) for t in ps]
    e = [jnp.exp(s[t] - m[t]) for t in ps]
    ov = [_dot(e[t], vslab[t]) for t in ps]
    o = [ov[t][:, :LANES] * (1.0 / (ov[t][:, LANES:] + jnp.exp(sink[t] - m[t]))) for t in ps]
    for j in range(nq):
        outs = [jnp.where(lo, o[t][blk * g:blk * (g + 1)], o[t][blk * (SWA_GROUP + g):blk * (SWA_GROUP + g + 1)])
                for t in ps if parts[t][0] == j for g in range(SWA_GROUP)]
        o_ref[blk * j:blk * (j + 1), :] = jnp.concatenate(outs, axis=1).astype(o_ref.dtype)


def _swa_attention(q, k, v, sinks, batch, seq):
    step = SWA_QBLOCKS * SWA_BLOCK
    ns = seq // step
    cur = lambda b, i: (b * ns + i, 0)
    prev = lambda b, i: (b * ns * SWA_QBLOCKS + jnp.maximum(i * SWA_QBLOCKS - 1, 0), 0)
    return pl.pallas_call(
        _swa_attn_kernel,
        grid=(batch, ns),
        in_specs=[pl.BlockSpec(memory_space=pltpu.SMEM),
                  pl.BlockSpec((step, SWA_QW), cur),
                  pl.BlockSpec((SWA_BLOCK, SWA_KVW), prev),
                  pl.BlockSpec((step, SWA_KVW), cur),
                  pl.BlockSpec((SWA_BLOCK, SWA_KVW), prev),
                  pl.BlockSpec((step, SWA_KVW), cur)],
        out_specs=pl.BlockSpec((step, SWA_QW), cur),
        out_shape=jax.ShapeDtypeStruct((batch * seq, SWA_QW), BF16),
        compiler_params=_cparams(("parallel", "parallel")),
        name="swa_attn",
    )(sinks, q, k, k, v, v)


def _mem_attn_kernel(q_ref, k_ref, v_ref, o_ref):
    q = q_ref[...]
    k = k_ref[0].astype(BF16)
    v = v_ref[0].astype(BF16)
    tq = q.shape[0]
    lo = lax.broadcasted_iota(jnp.int32, (tq, LANES), 1) < MEM_HEAD_DIM
    zero = jnp.zeros((tq, LANES), q.dtype)
    slabs = range(MEM_WIDTH // LANES)
    sls = [slice(LANES * j, LANES * (j + 1)) for j in slabs]
    qs = [jnp.concatenate([jnp.where(lo, q[:, sl], zero), jnp.where(lo, zero, q[:, sl])], axis=0) for sl in sls]
    s = [_dot_nt(qs[j], k[:, sls[j]]) * MEM_HEAD_DIM ** -0.5 for j in slabs]
    e = [jnp.exp(s[j] - jnp.max(s[j], axis=-1, keepdims=True)) for j in slabs]
    ones = jnp.ones((k.shape[0], LANES), BF16)
    ov = [_dot(e[j], jnp.concatenate([v[:, sls[j]], ones], axis=1)) for j in slabs]
    o = [ov[j][:, :LANES] * (1.0 / ov[j][:, LANES:]) for j in slabs]
    o_ref[...] = jnp.concatenate([jnp.where(lo, o[j][:tq], o[j][tq:]) for j in slabs], axis=1).astype(o_ref.dtype)


def _mem_attention(qm, mem_k, mem_v, batch, seq, tq):
    nq = seq // tq
    return pl.pallas_call(
        _mem_attn_kernel,
        grid=(batch, nq),
        in_specs=[pl.BlockSpec((tq, MEM_WIDTH), lambda b, i: (b * nq + i, 0)),
                  pl.BlockSpec((1, N_MEM, MEM_WIDTH), lambda b, i: (b, 0, 0)),
                  pl.BlockSpec((1, N_MEM, MEM_WIDTH), lambda b, i: (b, 0, 0))],
        out_specs=pl.BlockSpec((tq, MEM_WIDTH), lambda b, i: (b * nq + i, 0)),
        out_shape=jax.ShapeDtypeStruct((batch * seq, MEM_WIDTH), BF16),
        compiler_params=_cparams(("parallel", "parallel")),
        name="mem_attn",
    )(qm, mem_k, mem_v)


def _decode_attn_kernel(*refs, scale, head_dim, with_self):
    if with_self:
        q_ref, kt_ref, vt_ref, kn_ref, vn_ref, knc_ref, vnc_ref, sink_ref, o_ref, ko_ref, vo_ref = refs
    else:
        q_ref, kt_ref, vt_ref, o_ref = refs
    bs, width, n_keys = kt_ref.shape
    n_groups = q_ref.shape[0]
    hp = SUBLANES
    shift = head_dim.bit_length() - 1
    own = lax.broadcasted_iota(jnp.int32, (hp, width), 0) == (lax.broadcasted_iota(jnp.int32, (hp, width), 1) >> shift)
    samples = range(bs)
    kt_r = [kt_ref[b].astype(BF16) for b in samples]
    vt_r = [vt_ref[b].astype(BF16) for b in samples]
    qh = [jnp.concatenate([jnp.where(own, _bf16_round(q_ref[g, b:b + 1, :].astype(F32) * scale), 0.0)
                           for g in range(n_groups)], axis=0) for b in samples]
    s = [_dot(qh[b], kt_r[b]) for b in samples]
    m = [jnp.max(s[b], axis=-1, keepdims=True) for b in samples]
    if with_self:
        kn = kn_ref[...]
        vn = vn_ref[...]
        sink = sink_ref[...]
        s_self = [jnp.sum(qh[b] * _bf16_round(kn[b:b + 1, :]), axis=-1, keepdims=True) for b in samples]
        m = [jnp.maximum(jnp.maximum(m[b], s_self[b]), sink) for b in samples]
    e = [jnp.exp(s[b] - m[b]) for b in samples]
    den = [jnp.sum(e[b], axis=-1, keepdims=True) for b in samples]
    if with_self:
        e_self = [jnp.exp(s_self[b] - m[b]) for b in samples]
        den = [den[b] + e_self[b] + jnp.exp(sink - m[b]) for b in samples]
    inv = [1.0 / den[b] for b in samples]
    ov = [_dot_nt(e[b] * inv[b], vt_r[b]) for b in samples]
    if with_self:
        ov = [ov[b] + _bf16_round(e_self[b] * inv[b]) * _bf16_round(vn[b:b + 1, :]) for b in samples]
    for b in samples:
        for g in range(n_groups):
            o_ref[g, b:b + 1, :] = jnp.sum(jnp.where(own, ov[b][hp * g:hp * (g + 1)], 0.0), axis=0, keepdims=True)
    if with_self:
        last_key = lax.broadcasted_iota(jnp.int32, (width, n_keys), 1) == n_keys - 1
        knc = knc_ref[0]
        vnc = vnc_ref[0]
        for b in samples:
            ko_ref[b] = jnp.where(last_key, knc[:, b:b + 1], pltpu.roll(kt_ref[b], n_keys - 1, 1))
            vo_ref[b] = jnp.where(last_key, vnc[:, b:b + 1], pltpu.roll(vt_ref[b], n_keys - 1, 1))


def _sample_columns(x, bs):
    batch, width = x.shape
    return jnp.swapaxes(x.reshape(batch // bs, bs, width), 1, 2)


def _decode_attention(q_groups, cache_kt, cache_vt, scale, head_dim, bs, layer=0, new_k=None, new_v=None, sinks=None):
    n_groups, batch, width = q_groups.shape
    n_keys = cache_kt.shape[2]
    n_heads = width // head_dim
    assert n_heads <= SUBLANES
    nblk = batch // bs
    first = layer * nblk
    with_self = new_k is not None
    cache3 = pl.BlockSpec((bs, width, n_keys), lambda i: (first + i, 0, 0))
    qspec = pl.BlockSpec((n_groups, bs, width), lambda i: (0, i, 0))
    in_specs = [qspec, cache3, cache3]
    args = [q_groups, cache_kt, cache_vt]
    out_specs = [qspec]
    out_shape = [jax.ShapeDtypeStruct((n_groups, batch, width), F32)]
    if with_self:
        row = pl.BlockSpec((bs, width), lambda i: (i, 0))
        col = pl.BlockSpec((1, width, bs), lambda i: (i, 0, 0))
        sink_col = jnp.pad(sinks, ((0, 0), (0, SUBLANES - n_heads))).reshape(n_groups * SUBLANES, 1)
        in_specs += [row, row, col, col, pl.BlockSpec(sink_col.shape, lambda i: (0, 0))]
        args += [new_k, new_v, _sample_columns(new_k, bs), _sample_columns(new_v, bs), sink_col]
        blk3 = pl.BlockSpec((bs, width, n_keys), lambda i: (i, 0, 0))
        out_specs += [blk3, blk3]
        out_shape += [jax.ShapeDtypeStruct((batch, width, n_keys), F32)] * 2
    return pl.pallas_call(
        functools.partial(_decode_attn_kernel, scale=scale, head_dim=head_dim, with_self=with_self),
        grid=(nblk,),
        in_specs=in_specs, out_specs=out_specs, out_shape=out_shape,
        compiler_params=_cparams(("parallel",)),
        name="decode_attn",
    )(*args)


def _feature_major(cache):
    layers, batch, n_keys, heads, hd = cache.shape
    return jnp.transpose(cache, (0, 1, 3, 4, 2)).reshape(layers * batch, heads * hd, n_keys)


def _key_major(cache_t, heads):
    batch, width, n_keys = cache_t.shape
    return jnp.transpose(cache_t.reshape(batch, heads, width // heads, n_keys), (0, 3, 1, 2))


def _outproj_kernel(x_ref, h_ref, m_ref, w1_ref, w2_ref, g_ref, b_ref, o_ref):
    t = _dot(h_ref[...], w1_ref[...]) + _dot(m_ref[...], w2_ref[...])
    o_ref[...] = _layer_norm(DEEPNORM_ALPHA * x_ref[...] + t, g_ref[...], b_ref[...])


def _out_project(x, h, mem, w_out, g, b, tm):
    m = x.shape[0]
    hw = h.shape[1]
    w1, w2 = w_out[:hw], w_out[hw:]
    row = lambda i: (i, 0)
    fixed = lambda i: (0, 0)
    return pl.pallas_call(
        _outproj_kernel,
        grid=(m // tm,),
        in_specs=[pl.BlockSpec((tm, D_MODEL), row), pl.BlockSpec((tm, hw), row),
                  pl.BlockSpec((tm, MEM_WIDTH), row), pl.BlockSpec(w1.shape, fixed),
                  pl.BlockSpec(w2.shape, fixed), pl.BlockSpec((1, D_MODEL), fixed),
                  pl.BlockSpec((1, D_MODEL), fixed)],
        out_specs=pl.BlockSpec((tm, D_MODEL), row),
        out_shape=jax.ShapeDtypeStruct((m, D_MODEL), F32),
        compiler_params=_cparams(("parallel",)),
        name="out_proj",
    )(x, h, mem, w1, w2, g.reshape(1, -1), b.reshape(1, -1))


def _top2(logits):
    lane = lax.broadcasted_iota(jnp.int32, logits.shape, 1)
    valid = lane < N_EXPERTS
    lg = jnp.where(valid, logits, -jnp.inf)
    ex = jnp.exp(lg - jnp.max(lg, axis=-1, keepdims=True))
    probs = ex / jnp.sum(ex, axis=-1, keepdims=True)
    cand = jnp.where(valid, probs, -1.0)
    p1 = jnp.max(cand, axis=-1, keepdims=True)
    i1 = jnp.min(jnp.where(cand == p1, lane, LANES), axis=-1, keepdims=True)
    cand = jnp.where(lane == i1, -1.0, cand)
    p2 = jnp.max(cand, axis=-1, keepdims=True)
    i2 = jnp.min(jnp.where(cand == p2, lane, LANES), axis=-1, keepdims=True)
    tot = p1 + p2
    return p1 / tot, i1, p2 / tot, i2


def _router_gates(xb, rw):
    g1, i1, g2, i2 = _top2(_dot(xb, rw))
    lane = lax.broadcasted_iota(jnp.int32, (xb.shape[0], LANES), 1)
    return jnp.where(lane == i1, g1, 0.0) + jnp.where(lane == i2, g2, 0.0)


def _pad_router(router_w):
    return jnp.pad(router_w, ((0, 0), (0, LANES - N_EXPERTS))).astype(BF16)


def _ffn_kernel(*refs, moe):
    if moe:
        x_ref, wg_ref, wu_ref, wd_ref, rw_ref, g_ref, b_ref, o_ref, xb_ref, acc_ref, gate_ref = refs
    else:
        x_ref, wg_ref, wu_ref, wd_ref, g_ref, b_ref, o_ref, xb_ref, acc_ref = refs
    e = pl.program_id(1)
    f = pl.program_id(2)

    @pl.when((e == 0) & (f == 0))
    def _():
        xb_ref[...] = x_ref[...].astype(xb_ref.dtype)
        acc_ref[...] = jnp.zeros_like(acc_ref)
        if moe:
            gate_ref[...] = _router_gates(xb_ref[...], rw_ref[...])

    xb = xb_ref[...]
    gt = _dot(xb, wg_ref[0])
    up = _dot(xb, wu_ref[0])
    y = _dot(jax.nn.silu(gt) * up, wd_ref[0])
    if moe:
        lane = lax.broadcasted_iota(jnp.int32, gate_ref.shape, 1)
        y = y * jnp.sum(jnp.where(lane == e, gate_ref[...], 0.0), axis=-1, keepdims=True)
    acc_ref[...] += y

    @pl.when((e == pl.num_programs(1) - 1) & (f == pl.num_programs(2) - 1))
    def _():
        o_ref[...] = _layer_norm(DEEPNORM_ALPHA * x_ref[...] + acc_ref[...], g_ref[...], b_ref[...])


def _ffn(x, w_gu, w_down, g, b, tm, tf, router_w=None):
    m = x.shape[0]
    n_exp, _, two_f = w_gu.shape
    nf = two_f // 2 // tf
    moe = router_w is not None
    row = lambda i, e, f: (i, 0)
    fixed = lambda i, e, f: (0, 0)
    in_specs = [pl.BlockSpec((tm, D_MODEL), row),
                pl.BlockSpec((1, D_MODEL, tf), lambda i, e, f: (e, 0, f)),
                pl.BlockSpec((1, D_MODEL, tf), lambda i, e, f: (e, 0, nf + f)),
                pl.BlockSpec((1, tf, D_MODEL), lambda i, e, f: (e, f, 0))]
    args = [x, w_gu, w_gu, w_down]
    scratch = [pltpu.VMEM((tm, D_MODEL), BF16), pltpu.VMEM((tm, D_MODEL), F32)]
    if moe:
        in_specs.append(pl.BlockSpec((D_MODEL, LANES), fixed))
        args.append(_pad_router(router_w))
        scratch.append(pltpu.VMEM((tm, LANES), F32))
    in_specs += [pl.BlockSpec((1, D_MODEL), fixed), pl.BlockSpec((1, D_MODEL), fixed)]
    args += [g.reshape(1, -1), b.reshape(1, -1)]
    return pl.pallas_call(
        functools.partial(_ffn_kernel, moe=moe),
        grid=(m // tm, n_exp, nf),
        in_specs=in_specs,
        out_specs=pl.BlockSpec((tm, D_MODEL), row),
        out_shape=jax.ShapeDtypeStruct((m, D_MODEL), F32),
        scratch_shapes=scratch,
        compiler_params=_cparams(("parallel", "arbitrary", "arbitrary")),
        name="moe_ffn" if moe else "ffn",
    )(*args)


META_E1, META_E2, META_R1, META_R2 = 0, 1, 2, 3
ZERO_ROWS = 256


def _route_kernel(y_ref, rw_ref, gtop_ref, meta_ref, cnt_ref, carry_ref):
    i = pl.program_id(0)
    tm = y_ref.shape[0]

    @pl.when(i == 0)
    def _():
        carry_ref[...] = jnp.zeros_like(carry_ref)

    g1, i1, g2, i2 = _top2(_dot(y_ref[...], rw_ref[...]))
    lane = lax.broadcasted_iota(jnp.int32, (tm, LANES), 1)
    onehot = ((lane == i1) | (lane == i2)).astype(F32)
    r = lax.broadcasted_iota(jnp.int32, (tm, tm), 0)
    c = lax.broadcasted_iota(jnp.int32, (tm, tm), 1)
    before = _dot((r > c).astype(F32), onehot) + carry_ref[...]
    rank1 = jnp.sum(jnp.where(lane == i1, before, 0.0), axis=-1, keepdims=True)
    rank2 = jnp.sum(jnp.where(lane == i2, before, 0.0), axis=-1, keepdims=True)
    gtop_ref[...] = jnp.where(lane == 0, g1, jnp.where(lane == 1, g2, 0.0))
    meta = jnp.where(lane == META_E1, i1.astype(F32), jnp.where(lane == META_E2, i2.astype(F32), jnp.where(
        lane == META_R1, rank1, jnp.where(lane == META_R2, rank2, 0.0))))
    meta_ref[...] = meta.T[:SUBLANES, :].astype(jnp.int32)
    carry_ref[...] += jnp.sum(onehot, axis=0, keepdims=True)
    cnt_ref[...] = jnp.broadcast_to(carry_ref[...], cnt_ref.shape).astype(jnp.int32)


def _route(y, router_w, tm):
    m = y.shape[0]
    row = lambda i: (i, 0)
    return pl.pallas_call(
        _route_kernel,
        grid=(m // tm,),
        in_specs=[pl.BlockSpec((tm, D_MODEL), row), pl.BlockSpec((D_MODEL, LANES), lambda i: (0, 0))],
        out_specs=[pl.BlockSpec((tm, LANES), row), pl.BlockSpec((SUBLANES, tm), lambda i: (0, i)),
                   pl.BlockSpec((SUBLANES, LANES), lambda i: (0, 0))],
        out_shape=[jax.ShapeDtypeStruct((m, LANES), F32), jax.ShapeDtypeStruct((SUBLANES, m), jnp.int32),
                   jax.ShapeDtypeStruct((SUBLANES, LANES), jnp.int32)],
        scratch_shapes=[pltpu.VMEM((1, LANES), F32)],
        compiler_params=_cparams(("arbitrary",)),
        name="moe_route",
    )(y, _pad_router(router_w))


def _dispatch_plan(meta, cnt, tmg, n_tiles, tm):
    counts = cnt[0, :N_EXPERTS]
    padded = (counts + tmg - 1) // tmg * tmg
    gend = jnp.cumsum(padded)
    gstart = gend - padded
    pos1 = gstart[meta[META_E1]] + meta[META_R1]
    pos2 = gstart[meta[META_E2]] + meta[META_R2]
    pos = jnp.concatenate([pos1.reshape(-1, 1, tm), pos2.reshape(-1, 1, tm)], axis=2).astype(jnp.int32)
    n_used = (gend[-1] // tmg).astype(jnp.int32).reshape(1)
    tile_start = jnp.arange(n_tiles, dtype=jnp.int32) * tmg
    tile_expert = jnp.minimum(jnp.sum(tile_start[:, None] >= gend[None, :], axis=1), N_EXPERTS - 1)
    tail = jnp.stack([gend[-1], (n_tiles * tmg - gend[-1]) // ZERO_ROWS])
    pads = jnp.concatenate([jnp.stack([gstart + counts, padded - counts]), tail[:, None]], axis=1).astype(jnp.int32)
    return pos, tile_expert.astype(jnp.int32), n_used, pads


def _row_copy(src_ref, src_row, dst_ref, dst_row, sem):
    return pltpu.make_async_copy(src_ref.at[pl.ds(src_row, 1)], dst_ref.at[pl.ds(dst_row, 1)], sem)


def _dispatch_kernel(pads_ref, pos_ref, x_ref, xs_ref, zero_ref, sem):
    i = pl.program_id(0)
    tm = x_ref.shape[0]

    def scatter(r, k):
        return _row_copy(x_ref, r, xs_ref, pos_ref[0, 0, k * tm + r], sem)

    def start(r, carry):
        scatter(r, 0).start()
        scatter(r, 1).start()
        return carry

    def wait(r, carry):
        scatter(r, 0).wait()
        scatter(r, 1).wait()
        return carry

    for r in range(tm):
        start(r, 0)
    lax.fori_loop(0, tm, wait, 0, unroll=8)

    @pl.when(i == pl.num_programs(0) - 1)
    def _():
        zero_ref[...] = jnp.zeros_like(zero_ref)
        for e in range(N_EXPERTS):
            first = pads_ref[0, e]
            n_pad = pads_ref[1, e]
            fill = lambda r: _row_copy(zero_ref, 0, xs_ref, first + r, sem)
            lax.fori_loop(0, n_pad, lambda r, c: (fill(r).start(), c)[1], 0)
            lax.fori_loop(0, n_pad, lambda r, c: (fill(r).wait(), c)[1], 0)
        tail_first = pads_ref[0, N_EXPERTS]
        n_blocks = pads_ref[1, N_EXPERTS]
        fill_tail = lambda r: pltpu.make_async_copy(
            zero_ref, xs_ref.at[pl.ds(pl.multiple_of(tail_first + r * ZERO_ROWS, ZERO_ROWS), ZERO_ROWS)], sem)
        lax.fori_loop(0, n_blocks, lambda r, c: (fill_tail(r).start(), c)[1], 0)
        lax.fori_loop(0, n_blocks, lambda r, c: (fill_tail(r).wait(), c)[1], 0)


def _dispatch(x, pos, pads, n_slots, tm):
    m = x.shape[0]
    return pl.pallas_call(
        _dispatch_kernel,
        grid_spec=pltpu.PrefetchScalarGridSpec(
            num_scalar_prefetch=1,
            grid=(m // tm,),
            in_specs=[pl.BlockSpec((1, 1, 2 * tm), lambda i, pads: (i, 0, 0), memory_space=pltpu.SMEM),
                      pl.BlockSpec((tm, D_MODEL), lambda i, pads: (i, 0))],
            out_specs=pl.BlockSpec(memory_space=pl.ANY),
            scratch_shapes=[pltpu.VMEM((ZERO_ROWS, D_MODEL), F32), pltpu.SemaphoreType.DMA(())]),
        out_shape=jax.ShapeDtypeStruct((n_slots, D_MODEL), F32),
        compiler_params=_cparams(("arbitrary",)),
        name="moe_dispatch",
    )(pads, pos, x)


def _grouped_ffn_kernel(te_ref, nu_ref, xs_ref, wg_ref, wu_ref, wd_ref, o_ref, xb_ref):
    j = pl.program_id(0)
    f = pl.program_id(1)
    used = j < nu_ref[0]

    @pl.when(used)
    def _():
        @pl.when(f == 0)
        def _():
            xb_ref[...] = xs_ref[...].astype(BF16)

        xb = xb_ref[...]
        y = _dot(jax.nn.silu(_dot(xb, wg_ref[0])) * _dot(xb, wu_ref[0]), wd_ref[0])

        @pl.when(f == 0)
        def _():
            o_ref[...] = y

        @pl.when(f > 0)
        def _():
            o_ref[...] += y

    @pl.when(jnp.logical_not(used) & (f == 0))
    def _():
        o_ref[...] = jnp.zeros_like(o_ref)


def _grouped_ffn(xs, w_gu, w_down, tile_expert, n_used, tmg, tf):
    n_slots = xs.shape[0]
    nf = w_down.shape[1] // tf
    tile = lambda j, f, te, nu: (jnp.minimum(j, nu[0] - 1), 0)
    chunk = lambda j, f, nu: jnp.where(j < nu[0], f, nf - 1)
    return pl.pallas_call(
        _grouped_ffn_kernel,
        grid_spec=pltpu.PrefetchScalarGridSpec(
            num_scalar_prefetch=2,
            grid=(n_slots // tmg, nf),
            in_specs=[pl.BlockSpec((tmg, D_MODEL), tile),
                      pl.BlockSpec((1, D_MODEL, tf), lambda j, f, te, nu: (te[j], 0, chunk(j, f, nu))),
                      pl.BlockSpec((1, D_MODEL, tf), lambda j, f, te, nu: (te[j], 0, nf + chunk(j, f, nu))),
                      pl.BlockSpec((1, tf, D_MODEL), lambda j, f, te, nu: (te[j], chunk(j, f, nu), 0))],
            out_specs=pl.BlockSpec((tmg, D_MODEL), lambda j, f, te, nu: (j, 0)),
            scratch_shapes=[pltpu.VMEM((tmg, D_MODEL), BF16)]),
        out_shape=jax.ShapeDtypeStruct((n_slots, D_MODEL), F32),
        compiler_params=_cparams(("arbitrary", "arbitrary")),
        name="moe_grouped",
    )(tile_expert, n_used, xs, w_gu, w_gu, w_down)


def _combine_kernel(pos_ref, posn_ref, x_ref, gt_ref, ys_ref, g_ref, b_ref, o_ref, ybuf, sem):
    i = pl.program_id(0)
    n = pl.num_programs(0)
    tm = x_ref.shape[0]
    slot = i % 2

    def gather(p_ref, s, r, k):
        return _row_copy(ys_ref, p_ref[0, 0, k * tm + r], ybuf.at[s, k], r, sem.at[s])

    def issue(p_ref, s):
        def body(r, carry):
            gather(p_ref, s, r, 0).start()
            gather(p_ref, s, r, 1).start()
            return carry
        for r in range(tm):
            body(r, 0)

    @pl.when(i == 0)
    def _():
        issue(pos_ref, 0)

    @pl.when(i + 1 < n)
    def _():
        issue(posn_ref, 1 - slot)

    def wait(r, carry):
        gather(pos_ref, slot, r, 0).wait()
        gather(pos_ref, slot, r, 1).wait()
        return carry

    lax.fori_loop(0, tm, wait, 0, unroll=8)
    gt = gt_ref[...]
    y = gt[:, 0:1] * ybuf[slot, 0] + gt[:, 1:2] * ybuf[slot, 1]
    o_ref[...] = _layer_norm(DEEPNORM_ALPHA * x_ref[...] + y, g_ref[...], b_ref[...])


def _combine(x, gtop, ys, pos, g, b, tm):
    m = x.shape[0]
    n = m // tm
    row = lambda i: (i, 0)
    fixed = lambda i: (0, 0)
    return pl.pallas_call(
        _combine_kernel,
        grid=(n,),
        in_specs=[pl.BlockSpec((1, 1, 2 * tm), lambda i: (i, 0, 0), memory_space=pltpu.SMEM),
                  pl.BlockSpec((1, 1, 2 * tm), lambda i: (jnp.minimum(i + 1, n - 1), 0, 0),
                               memory_space=pltpu.SMEM),
                  pl.BlockSpec((tm, D_MODEL), row), pl.BlockSpec((tm, LANES), row),
                  pl.BlockSpec(memory_space=pl.ANY),
                  pl.BlockSpec((1, D_MODEL), fixed), pl.BlockSpec((1, D_MODEL), fixed)],
        out_specs=pl.BlockSpec((tm, D_MODEL), row),
        out_shape=jax.ShapeDtypeStruct((m, D_MODEL), F32),
        scratch_shapes=[pltpu.VMEM((2, 2, tm, D_MODEL), F32), pltpu.SemaphoreType.DMA((2,))],
        compiler_params=_cparams(("arbitrary",)),
        name="moe_combine",
    )(pos, pos, x, gtop, ys, g.reshape(1, -1), b.reshape(1, -1))


def _moe(x, router_w, w_gu, w_down, g, b, tmg, tf, tm):
    m = x.shape[0]
    n_tiles = -(-(2 * m + N_EXPERTS * (tmg - 1)) // tmg)
    gtop, meta, cnt = _route(x, router_w, tm)
    pos, tile_expert, n_used, pads = _dispatch_plan(meta, cnt, tmg, n_tiles, tm)
    xs = _dispatch(x, pos, pads, n_tiles * tmg, tm)
    ys = _grouped_ffn(xs, w_gu, w_down, tile_expert, n_used, tmg, tf)
    return _combine(x, gtop, ys, pos, g, b, tm)


def _gdn_conv_post(conv, col0, q_ref, k_ref, v_ref):
    half = 0.5 * conv
    c = half + half * jnp.tanh(half)
    for j in range(conv.shape[1] // GDN_DK):
        col = col0 + GDN_DK * j
        x = c[:, GDN_DK * j:GDN_DK * (j + 1)]
        if col < GDN_QK:
            q_ref[:, col:col + GDN_DK] = x * lax.rsqrt(jnp.sum(x * x, axis=-1, keepdims=True) + L2_EPS) * GDN_DK ** -0.5
        elif col < 2 * GDN_QK:
            k_ref[:, col - GDN_QK:col - GDN_QK + GDN_DK] = x * lax.rsqrt(jnp.sum(x * x, axis=-1, keepdims=True) + L2_EPS)
        else:
            v_ref[:, col - 2 * GDN_QK:col - 2 * GDN_QK + GDN_DK] = x


def _gdn_gates(ab, alog_ref, dtb_ref, gb_ref):
    lane = lax.broadcasted_iota(jnp.int32, ab.shape, 1)
    decay = -jnp.exp(alog_ref[...]) * jax.nn.softplus(ab + dtb_ref[...])
    gb_ref[...] = jnp.where(lane < GDN_HEADS, decay, jax.nn.sigmoid(ab))


GDN_Z0 = GDN_CONV_DIM
GDN_AB0 = GDN_CONV_DIM + GDN_VW
GDN_QM0 = GDN_AB0 + GDN_AB_PAD
GDN_PROJ_GROUP = 2 * LANES


def _gdn_proj_kernel(x_ref, w_ref, cw_ref, alog_ref, dtb_ref,
                     q_ref, k_ref, v_ref, z_ref, gb_ref, qm_ref, cs_ref, buf_ref):
    i = pl.program_id(1)
    tm = x_ref.shape[0]
    pad = SUBLANES
    xb = x_ref[...].astype(BF16)

    @pl.when(i == 0)
    def _():
        buf_ref[0:pad, :] = jnp.zeros((pad, GDN_CONV_DIM), F32)

    def finish(c0, c1, d):
        cols = slice(c0, c1)
        if c1 <= GDN_CONV_DIM:
            buf_ref[pad:pad + tm, cols] = d
            conv = buf_ref[pad - 3:pad - 3 + tm, cols] * cw_ref[0:1, cols]
            conv = conv + buf_ref[pad - 2:pad - 2 + tm, cols] * cw_ref[1:2, cols]
            conv = conv + buf_ref[pad - 1:pad - 1 + tm, cols] * cw_ref[2:3, cols]
            conv = conv + d * cw_ref[3:4, cols]
            _gdn_conv_post(conv, c0, q_ref, k_ref, v_ref)
        elif c1 <= GDN_AB0:
            z_ref[:, c0 - GDN_Z0:c1 - GDN_Z0] = d
        else:
            _gdn_gates(d[:, :GDN_AB_PAD], alog_ref, dtb_ref, gb_ref)
            qm_ref[...] = d[:, GDN_AB_PAD:].astype(qm_ref.dtype)

    bounds = list(range(0, GDN_AB0, GDN_PROJ_GROUP)) + [GDN_AB0, w_ref.shape[1]]
    pending = None
    for c0, c1 in zip(bounds[:-1], bounds[1:]):
        d = _dot(xb, w_ref[:, c0:c1])
        if pending is not None:
            finish(*pending)
        pending = (c0, c1, d)
    finish(*pending)
    tail = buf_ref[tm:tm + pad, :]
    buf_ref[0:pad, :] = tail
    cs_ref[0] = tail


def _gdn_gate_params(a_log, dt_bias):
    padv = lambda v: jnp.pad(v.astype(F32), (0, GDN_AB_PAD - GDN_HEADS)).reshape(1, GDN_AB_PAD)
    return padv(a_log), padv(dt_bias)


def _gdn_pad_w_in(w_in):
    o3 = GDN_CONV_DIM + GDN_VW
    o4 = o3 + 2 * GDN_HEADS
    ab = jnp.pad(w_in[:, o3:o4], ((0, 0), (0, GDN_AB_PAD - 2 * GDN_HEADS)))
    return jnp.concatenate([w_in[:, :o3], ab, w_in[:, o4:]], axis=1)


def _gdn_project(x, w_pad, conv_w, a_log, dt_bias, batch, seq, tm):
    m = x.shape[0]
    nt = seq // tm
    row = lambda b, i: (b * nt + i, 0)
    fixed = lambda b, i: (0, 0)
    alog, dtb = _gdn_gate_params(a_log, dt_bias)
    wide = jax.ShapeDtypeStruct((m, GDN_QK), F32)
    return pl.pallas_call(
        _gdn_proj_kernel,
        grid=(batch, nt),
        in_specs=[pl.BlockSpec((tm, D_MODEL), row), pl.BlockSpec(w_pad.shape, fixed),
                  pl.BlockSpec(conv_w.shape, fixed), pl.BlockSpec(alog.shape, fixed),
                  pl.BlockSpec(dtb.shape, fixed)],
        out_specs=[pl.BlockSpec((tm, GDN_QK), row), pl.BlockSpec((tm, GDN_QK), row),
                   pl.BlockSpec((tm, GDN_VW), row), pl.BlockSpec((tm, GDN_VW), row),
                   pl.BlockSpec((tm, GDN_AB_PAD), row), pl.BlockSpec((tm, MEM_WIDTH), row),
                   pl.BlockSpec((1, SUBLANES, GDN_CONV_DIM), lambda b, i: (b, 0, 0))],
        out_shape=[wide, wide, wide, wide, jax.ShapeDtypeStruct((m, GDN_AB_PAD), F32),
                   jax.ShapeDtypeStruct((m, MEM_WIDTH), BF16),
                   jax.ShapeDtypeStruct((batch, SUBLANES, GDN_CONV_DIM), F32)],
        scratch_shapes=[pltpu.VMEM((tm + SUBLANES, GDN_CONV_DIM), F32)],
        compiler_params=_cparams(("parallel", "arbitrary")),
        name="gdn_proj",
    )(x, w_pad, conv_w, alog, dtb)


def _gdn_step_proj_kernel(x_ref, w_ref, cs_ref, cw_ref, alog_ref, dtb_ref,
                          pre_ref, q_ref, k_ref, v_ref, z_ref, gb_ref, qm_ref):
    proj = _dot(x_ref[...], w_ref[...])
    qkv = proj[:, :GDN_CONV_DIM]
    pre_ref[...] = qkv
    cw = cw_ref[...]
    conv = cs_ref[0] * cw[0:1]
    conv = conv + cs_ref[1] * cw[1:2]
    conv = conv + cs_ref[2] * cw[2:3]
    conv = conv + qkv * cw[3:4]
    _gdn_conv_post(conv, 0, q_ref, k_ref, v_ref)
    z_ref[...] = proj[:, GDN_Z0:GDN_AB0]
    _gdn_gates(proj[:, GDN_AB0:GDN_QM0], alog_ref, dtb_ref, gb_ref)
    qm_ref[...] = proj[:, GDN_QM0:].astype(qm_ref.dtype)


def _gdn_step_project(x, w_pad, conv_state, conv_w, a_log, dt_bias):
    m = x.shape[0]
    alog, dtb = _gdn_gate_params(a_log, dt_bias)
    cs = jnp.transpose(conv_state, (1, 0, 2))
    wide = jax.ShapeDtypeStruct((m, GDN_QK), F32)
    return pl.pallas_call(
        _gdn_step_proj_kernel,
        out_shape=[jax.ShapeDtypeStruct((m, GDN_CONV_DIM), F32), wide, wide, wide, wide,
                   jax.ShapeDtypeStruct((m, GDN_AB_PAD), F32), jax.ShapeDtypeStruct((m, MEM_WIDTH), F32)],
        compiler_params=pltpu.CompilerParams(vmem_limit_bytes=VMEM_LIMIT_MB * 1024 * 1024),
        name="gdn_step_proj",
    )(x, w_pad, cs, conv_w, alog, dtb)


def _gated_out(o, z, nw):
    on = o * lax.rsqrt(jnp.mean(o * o, axis=-1, keepdims=True) + RMS_EPS) * nw
    return on * jax.nn.silu(z)


GDN_SUPER = 2 * GDN_CHUNK
GDN_INTRA_ROWS = 256


def _gdn_intra_kernel(q_ref, k_ref, v_ref, gb_ref, u_ref, w_ref, qg_ref, kg_ref, qk_ref, egl_ref):
    n = GDN_SUPER
    c = GDN_CHUNK
    shift = c.bit_length() - 1
    r = lax.broadcasted_iota(jnp.int32, (n, n), 0)
    col = lax.broadcasted_iota(jnp.int32, (n, n), 1)
    same = (r >> shift) == (col >> shift)
    causal = same & (r >= col)
    strict = same & (r > col)
    eye = (r == col).astype(F32)
    tri = causal.astype(F32)
    ones = same.astype(F32)
    groups = [slice(n * j, n * (j + 1)) for j in range(q_ref.shape[0] // n)]
    gb = [gb_ref[rows, :] for rows in groups]
    gc_cols = [_dot_f32(tri, x) for x in gb]
    gc_rows = [lax.dot_general(x, tri, (((0,), (1,)), ((), ())), preferred_element_type=F32,
                               precision=lax.Precision.HIGHEST) for x in gb]
    gl_cols = [_dot_f32(ones, x) for x in gb]
    for j, rows in enumerate(groups):
        egl_ref[rows, :] = jnp.exp(gl_cols[j])
    parts = [(j, h) for j in range(len(groups)) for h in range(GDN_HEADS)]
    idx = [(groups[j], slice(GDN_DK * h, GDN_DK * (h + 1))) for j, h in parts]
    ps = range(len(parts))
    q = [q_ref[i] for i in idx]
    k = [k_ref[i] for i in idx]
    gcol = [gc_cols[j][:, h:h + 1] for j, h in parts]
    beta = [gb[j][:, GDN_HEADS + h:GDN_HEADS + h + 1] for j, h in parts]
    decay = [jnp.exp(jnp.where(causal, gcol[p] - gc_rows[j][h:h + 1, :], -jnp.inf)) for p, (j, h) in enumerate(parts)]
    kb = [k[p] * beta[p] for p in ps]
    a = [jnp.where(strict, _dot_nt(kb[p], k[p]) * decay[p], 0.0) for p in ps]
    qk = [jnp.where(causal, _dot_nt(q[p], k[p]) * decay[p], 0.0) for p in ps]
    qk = [jnp.concatenate([m[c * i:c * (i + 1), c * i:c * (i + 1)] for i in range(n // c)], axis=0) for m in qk]
    for j, rows in enumerate(groups):
        qk_ref[rows, :] = jnp.concatenate(qk[GDN_HEADS * j:GDN_HEADS * (j + 1)], axis=1).astype(qk_ref.dtype)
    tinv = [eye - a[p] for p in ps]
    apow = a
    for _ in range(shift - 1):
        apow = [_dot(apow[p], apow[p]) for p in ps]
        tinv = [_dot(tinv[p], eye + apow[p]) for p in ps]
    eg = [jnp.exp(gcol[p]) for p in ps]
    for p in ps:
        u_ref[idx[p]] = _dot(tinv[p], v_ref[idx[p]] * beta[p])
    for p in ps:
        w_ref[idx[p]] = _dot(tinv[p], kb[p] * eg[p]).astype(w_ref.dtype)
    for p, (j, h) in enumerate(parts):
        qg_ref[idx[p]] = (q[p] * eg[p]).astype(qg_ref.dtype)
        kg_ref[idx[p]] = (k[p] * jnp.exp(gl_cols[j][:, h:h + 1] - gcol[p])).astype(kg_ref.dtype)


def _gdn_scan_kernel(u_ref, w_ref, qg_ref, kg_ref, qk_ref, egl_ref, z_ref, nw_ref, o_ref, sfin_ref, s_ref):
    n = pl.program_id(0)
    c = GDN_CHUNK

    @pl.when(n == 0)
    def _():
        s_ref[...] = jnp.zeros_like(s_ref)

    nw = nw_ref[...]

    def per_sequence(b, carry):
        egl = egl_ref[b, 0:1, :]
        heads = range(GDN_HEADS)
        sls = [slice(GDN_DK * h, GDN_DK * (h + 1)) for h in heads]
        state = [s_ref[b, h] for h in heads]
        ws_qs = [_dot(jnp.concatenate([w_ref[b, :, sls[h]], qg_ref[b, :, sls[h]]], axis=0), state[h])
                 for h in heads]
        v_new = [(u_ref[b, :, sls[h]] - ws_qs[h][:c]).astype(BF16) for h in heads]
        pairs = [qk_ref[b, :, LANES * j:LANES * (j + 1)] for j in range(GDN_HEADS // 2)]
        o = [ws_qs[h][c:] + _dot(pairs[h // 2][:, c * (h % 2):c * (h % 2 + 1)], v_new[h]) for h in heads]
        upd = [_dot_tn(kg_ref[b, :, sls[h]], v_new[h]) for h in heads]
        for h in heads:
            s_ref[b, h] = state[h] * egl[:, h:h + 1] + upd[h]
        for h in heads:
            o_ref[b, :, sls[h]] = _gated_out(o[h], z_ref[b, :, sls[h]], nw).astype(o_ref.dtype)
        return carry

    lax.fori_loop(0, u_ref.shape[0], per_sequence, 0, unroll=2)

    @pl.when(n == pl.num_programs(0) - 1)
    def _():
        sfin_ref[...] = s_ref[...]


def _gdn_chunked(q, k, v, z, gb, norm_w, batch, seq):
    m = batch * seq
    row = lambda i: (i, 0)
    wide = pl.BlockSpec((GDN_INTRA_ROWS, GDN_QK), row)
    qk_w = GDN_HEADS * GDN_CHUNK
    u, w, qg, kg, qk, egl = pl.pallas_call(
        _gdn_intra_kernel,
        grid=(m // GDN_INTRA_ROWS,),
        in_specs=[wide, wide, wide, pl.BlockSpec((GDN_INTRA_ROWS, GDN_AB_PAD), row)],
        out_specs=[wide, wide, wide, wide, pl.BlockSpec((GDN_INTRA_ROWS, qk_w), row),
                   pl.BlockSpec((GDN_INTRA_ROWS, LANES), row)],
        out_shape=[jax.ShapeDtypeStruct((m, GDN_VW), F32), jax.ShapeDtypeStruct((m, GDN_QK), BF16),
                   jax.ShapeDtypeStruct((m, GDN_QK), BF16), jax.ShapeDtypeStruct((m, GDN_QK), BF16),
                   jax.ShapeDtypeStruct((m, qk_w), BF16), jax.ShapeDtypeStruct((m, LANES), F32)],
        compiler_params=_cparams(("parallel",)),
        name="gdn_intra",
    )(q, k, v, gb)
    per_seq = lambda a: a.reshape(batch, seq, a.shape[-1])
    chunk = lambda width: pl.BlockSpec((batch, GDN_CHUNK, width), lambda n: (0, n, 0))
    state_spec = pl.BlockSpec((batch, GDN_HEADS, GDN_DK, GDN_DV), lambda n: (0, 0, 0, 0))
    h, s_fin = pl.pallas_call(
        _gdn_scan_kernel,
        grid=(seq // GDN_CHUNK,),
        in_specs=[chunk(GDN_VW), chunk(GDN_QK), chunk(GDN_QK), chunk(GDN_QK), chunk(qk_w), chunk(LANES),
                  chunk(GDN_VW), pl.BlockSpec((1, GDN_DV), lambda n: (0, 0))],
        out_specs=[chunk(GDN_VW), state_spec],
        out_shape=[jax.ShapeDtypeStruct((batch, seq, GDN_VW), BF16),
                   jax.ShapeDtypeStruct((batch, GDN_HEADS, GDN_DK, GDN_DV), F32)],
        scratch_shapes=[pltpu.VMEM((batch, GDN_HEADS, GDN_DK, GDN_DV), F32)],
        compiler_params=_cparams(("arbitrary",)),
        name="gdn_scan",
    )(per_seq(u), per_seq(w), per_seq(qg), per_seq(kg), per_seq(qk), per_seq(egl), per_seq(z),
      norm_w.reshape(1, -1))
    return h.reshape(m, GDN_VW), s_fin


def _gdn_recurrent_kernel(q_ref, k_ref, v_ref, z_ref, gb_ref, nw_ref, s_ref, o_ref, so_ref):
    bs = q_ref.shape[0]
    gb = gb_ref[...]
    nw = nw_ref[...]
    for h in range(GDN_HEADS):
        sl = slice(GDN_DK * h, GDN_DK * (h + 1))
        q_t = q_ref[:, sl].T
        k_t = k_ref[:, sl].T
        v = v_ref[:, sl]
        rows = []
        for b in range(bs):
            state = s_ref[b, h] * jnp.exp(gb[b:b + 1, h:h + 1])
            kcol = k_t[:, b:b + 1]
            v_new = (v[b:b + 1, :] - jnp.sum(kcol * state, axis=0, keepdims=True)) \
                * gb[b:b + 1, GDN_HEADS + h:GDN_HEADS + h + 1]
            state = state + kcol * v_new
            so_ref[b, h] = state
            rows.append(jnp.sum(q_t[:, b:b + 1] * state, axis=0, keepdims=True))
        o_ref[:, sl] = _gated_out(jnp.concatenate(rows, axis=0), z_ref[:, sl], nw).astype(o_ref.dtype)


def _gdn_recurrent(q, k, v, z, gb, norm_w, state, bs):
    batch = q.shape[0]
    row = lambda i: (i, 0)
    wide = pl.BlockSpec((bs, GDN_QK), row)
    st = pl.BlockSpec((bs, GDN_HEADS, GDN_DK, GDN_DV), lambda i: (i, 0, 0, 0))
    return pl.pallas_call(
        _gdn_recurrent_kernel,
        grid=(batch // bs,),
        in_specs=[wide, wide, wide, wide, pl.BlockSpec((bs, GDN_AB_PAD), row),
                  pl.BlockSpec((1, GDN_DV), lambda i: (0, 0)), st],
        out_specs=[wide, st],
        out_shape=[jax.ShapeDtypeStruct((batch, GDN_VW), F32), jax.ShapeDtypeStruct(state.shape, F32)],
        compiler_params=_cparams(("parallel",)),
        name="gdn_recurrent",
    )(q, k, v, z, gb, norm_w.reshape(1, -1), state)


PROMPT_TM = 512
GDN_TM = 512
FFN_TM = 1024
FFN_TF = 512
MOE_TMG = 1024
MOE_TM = 512
DECODE_BS = 8


def kernel(x_prompt, x_sample, cache_swa_k, cache_swa_v, state_gdn_conv, state_gdn_rec, cache_mem_k, cache_mem_v, mem_prompt, w_in_swa, swa_sinks, w_in_gdn, gdn_conv_w, gdn_a_log, gdn_dt_bias, gdn_norm_w, w_mem_kv, w_out, ln1_g, ln1_b, ln2_g, ln2_b, ffn_w_gu, ffn_w_down, router_w, moe_w_gu, moe_w_down):
    batch, seq, _ = x_prompt.shape
    dec = x_sample.shape[0]
    assert x_sample.shape[1] == 1
    yp = x_prompt.reshape(batch * seq, D_MODEL)
    ys = x_sample.reshape(dec, D_MODEL)
    mem_flat = mem_prompt.reshape(batch * N_MEM, D_MODEL)
    mem3 = lambda a: a.reshape(-1, N_MEM, MEM_WIDTH)

    pair_perm = _swa_pair_perm()
    w_in0 = jnp.concatenate([w_in_swa[0][:, :SWA_QW][:, pair_perm], w_in_swa[0][:, SWA_QW:]], axis=1).astype(BF16)
    w_out0 = w_out[0].astype(BF16)
    w_out0_paired = jnp.concatenate([w_out0[:SWA_QW][pair_perm], w_out0[SWA_QW:]], axis=0)
    w_gu0 = ffn_w_gu[0:1].astype(BF16)
    w_dn0 = ffn_w_down[0:1].astype(BF16)
    mkv = _matmul(mem_flat, w_mem_kv[0].astype(BF16), PROMPT_TM)
    mk0, mv0 = mkv[:, :MEM_WIDTH], mkv[:, MEM_WIDTH:]
    cos_p, sin_p = _rope_tables(jnp.arange(seq, dtype=jnp.int32))
    q, k, v, qm, k_tail, v_tail = _swa_project(yp, w_in0, cos_p, sin_p, PROMPT_TM)
    h = _swa_attention(q, k, v, swa_sinks[0], batch, seq)
    ma = _mem_attention(qm, mem3(mk0), mem3(mv0), batch, seq, PROMPT_TM)
    swa_kp = k_tail.reshape(batch, WINDOW, SWA_KV_HEADS, HEAD_DIM)
    swa_vp = v_tail.reshape(batch, WINDOW, SWA_KV_HEADS, HEAD_DIM)
    yp = _out_project(yp, h, ma, w_out0_paired, ln1_g[0], ln1_b[0], PROMPT_TM)
    yp = _ffn(yp, w_gu0, w_dn0, ln2_g[0], ln2_b[0], FFN_TM, FFN_TF)

    cos_s, sin_s = _rope_tables(jnp.full((dec,), PAST_LEN, jnp.int32))
    q, k, v, qm, _, _ = _swa_project(ys, w_in0, cos_s, sin_s, dec)
    q = q[:, np.argsort(pair_perm)]
    qg = q.reshape(dec, SWA_KV_HEADS, SWA_GROUP, HEAD_DIM).transpose(2, 0, 1, 3).reshape(SWA_GROUP, dec, SWA_KVW)
    sink_g = swa_sinks[0].reshape(SWA_KV_HEADS, SWA_GROUP).T
    mem_kt, mem_vt = _feature_major(cache_mem_k), _feature_major(cache_mem_v)
    og, swa_ks, swa_vs = _decode_attention(
        qg, _feature_major(cache_swa_k), _feature_major(cache_swa_v),
        1.0, HEAD_DIM, DECODE_BS, new_k=k, new_v=v, sinks=sink_g)
    h = og.reshape(SWA_GROUP, dec, SWA_KV_HEADS, HEAD_DIM).transpose(1, 2, 0, 3).reshape(dec, SWA_QW)
    ma, = _decode_attention(qm.reshape(1, dec, MEM_WIDTH), mem_kt, mem_vt,
                            MEM_HEAD_DIM ** -0.5, MEM_HEAD_DIM, DECODE_BS, layer=0)
    ys = _out_project(ys, h, ma[0], w_out0, ln1_g[0], ln1_b[0], dec)
    ys = _ffn(ys, w_gu0, w_dn0, ln2_g[0], ln2_b[0], dec, FFN_TF)

    w_in1 = _gdn_pad_w_in(w_in_gdn[0]).astype(BF16)
    w_out1 = w_out[1].astype(BF16)
    w_gu1 = moe_w_gu[0]
    w_dn1 = moe_w_down[0]
    mkv = _matmul(mem_flat, w_mem_kv[1].astype(BF16), PROMPT_TM)
    mk1, mv1 = mkv[:, :MEM_WIDTH], mkv[:, MEM_WIDTH:]
    q, k, v, z, gb, qm, conv_tail = _gdn_project(yp, w_in1, gdn_conv_w[0], gdn_a_log[0], gdn_dt_bias[0],
                                                 batch, seq, GDN_TM)
    h, rec_p = _gdn_chunked(q, k, v, z, gb, gdn_norm_w[0], batch, seq)
    ma = _mem_attention(qm, mem3(mk1), mem3(mv1), batch, seq, PROMPT_TM)
    yp = _out_project(yp, h, ma, w_out1, ln1_g[1], ln1_b[1], PROMPT_TM)
    yp = _moe(yp, router_w[0], w_gu1, w_dn1, ln2_g[1], ln2_b[1], MOE_TMG, FFN_TF, MOE_TM)

    pre, q, k, v, z, gb, qm = _gdn_step_project(ys, w_in1, state_gdn_conv[0], gdn_conv_w[0],
                                                gdn_a_log[0], gdn_dt_bias[0])
    h, rec_s = _gdn_recurrent(q, k, v, z, gb, gdn_norm_w[0], state_gdn_rec[0], DECODE_BS)
    ma, = _decode_attention(qm.reshape(1, dec, MEM_WIDTH), mem_kt, mem_vt,
                            MEM_HEAD_DIM ** -0.5, MEM_HEAD_DIM, DECODE_BS, layer=1)
    ys = _out_project(ys, h, ma[0], w_out1, ln1_g[1], ln1_b[1], dec)
    ys = _ffn(ys, w_gu1, w_dn1, ln2_g[1], ln2_b[1], dec, FFN_TF, router_w=router_w[0])

    mem_shape = (batch, N_MEM, MEM_HEADS, MEM_HEAD_DIM)
    conv_s = jnp.concatenate([state_gdn_conv[0][:, 1:], pre[:, None, :]], axis=1)
    return (yp.reshape(batch, seq, D_MODEL), ys.reshape(dec, 1, D_MODEL),
            swa_kp[None], swa_vp[None],
            conv_tail[None, :, SUBLANES - (GDN_CONV_W - 1):, :], rec_p[None],
            jnp.stack([mk0.reshape(mem_shape), mk1.reshape(mem_shape)]),
            jnp.stack([mv0.reshape(mem_shape), mv1.reshape(mem_shape)]),
            _key_major(swa_ks, SWA_KV_HEADS)[None], _key_major(swa_vs, SWA_KV_HEADS)[None],
            conv_s[None], rec_s[None])
```

```python
import functools

import jax
import jax.numpy as jnp
import numpy as np
from jax import lax
from jax.experimental import pallas as pl
from jax.experimental.pallas import tpu as pltpu

F32 = jnp.float32
BF16 = jnp.bfloat16

D_MODEL = 1024
DEPTH = 2
PAST_LEN = 16384
SWA_HEADS = 12
SWA_KV_HEADS = 4
SWA_GROUP = SWA_HEADS // SWA_KV_HEADS
HEAD_DIM = 64
WINDOW = 128
SWA_BLOCK = 128
ROPE_THETA = 10000.0
SWA_QW = SWA_HEADS * HEAD_DIM
SWA_KVW = SWA_KV_HEADS * HEAD_DIM
GDN_HEADS = 6
GDN_DK = 128
GDN_DV = 128
GDN_CONV_W = 4
GDN_CHUNK = 64
GDN_QK = GDN_HEADS * GDN_DK
GDN_VW = GDN_HEADS * GDN_DV
GDN_CONV_DIM = 2 * GDN_QK + GDN_VW
N_MEM = 256
MEM_HEADS = 4
MEM_HEAD_DIM = 64
MEM_WIDTH = MEM_HEADS * MEM_HEAD_DIM
D_FF = 3584
N_EXPERTS = 8
DEEPNORM_ALPHA = (2 * DEPTH) ** 0.25
LN_EPS = 1e-5
RMS_EPS = 1e-6
L2_EPS = 1e-6

LANES = 128
SUBLANES = 8
BF16_ROWS = 16
GDN_AB_PAD = LANES
VMEM_LIMIT_MB = 56


def _cparams(sem, vmem_mb=VMEM_LIMIT_MB):
    return pltpu.CompilerParams(dimension_semantics=sem, vmem_limit_bytes=vmem_mb * 1024 * 1024)


def _bf16_round(x):
    return x.astype(BF16).astype(F32)


def _dot(a, b):
    return jnp.dot(a.astype(BF16), b.astype(BF16), preferred_element_type=F32)


def _dot_nt(a, b):
    return lax.dot_general(a.astype(BF16), b.astype(BF16), (((1,), (1,)), ((), ())),
                           preferred_element_type=F32)


def _dot_tn(a, b):
    return lax.dot_general(a.astype(BF16), b.astype(BF16), (((0,), (0,)), ((), ())),
                           preferred_element_type=F32)


def _dot_f32(a, b):
    return jnp.dot(a, b, preferred_element_type=F32, precision=lax.Precision.HIGHEST)


def _layer_norm(t, g, b):
    mu = jnp.mean(t, axis=-1, keepdims=True)
    d = t - mu
    var = jnp.mean(d * d, axis=-1, keepdims=True)
    return d * lax.rsqrt(var + LN_EPS) * g + b


def _mm_kernel(x_ref, w_ref, o_ref):
    o_ref[...] = _dot(x_ref[...], w_ref[...])


def _matmul(x, w, tm):
    m, k = x.shape
    n = w.shape[1]
    return pl.pallas_call(
        _mm_kernel,
        grid=(m // tm,),
        in_specs=[pl.BlockSpec((tm, k), lambda i: (i, 0)),
                  pl.BlockSpec((k, n), lambda i: (0, 0))],
        out_specs=pl.BlockSpec((tm, n), lambda i: (i, 0)),
        out_shape=jax.ShapeDtypeStruct((m, n), F32),
        compiler_params=_cparams(("parallel",)),
        name="matmul",
    )(x, w)


def _swa_proj_kernel(x_ref, w_ref, cos_ref, sin_ref, q_ref, k_ref, v_ref, qm_ref, kt_ref, vt_ref):
    proj = _dot(x_ref[...], w_ref[...])
    cos = cos_ref[...]
    sin = sin_ref[...]
    lane = lax.broadcasted_iota(jnp.int32, cos.shape, 1)
    first_half = (lane & (HEAD_DIM - 1)) < HEAD_DIM // 2

    def rope(xb):
        partner = jnp.where(first_half, pltpu.roll(xb, LANES - HEAD_DIM // 2, 1),
                            pltpu.roll(xb, HEAD_DIM // 2, 1))
        return xb * cos + partner * sin

    for j in range(SWA_QW // LANES):
        sl = slice(LANES * j, LANES * (j + 1))
        q_ref[:, sl] = (rope(proj[:, sl]) * HEAD_DIM ** -0.5).astype(q_ref.dtype)
    for j in range(SWA_KVW // LANES):
        k_ref[:, LANES * j:LANES * (j + 1)] = rope(proj[:, SWA_QW + LANES * j:SWA_QW + LANES * (j + 1)])
    v_ref[...] = proj[:, SWA_QW + SWA_KVW:SWA_QW + 2 * SWA_KVW]
    qm_ref[...] = proj[:, SWA_QW + 2 * SWA_KVW:].astype(qm_ref.dtype)
    tm = x_ref.shape[0]
    kt_ref[...] = k_ref[tm - WINDOW:, :]
    vt_ref[...] = v_ref[tm - WINDOW:, :]


def _rope_tables(pos):
    half = HEAD_DIM // 2
    inv = ROPE_THETA ** (-jnp.arange(half, dtype=F32) / half)
    ang = pos.astype(F32)[:, None] * inv[None, :]
    cos = jnp.cos(ang)
    sin = jnp.sin(ang)
    reps = LANES // HEAD_DIM
    return jnp.tile(cos, (1, 2 * reps)), jnp.tile(jnp.concatenate([-sin, sin], axis=1), (1, reps))


def _swa_project(x, w, cos, sin, tm):
    m = x.shape[0]
    n_in = w.shape[1]
    tab_blocks = cos.shape[0] // tm
    n_seq = m // (tm * tab_blocks)
    assert tm >= WINDOW
    row = lambda i: (i, 0)
    tail = lambda i: (i // tab_blocks, 0)
    return pl.pallas_call(
        _swa_proj_kernel,
        grid=(m // tm,),
        in_specs=[pl.BlockSpec((tm, D_MODEL), row),
                  pl.BlockSpec((D_MODEL, n_in), lambda i: (0, 0)),
                  pl.BlockSpec((tm, LANES), lambda i: (i % tab_blocks, 0)),
                  pl.BlockSpec((tm, LANES), lambda i: (i % tab_blocks, 0))],
        out_specs=[pl.BlockSpec((tm, SWA_QW), row), pl.BlockSpec((tm, SWA_KVW), row),
                   pl.BlockSpec((tm, SWA_KVW), row), pl.BlockSpec((tm, MEM_WIDTH), row),
                   pl.BlockSpec((WINDOW, SWA_KVW), tail), pl.BlockSpec((WINDOW, SWA_KVW), tail)],
        out_shape=[jax.ShapeDtypeStruct((m, SWA_QW), BF16),
                   jax.ShapeDtypeStruct((m, SWA_KVW), F32),
                   jax.ShapeDtypeStruct((m, SWA_KVW), F32),
                   jax.ShapeDtypeStruct((m, MEM_WIDTH), BF16),
                   jax.ShapeDtypeStruct((n_seq * WINDOW, SWA_KVW), F32),
                   jax.ShapeDtypeStruct((n_seq * WINDOW, SWA_KVW), F32)],
        compiler_params=_cparams(("arbitrary",)),
        name="swa_proj",
    )(x, w, cos, sin)


def _swa_pair_heads():
    return [(SWA_GROUP * (2 * p) + g, SWA_GROUP * (2 * p + 1) + g)
            for p in range(SWA_KV_HEADS // 2) for g in range(SWA_GROUP)]


def _swa_pair_perm():
    cols = []
    for a, b in _swa_pair_heads():
        cols += list(range(HEAD_DIM * a, HEAD_DIM * (a + 1))) + list(range(HEAD_DIM * b, HEAD_DIM * (b + 1)))
    return np.asarray(cols, np.int32)


SWA_QBLOCKS = 8


def _swa_attn_kernel(sink_ref, q_ref, kp_ref, kc_ref, vp_ref, vc_ref, o_ref):
    i = pl.program_id(1)
    blk = SWA_BLOCK
    nq = q_ref.shape[0] // blk
    kwin = jnp.concatenate([kp_ref[...], kc_ref[...]], axis=0).astype(BF16)
    vwin = jnp.concatenate([vp_ref[...], vc_ref[...]], axis=0).astype(BF16)
    qi = lax.broadcasted_iota(jnp.int32, (blk, 2 * blk), 0)
    kj = lax.broadcasted_iota(jnp.int32, (blk, 2 * blk), 1)
    band = (kj >= qi) & (kj <= qi + WINDOW)
    lo = lax.broadcasted_iota(jnp.int32, (blk, LANES), 1) < HEAD_DIM
    pair_heads = _swa_pair_heads()
    zero = jnp.zeros((blk, LANES), q_ref.dtype)
    ones = jnp.ones((2 * blk, LANES), BF16)
    parts = [(j, p) for j in range(nq) for p in range(SWA_KV_HEADS // 2)]
    ps = range(len(parts))
    qs, sink, mask, kslab, vslab = [], [], [], [], []
    for j, p in parts:
        ms = range(SWA_GROUP * p, SWA_GROUP * (p + 1))
        blocks = [q_ref[blk * j:blk * (j + 1), LANES * m:LANES * (m + 1)] for m in ms]
        qs.append(jnp.concatenate([jnp.where(lo, x, zero) for x in blocks]
                                  + [jnp.where(lo, zero, x) for x in blocks], axis=0))
        heads = [pair_heads[m][0] for m in ms] + [pair_heads[m][1] for m in ms]
        sink.append(jnp.concatenate([jnp.full((blk, 1), sink_ref[h], F32) for h in heads], axis=0))
        valid = band & ((kj >= blk) | (i > 0)) if j == 0 else band
        mask.append(jnp.concatenate([valid] * (2 * SWA_GROUP), axis=0))
        kslab.append(kwin[blk * j:blk * (j + 2), LANES * p:LANES * (p + 1)])
        vslab.append(jnp.concatenate([vwin[blk * j:blk * (j + 2), LANES * p:LANES * (p + 1)], ones], axis=1))
    s = [jnp.where(mask[t], _dot_nt(qs[t], kslab[t]), -jnp.inf) for t in ps]
    m = [jnp.maximum(jnp.max(s[t], axis=-1, keepdims=True), sink[t]) for t in ps]
    e = [jnp.exp(s[t] - m[t]) for t in ps]
    ov = [_dot(e[t], vslab[t]) for t in ps]
    o = [ov[t][:, :LANES] * (1.0 / (ov[t][:, LANES:] + jnp.exp(sink[t] - m[t]))) for t in ps]
    for j in range(nq):
        outs = [jnp.where(lo, o[t][blk * g:blk * (g + 1)], o[t][blk * (SWA_GROUP + g):blk * (SWA_GROUP + g + 1)])
                for t in ps if parts[t][0] == j for g in range(SWA_GROUP)]
        o_ref[blk * j:blk * (j + 1), :] = jnp.concatenate(outs, axis=1).astype(o_ref.dtype)


def _swa_attention(q, k, v, sinks, batch, seq):
    step = SWA_QBLOCKS * SWA_BLOCK
    ns = seq // step
    cur = lambda b, i: (b * ns + i, 0)
    prev = lambda b, i: (b * ns * SWA_QBLOCKS + jnp.maximum(i * SWA_QBLOCKS - 1, 0), 0)
    return pl.pallas_call(
        _swa_attn_kernel,
        grid=(batch, ns),
        in_specs=[pl.BlockSpec(memory_space=pltpu.SMEM),
                  pl.BlockSpec((step, SWA_QW), cur),
                  pl.BlockSpec((SWA_BLOCK, SWA_KVW), prev),
                  pl.BlockSpec((step, SWA_KVW), cur),
                  pl.BlockSpec((SWA_BLOCK, SWA_KVW), prev),
                  pl.BlockSpec((step, SWA_KVW), cur)],
        out_specs=pl.BlockSpec((step, SWA_QW), cur),
        out_shape=jax.ShapeDtypeStruct((batch * seq, SWA_QW), BF16),
        compiler_params=_cparams(("parallel", "parallel")),
        name="swa_attn",
    )(sinks, q, k, k, v, v)


def _mem_attn_kernel(q_ref, k_ref, v_ref, o_ref):
    q = q_ref[...]
    k = k_ref[0].astype(BF16)
    v = v_ref[0].astype(BF16)
    tq = q.shape[0]
    lo = lax.broadcasted_iota(jnp.int32, (tq, LANES), 1) < MEM_HEAD_DIM
    zero = jnp.zeros((tq, LANES), q.dtype)
    slabs = range(MEM_WIDTH // LANES)
    sls = [slice(LANES * j, LANES * (j + 1)) for j in slabs]
    qs = [jnp.concatenate([jnp.where(lo, q[:, sl], zero), jnp.where(lo, zero, q[:, sl])], axis=0) for sl in sls]
    s = [_dot_nt(qs[j], k[:, sls[j]]) * MEM_HEAD_DIM ** -0.5 for j in slabs]
    e = [jnp.exp(s[j] - jnp.max(s[j], axis=-1, keepdims=True)) for j in slabs]
    ones = jnp.ones((k.shape[0], LANES), BF16)
    ov = [_dot(e[j], jnp.concatenate([v[:, sls[j]], ones], axis=1)) for j in slabs]
    o = [ov[j][:, :LANES] * (1.0 / ov[j][:, LANES:]) for j in slabs]
    o_ref[...] = jnp.concatenate([jnp.where(lo, o[j][:tq], o[j][tq:]) for j in slabs], axis=1).astype(o_ref.dtype)


def _mem_attention(qm, mem_k, mem_v, batch, seq, tq):
    nq = seq // tq
    return pl.pallas_call(
        _mem_attn_kernel,
        grid=(batch, nq),
        in_specs=[pl.BlockSpec((tq, MEM_WIDTH), lambda b, i: (b * nq + i, 0)),
                  pl.BlockSpec((1, N_MEM, MEM_WIDTH), lambda b, i: (b, 0, 0)),
                  pl.BlockSpec((1, N_MEM, MEM_WIDTH), lambda b, i: (b, 0, 0))],
        out_specs=pl.BlockSpec((tq, MEM_WIDTH), lambda b, i: (b * nq + i, 0)),
        out_shape=jax.ShapeDtypeStruct((batch * seq, MEM_WIDTH), BF16),
        compiler_params=_cparams(("parallel", "parallel")),
        name="mem_attn",
    )(qm, mem_k, mem_v)


def _decode_attn_kernel(*refs, scale, head_dim, with_self):
    if with_self:
        q_ref, kt_ref, vt_ref, kn_ref, vn_ref, knc_ref, vnc_ref, sink_ref, o_ref, ko_ref, vo_ref = refs
    else:
        q_ref, kt_ref, vt_ref, o_ref = refs
    bs, width, n_keys = kt_ref.shape
    n_groups = q_ref.shape[0]
    hp = SUBLANES
    shift = head_dim.bit_length() - 1
    own = lax.broadcasted_iota(jnp.int32, (hp, width), 0) == (lax.broadcasted_iota(jnp.int32, (hp, width), 1) >> shift)
    samples = range(bs)
    kt_r = [kt_ref[b].astype(BF16) for b in samples]
    vt_r = [vt_ref[b].astype(BF16) for b in samples]
    qh = [jnp.concatenate([jnp.where(own, _bf16_round(q_ref[g, b:b + 1, :].astype(F32) * scale), 0.0)
                           for g in range(n_groups)], axis=0) for b in samples]
    s = [_dot(qh[b], kt_r[b]) for b in samples]
    m = [jnp.max(s[b], axis=-1, keepdims=True) for b in samples]
    if with_self:
        kn = kn_ref[...]
        vn = vn_ref[...]
        sink = sink_ref[...]
        s_self = [jnp.sum(qh[b] * _bf16_round(kn[b:b + 1, :]), axis=-1, keepdims=True) for b in samples]
        m = [jnp.maximum(jnp.maximum(m[b], s_self[b]), sink) for b in samples]
    e = [jnp.exp(s[b] - m[b]) for b in samples]
    den = [jnp.sum(e[b], axis=-1, keepdims=True) for b in samples]
    if with_self:
        e_self = [jnp.exp(s_self[b] - m[b]) for b in samples]
        den = [den[b] + e_self[b] + jnp.exp(sink - m[b]) for b in samples]
    inv = [1.0 / den[b] for b in samples]
    ov = [_dot_nt(e[b] * inv[b], vt_r[b]) for b in samples]
    if with_self:
        ov = [ov[b] + _bf16_round(e_self[b] * inv[b]) * _bf16_round(vn[b:b + 1, :]) for b in samples]
    for b in samples:
        for g in range(n_groups):
            o_ref[g, b:b + 1, :] = jnp.sum(jnp.where(own, ov[b][hp * g:hp * (g + 1)], 0.0), axis=0, keepdims=True)
    if with_self:
        last_key = lax.broadcasted_iota(jnp.int32, (width, n_keys), 1) == n_keys - 1
        knc = knc_ref[0]
        vnc = vnc_ref[0]
        for b in samples:
            ko_ref[b] = jnp.where(last_key, knc[:, b:b + 1], pltpu.roll(kt_ref[b], n_keys - 1, 1))
            vo_ref[b] = jnp.where(last_key, vnc[:, b:b + 1], pltpu.roll(vt_ref[b], n_keys - 1, 1))


def _sample_columns(x, bs):
    batch, width = x.shape
    return jnp.swapaxes(x.reshape(batch // bs, bs, width), 1, 2)


def _decode_attention(q_groups, cache_kt, cache_vt, scale, head_dim, bs, layer=0, new_k=None, new_v=None, sinks=None):
    n_groups, batch, width = q_groups.shape
    n_keys = cache_kt.shape[2]
    n_heads = width // head_dim
    assert n_heads <= SUBLANES
    nblk = batch // bs
    first = layer * nblk
    with_self = new_k is not None
    cache3 = pl.BlockSpec((bs, width, n_keys), lambda i: (first + i, 0, 0))
    qspec = pl.BlockSpec((n_groups, bs, width), lambda i: (0, i, 0))
    in_specs = [qspec, cache3, cache3]
    args = [q_groups, cache_kt, cache_vt]
    out_specs = [qspec]
    out_shape = [jax.ShapeDtypeStruct((n_groups, batch, width), F32)]
    if with_self:
        row = pl.BlockSpec((bs, width), lambda i: (i, 0))
        col = pl.BlockSpec((1, width, bs), lambda i: (i, 0, 0))
        sink_col = jnp.pad(sinks, ((0, 0), (0, SUBLANES - n_heads))).reshape(n_groups * SUBLANES, 1)
        in_specs += [row, row, col, col, pl.BlockSpec(sink_col.shape, lambda i: (0, 0))]
        args += [new_k, new_v, _sample_columns(new_k, bs), _sample_columns(new_v, bs), sink_col]
        blk3 = pl.BlockSpec((bs, width, n_keys), lambda i: (i, 0, 0))
        out_specs += [blk3, blk3]
        out_shape += [jax.ShapeDtypeStruct((batch, width, n_keys), F32)] * 2
    return pl.pallas_call(
        functools.partial(_decode_attn_kernel, scale=scale, head_dim=head_dim, with_self=with_self),
        grid=(nblk,),
        in_specs=in_specs, out_specs=out_specs, out_shape=out_shape,
        compiler_params=_cparams(("parallel",)),
        name="decode_attn",
    )(*args)


def _feature_major(cache):
    layers, batch, n_keys, heads, hd = cache.shape
    return jnp.transpose(cache, (0, 1, 3, 4, 2)).reshape(layers * batch, heads * hd, n_keys)


def _key_major(cache_t, heads):
    batch, width, n_keys = cache_t.shape
    return jnp.transpose(cache_t.reshape(batch, heads, width // heads, n_keys), (0, 3, 1, 2))


def _outproj_kernel(x_ref, h_ref, m_ref, w1_ref, w2_ref, g_ref, b_ref, *rest, route):
    t = _dot(h_ref[...], w1_ref[...]) + _dot(m_ref[...], w2_ref[...])
    y = _layer_norm(DEEPNORM_ALPHA * x_ref[...] + t, g_ref[...], b_ref[...])
    if route:
        rw_ref, o_ref, gtop_ref, meta_ref, cnt_ref, carry_ref = rest
        _route_tile(y, rw_ref, gtop_ref, meta_ref, cnt_ref, carry_ref)
    else:
        o_ref, = rest
    o_ref[...] = y


def _out_project(x, h, mem, w_out, g, b, tm, router_w=None):
    m = x.shape[0]
    hw = h.shape[1]
    w1, w2 = w_out[:hw], w_out[hw:]
    row = lambda i: (i, 0)
    fixed = lambda i: (0, 0)
    route = router_w is not None
    in_specs = [pl.BlockSpec((tm, D_MODEL), row), pl.BlockSpec((tm, hw), row),
                pl.BlockSpec((tm, MEM_WIDTH), row), pl.BlockSpec(w1.shape, fixed),
                pl.BlockSpec(w2.shape, fixed), pl.BlockSpec((1, D_MODEL), fixed),
                pl.BlockSpec((1, D_MODEL), fixed)]
    args = [x, h, mem, w1, w2, g.reshape(1, -1), b.reshape(1, -1)]
    out_specs = [pl.BlockSpec((tm, D_MODEL), row)]
    out_shape = [jax.ShapeDtypeStruct((m, D_MODEL), F32)]
    scratch = []
    if route:
        rw_t = jnp.pad(router_w.T, ((0, BF16_ROWS - N_EXPERTS), (0, 0))).astype(BF16)
        in_specs.append(pl.BlockSpec(rw_t.shape, fixed))
        args.append(rw_t)
        out_specs += [pl.BlockSpec((tm, LANES), row), pl.BlockSpec((SUBLANES, tm), lambda i: (0, i)),
                      pl.BlockSpec((SUBLANES, LANES), fixed)]
        out_shape += [jax.ShapeDtypeStruct((m, LANES), F32), jax.ShapeDtypeStruct((SUBLANES, m), jnp.int32),
                      jax.ShapeDtypeStruct((SUBLANES, LANES), jnp.int32)]
        scratch = [pltpu.VMEM((N_EXPERTS, LANES), F32)]
    outs = pl.pallas_call(
        functools.partial(_outproj_kernel, route=route),
        grid=(m // tm,),
        in_specs=in_specs, out_specs=out_specs, out_shape=out_shape, scratch_shapes=scratch,
        compiler_params=_cparams(("arbitrary",) if route else ("parallel",)),
        name="out_proj_route" if route else "out_proj",
    )(*args)
    return outs if route else outs[0]


def _top2(logits):
    lane = lax.broadcasted_iota(jnp.int32, logits.shape, 1)
    valid = lane < N_EXPERTS
    lg = jnp.where(valid, logits, -jnp.inf)
    ex = jnp.exp(lg - jnp.max(lg, axis=-1, keepdims=True))
    probs = ex / jnp.sum(ex, axis=-1, keepdims=True)
    cand = jnp.where(valid, probs, -1.0)
    p1 = jnp.max(cand, axis=-1, keepdims=True)
    i1 = jnp.min(jnp.where(cand == p1, lane, LANES), axis=-1, keepdims=True)
    cand = jnp.where(lane == i1, -1.0, cand)
    p2 = jnp.max(cand, axis=-1, keepdims=True)
    i2 = jnp.min(jnp.where(cand == p2, lane, LANES), axis=-1, keepdims=True)
    tot = p1 + p2
    return p1 / tot, i1, p2 / tot, i2


def _router_gates(xb, rw):
    g1, i1, g2, i2 = _top2(_dot(xb, rw))
    lane = lax.broadcasted_iota(jnp.int32, (xb.shape[0], LANES), 1)
    return jnp.where(lane == i1, g1, 0.0) + jnp.where(lane == i2, g2, 0.0)


def _pad_router(router_w):
    return jnp.pad(router_w, ((0, 0), (0, LANES - N_EXPERTS))).astype(BF16)


def _ffn_kernel(*refs, moe):
    if moe:
        x_ref, wg_ref, wu_ref, wd_ref, rw_ref, g_ref, b_ref, o_ref, xb_ref, acc_ref, gate_ref = refs
    else:
        x_ref, wg_ref, wu_ref, wd_ref, g_ref, b_ref, o_ref, xb_ref, acc_ref = refs
    e = pl.program_id(1)
    f = pl.program_id(2)

    @pl.when((e == 0) & (f == 0))
    def _():
        xb_ref[...] = x_ref[...].astype(xb_ref.dtype)
        acc_ref[...] = jnp.zeros_like(acc_ref)
        if moe:
            gate_ref[...] = _router_gates(xb_ref[...], rw_ref[...])

    xb = xb_ref[...]
    gt = _dot(xb, wg_ref[0])
    up = _dot(xb, wu_ref[0])
    y = _dot(jax.nn.silu(gt) * up, wd_ref[0])
    if moe:
        lane = lax.broadcasted_iota(jnp.int32, gate_ref.shape, 1)
        y = y * jnp.sum(jnp.where(lane == e, gate_ref[...], 0.0), axis=-1, keepdims=True)
    acc_ref[...] += y

    @pl.when((e == pl.num_programs(1) - 1) & (f == pl.num_programs(2) - 1))
    def _():
        o_ref[...] = _layer_norm(DEEPNORM_ALPHA * x_ref[...] + acc_ref[...], g_ref[...], b_ref[...])


def _ffn(x, w_gu, w_down, g, b, tm, tf, router_w=None):
    m = x.shape[0]
    n_exp, _, two_f = w_gu.shape
    nf = two_f // 2 // tf
    moe = router_w is not None
    row = lambda i, e, f: (i, 0)
    fixed = lambda i, e, f: (0, 0)
    in_specs = [pl.BlockSpec((tm, D_MODEL), row),
                pl.BlockSpec((1, D_MODEL, tf), lambda i, e, f: (e, 0, f)),
                pl.BlockSpec((1, D_MODEL, tf), lambda i, e, f: (e, 0, nf + f)),
                pl.BlockSpec((1, tf, D_MODEL), lambda i, e, f: (e, f, 0))]
    args = [x, w_gu, w_gu, w_down]
    scratch = [pltpu.VMEM((tm, D_MODEL), BF16), pltpu.VMEM((tm, D_MODEL), F32)]
    if moe:
        in_specs.append(pl.BlockSpec((D_MODEL, LANES), fixed))
        args.append(_pad_router(router_w))
        scratch.append(pltpu.VMEM((tm, LANES), F32))
    in_specs += [pl.BlockSpec((1, D_MODEL), fixed), pl.BlockSpec((1, D_MODEL), fixed)]
    args += [g.reshape(1, -1), b.reshape(1, -1)]
    return pl.pallas_call(
        functools.partial(_ffn_kernel, moe=moe),
        grid=(m // tm, n_exp, nf),
        in_specs=in_specs,
        out_specs=pl.BlockSpec((tm, D_MODEL), row),
        out_shape=jax.ShapeDtypeStruct((m, D_MODEL), F32),
        scratch_shapes=scratch,
        compiler_params=_cparams(("parallel", "arbitrary", "arbitrary")),
        name="moe_ffn" if moe else "ffn",
    )(*args)


META_E1, META_E2, META_R1, META_R2 = 0, 1, 2, 3
ZERO_ROWS = 256


def _route_tile(y, rw_ref, gtop_ref, meta_ref, cnt_ref, carry_ref):
    i = pl.program_id(0)
    tm = y.shape[0]

    @pl.when(i == 0)
    def _():
        carry_ref[...] = jnp.zeros_like(carry_ref)

    logits = _dot_nt(rw_ref[...], y)[:N_EXPERTS]
    row = lax.broadcasted_iota(jnp.int32, logits.shape, 0)
    ex = jnp.exp(logits - jnp.max(logits, axis=0, keepdims=True))
    probs = ex / jnp.sum(ex, axis=0, keepdims=True)
    p1 = jnp.max(probs, axis=0, keepdims=True)
    i1 = jnp.min(jnp.where(probs == p1, row, N_EXPERTS), axis=0, keepdims=True)
    cand = jnp.where(row == i1, -1.0, probs)
    p2 = jnp.max(cand, axis=0, keepdims=True)
    i2 = jnp.min(jnp.where(cand == p2, row, N_EXPERTS), axis=0, keepdims=True)
    tot = p1 + p2
    onehot = ((row == i1) | (row == i2)).astype(F32)
    r = lax.broadcasted_iota(jnp.int32, (tm, tm), 0)
    c = lax.broadcasted_iota(jnp.int32, (tm, tm), 1)
    before = _dot(onehot, (r < c).astype(F32)) + carry_ref[:, 0:1]
    rank1 = jnp.sum(jnp.where(row == i1, before, 0.0), axis=0, keepdims=True)
    rank2 = jnp.sum(jnp.where(row == i2, before, 0.0), axis=0, keepdims=True)
    meta_ref[...] = jnp.where(row == META_E1, i1.astype(F32), jnp.where(row == META_E2, i2.astype(F32), jnp.where(
        row == META_R1, rank1, jnp.where(row == META_R2, rank2, 0.0)))).astype(jnp.int32)
    gates = jnp.where(row == 0, p1 / tot, jnp.where(row == 1, p2 / tot, 0.0))
    gtop_ref[...] = jnp.concatenate([gates, jnp.zeros((LANES - N_EXPERTS, tm), F32)], axis=0).T
    carry_ref[...] += jnp.sum(onehot, axis=1, keepdims=True)
    cnt_ref[...] = carry_ref[...].astype(jnp.int32)


def _dispatch_plan(meta, cnt, tmg, n_tiles, tm):
    counts = cnt[:N_EXPERTS, 0]
    padded = (counts + tmg - 1) // tmg * tmg
    gend = jnp.cumsum(padded)
    gstart = gend - padded
    pos1 = gstart[meta[META_E1]] + meta[META_R1]
    pos2 = gstart[meta[META_E2]] + meta[META_R2]
    pos = jnp.concatenate([pos1.reshape(-1, 1, tm), pos2.reshape(-1, 1, tm)], axis=2).astype(jnp.int32)
    n_used = (gend[-1] // tmg).astype(jnp.int32).reshape(1)
    tile_start = jnp.arange(n_tiles, dtype=jnp.int32) * tmg
    tile_expert = jnp.minimum(jnp.sum(tile_start[:, None] >= gend[None, :], axis=1), N_EXPERTS - 1)
    tail = jnp.stack([gend[-1], (n_tiles * tmg - gend[-1]) // ZERO_ROWS])
    pads = jnp.concatenate([jnp.stack([gstart + counts, padded - counts]), tail[:, None]], axis=1).astype(jnp.int32)
    return pos, tile_expert.astype(jnp.int32), n_used, pads


def _row_copy(src_ref, src_row, dst_ref, dst_row, sem):
    return pltpu.make_async_copy(src_ref.at[pl.ds(src_row, 1)], dst_ref.at[pl.ds(dst_row, 1)], sem)


def _dispatch_kernel(pads_ref, pos_ref, x_ref, xs_ref, zero_ref, sem):
    i = pl.program_id(0)
    tm = x_ref.shape[0]

    def scatter(r, k):
        return _row_copy(x_ref, r, xs_ref, pos_ref[0, 0, k * tm + r], sem)

    def start(r, carry):
        scatter(r, 0).start()
        scatter(r, 1).start()
        return carry

    def wait(r, carry):
        scatter(r, 0).wait()
        scatter(r, 1).wait()
        return carry

    for r in range(tm):
        start(r, 0)
    lax.fori_loop(0, tm, wait, 0, unroll=8)

    @pl.when(i == pl.num_programs(0) - 1)
    def _():
        zero_ref[...] = jnp.zeros_like(zero_ref)
        for e in range(N_EXPERTS):
            first = pads_ref[0, e]
            n_pad = pads_ref[1, e]
            fill = lambda r: _row_copy(zero_ref, 0, xs_ref, first + r, sem)
            lax.fori_loop(0, n_pad, lambda r, c: (fill(r).start(), c)[1], 0)
            lax.fori_loop(0, n_pad, lambda r, c: (fill(r).wait(), c)[1], 0)
        tail_first = pads_ref[0, N_EXPERTS]
        n_blocks = pads_ref[1, N_EXPERTS]
        fill_tail = lambda r: pltpu.make_async_copy(
            zero_ref, xs_ref.at[pl.ds(pl.multiple_of(tail_first + r * ZERO_ROWS, ZERO_ROWS), ZERO_ROWS)], sem)
        lax.fori_loop(0, n_blocks, lambda r, c: (fill_tail(r).start(), c)[1], 0)
        lax.fori_loop(0, n_blocks, lambda r, c: (fill_tail(r).wait(), c)[1], 0)


def _dispatch(x, pos, pads, n_slots, tm):
    m = x.shape[0]
    return pl.pallas_call(
        _dispatch_kernel,
        grid_spec=pltpu.PrefetchScalarGridSpec(
            num_scalar_prefetch=1,
            grid=(m // tm,),
            in_specs=[pl.BlockSpec((1, 1, 2 * tm), lambda i, pads: (i, 0, 0), memory_space=pltpu.SMEM),
                      pl.BlockSpec((tm, D_MODEL), lambda i, pads: (i, 0))],
            out_specs=pl.BlockSpec(memory_space=pl.ANY),
            scratch_shapes=[pltpu.VMEM((ZERO_ROWS, D_MODEL), F32), pltpu.SemaphoreType.DMA(())]),
        out_shape=jax.ShapeDtypeStruct((n_slots, D_MODEL), F32),
        compiler_params=_cparams(("arbitrary",)),
        name="moe_dispatch",
    )(pads, pos, x)


def _grouped_ffn_kernel(te_ref, nu_ref, xs_ref, wg_ref, wu_ref, wd_ref, o_ref, xb_ref):
    j = pl.program_id(0)
    f = pl.program_id(1)
    used = j < nu_ref[0]

    @pl.when(used)
    def _():
        @pl.when(f == 0)
        def _():
            xb_ref[...] = xs_ref[...].astype(BF16)

        xb = xb_ref[...]
        y = _dot(jax.nn.silu(_dot(xb, wg_ref[0])) * _dot(xb, wu_ref[0]), wd_ref[0])

        @pl.when(f == 0)
        def _():
            o_ref[...] = y

        @pl.when(f > 0)
        def _():
            o_ref[...] += y

    @pl.when(jnp.logical_not(used) & (f == 0))
    def _():
        o_ref[...] = jnp.zeros_like(o_ref)


def _grouped_ffn(xs, w_gu, w_down, tile_expert, n_used, tmg, tf):
    n_slots = xs.shape[0]
    nf = w_down.shape[1] // tf
    tile = lambda j, f, te, nu: (jnp.minimum(j, nu[0] - 1), 0)
    chunk = lambda j, f, nu: jnp.where(j < nu[0], f, nf - 1)
    return pl.pallas_call(
        _grouped_ffn_kernel,
        grid_spec=pltpu.PrefetchScalarGridSpec(
            num_scalar_prefetch=2,
            grid=(n_slots // tmg, nf),
            in_specs=[pl.BlockSpec((tmg, D_MODEL), tile),
                      pl.BlockSpec((1, D_MODEL, tf), lambda j, f, te, nu: (te[j], 0, chunk(j, f, nu))),
                      pl.BlockSpec((1, D_MODEL, tf), lambda j, f, te, nu: (te[j], 0, nf + chunk(j, f, nu))),
                      pl.BlockSpec((1, tf, D_MODEL), lambda j, f, te, nu: (te[j], chunk(j, f, nu), 0))],
            out_specs=pl.BlockSpec((tmg, D_MODEL), lambda j, f, te, nu: (j, 0)),
            scratch_shapes=[pltpu.VMEM((tmg, D_MODEL), BF16)]),
        out_shape=jax.ShapeDtypeStruct((n_slots, D_MODEL), F32),
        compiler_params=_cparams(("arbitrary", "arbitrary")),
        name="moe_grouped",
    )(tile_expert, n_used, xs, w_gu, w_gu, w_down)


def _combine_kernel(pos_ref, posn_ref, x_ref, gt_ref, ys_ref, g_ref, b_ref, o_ref, ybuf, sem):
    i = pl.program_id(0)
    n = pl.num_programs(0)
    tm = x_ref.shape[0]
    slot = i % 2

    def gather(p_ref, s, r, k):
        return _row_copy(ys_ref, p_ref[0, 0, k * tm + r], ybuf.at[s, k], r, sem.at[s])

    def issue(p_ref, s):
        def body(r, carry):
            gather(p_ref, s, r, 0).start()
            gather(p_ref, s, r, 1).start()
            return carry
        for r in range(tm):
            body(r, 0)

    @pl.when(i == 0)
    def _():
        issue(pos_ref, 0)

    @pl.when(i + 1 < n)
    def _():
        issue(posn_ref, 1 - slot)

    def wait(r, carry):
        gather(pos_ref, slot, r, 0).wait()
        gather(pos_ref, slot, r, 1).wait()
        return carry

    lax.fori_loop(0, tm, wait, 0, unroll=8)
    gt = gt_ref[...]
    y = gt[:, 0:1] * ybuf[slot, 0] + gt[:, 1:2] * ybuf[slot, 1]
    o_ref[...] = _layer_norm(DEEPNORM_ALPHA * x_ref[...] + y, g_ref[...], b_ref[...])


def _combine(x, gtop, ys, pos, g, b, tm):
    m = x.shape[0]
    n = m // tm
    row = lambda i: (i, 0)
    fixed = lambda i: (0, 0)
    return pl.pallas_call(
        _combine_kernel,
        grid=(n,),
        in_specs=[pl.BlockSpec((1, 1, 2 * tm), lambda i: (i, 0, 0), memory_space=pltpu.SMEM),
                  pl.BlockSpec((1, 1, 2 * tm), lambda i: (jnp.minimum(i + 1, n - 1), 0, 0),
                               memory_space=pltpu.SMEM),
                  pl.BlockSpec((tm, D_MODEL), row), pl.BlockSpec((tm, LANES), row),
                  pl.BlockSpec(memory_space=pl.ANY),
                  pl.BlockSpec((1, D_MODEL), fixed), pl.BlockSpec((1, D_MODEL), fixed)],
        out_specs=pl.BlockSpec((tm, D_MODEL), row),
        out_shape=jax.ShapeDtypeStruct((m, D_MODEL), F32),
        scratch_shapes=[pltpu.VMEM((2, 2, tm, D_MODEL), F32), pltpu.SemaphoreType.DMA((2,))],
        compiler_params=_cparams(("arbitrary",)),
        name="moe_combine",
    )(pos, pos, x, gtop, ys, g.reshape(1, -1), b.reshape(1, -1))


def _moe(x, gtop, meta, cnt, w_gu, w_down, g, b, tmg, tf, tm):
    m = x.shape[0]
    n_tiles = -(-(2 * m + N_EXPERTS * (tmg - 1)) // tmg)
    pos, tile_expert, n_used, pads = _dispatch_plan(meta, cnt, tmg, n_tiles, tm)
    xs = _dispatch(x, pos, pads, n_tiles * tmg, tm)
    ys = _grouped_ffn(xs, w_gu, w_down, tile_expert, n_used, tmg, tf)
    return _combine(x, gtop, ys, pos, g, b, tm)


def _gdn_conv_post(conv, col0, q_ref, k_ref, v_ref):
    half = 0.5 * conv
    c = half + half * jnp.tanh(half)
    for j in range(conv.shape[1] // GDN_DK):
        col = col0 + GDN_DK * j
        x = c[:, GDN_DK * j:GDN_DK * (j + 1)]
        if col < GDN_QK:
            q_ref[:, col:col + GDN_DK] = x * lax.rsqrt(jnp.sum(x * x, axis=-1, keepdims=True) + L2_EPS) * GDN_DK ** -0.5
        elif col < 2 * GDN_QK:
            k_ref[:, col - GDN_QK:col - GDN_QK + GDN_DK] = x * lax.rsqrt(jnp.sum(x * x, axis=-1, keepdims=True) + L2_EPS)
        else:
            v_ref[:, col - 2 * GDN_QK:col - 2 * GDN_QK + GDN_DK] = x


def _gdn_gates(ab, alog_ref, dtb_ref, gb_ref):
    lane = lax.broadcasted_iota(jnp.int32, ab.shape, 1)
    decay = -jnp.exp(alog_ref[...]) * jax.nn.softplus(ab + dtb_ref[...])
    gb_ref[...] = jnp.where(lane < GDN_HEADS, decay, jax.nn.sigmoid(ab))


GDN_Z0 = GDN_CONV_DIM
GDN_AB0 = GDN_CONV_DIM + GDN_VW
GDN_QM0 = GDN_AB0 + GDN_AB_PAD
GDN_PROJ_GROUP = 2 * LANES


def _gdn_proj_kernel(x_ref, w_ref, cw_ref, alog_ref, dtb_ref,
                     q_ref, k_ref, v_ref, z_ref, gb_ref, qm_ref, cs_ref, buf_ref):
    i = pl.program_id(1)
    tm = x_ref.shape[0]
    pad = SUBLANES
    xb = x_ref[...].astype(BF16)

    @pl.when(i == 0)
    def _():
        buf_ref[0:pad, :] = jnp.zeros((pad, GDN_CONV_DIM), F32)

    def finish(c0, c1, d):
        cols = slice(c0, c1)
        if c1 <= GDN_CONV_DIM:
            buf_ref[pad:pad + tm, cols] = d
            conv = buf_ref[pad - 3:pad - 3 + tm, cols] * cw_ref[0:1, cols]
            conv = conv + buf_ref[pad - 2:pad - 2 + tm, cols] * cw_ref[1:2, cols]
            conv = conv + buf_ref[pad - 1:pad - 1 + tm, cols] * cw_ref[2:3, cols]
            conv = conv + d * cw_ref[3:4, cols]
            _gdn_conv_post(conv, c0, q_ref, k_ref, v_ref)
        elif c1 <= GDN_AB0:
            z_ref[:, c0 - GDN_Z0:c1 - GDN_Z0] = d
        else:
            _gdn_gates(d[:, :GDN_AB_PAD], alog_ref, dtb_ref, gb_ref)
            qm_ref[...] = d[:, GDN_AB_PAD:].astype(qm_ref.dtype)

    bounds = list(range(0, GDN_AB0, GDN_PROJ_GROUP)) + [GDN_AB0, w_ref.shape[1]]
    pending = None
    for c0, c1 in zip(bounds[:-1], bounds[1:]):
        d = _dot(xb, w_ref[:, c0:c1])
        if pending is not None:
            finish(*pending)
        pending = (c0, c1, d)
    finish(*pending)
    tail = buf_ref[tm:tm + pad, :]
    buf_ref[0:pad, :] = tail
    cs_ref[0] = tail


def _gdn_gate_params(a_log, dt_bias):
    padv = lambda v: jnp.pad(v.astype(F32), (0, GDN_AB_PAD - GDN_HEADS)).reshape(1, GDN_AB_PAD)
    return padv(a_log), padv(dt_bias)


def _gdn_pad_w_in(w_in):
    o3 = GDN_CONV_DIM + GDN_VW
    o4 = o3 + 2 * GDN_HEADS
    ab = jnp.pad(w_in[:, o3:o4], ((0, 0), (0, GDN_AB_PAD - 2 * GDN_HEADS)))
    return jnp.concatenate([w_in[:, :o3], ab, w_in[:, o4:]], axis=1)


def _gdn_project(x, w_pad, conv_w, a_log, dt_bias, batch, seq, tm):
    m = x.shape[0]
    nt = seq // tm
    row = lambda b, i: (b * nt + i, 0)
    fixed = lambda b, i: (0, 0)
    alog, dtb = _gdn_gate_params(a_log, dt_bias)
    wide = jax.ShapeDtypeStruct((m, GDN_QK), F32)
    return pl.pallas_call(
        _gdn_proj_kernel,
        grid=(batch, nt),
        in_specs=[pl.BlockSpec((tm, D_MODEL), row), pl.BlockSpec(w_pad.shape, fixed),
                  pl.BlockSpec(conv_w.shape, fixed), pl.BlockSpec(alog.shape, fixed),
                  pl.BlockSpec(dtb.shape, fixed)],
        out_specs=[pl.BlockSpec((tm, GDN_QK), row), pl.BlockSpec((tm, GDN_QK), row),
                   pl.BlockSpec((tm, GDN_VW), row), pl.BlockSpec((tm, GDN_VW), row),
                   pl.BlockSpec((tm, GDN_AB_PAD), row), pl.BlockSpec((tm, MEM_WIDTH), row),
                   pl.BlockSpec((1, SUBLANES, GDN_CONV_DIM), lambda b, i: (b, 0, 0))],
        out_shape=[wide, wide, wide, wide, jax.ShapeDtypeStruct((m, GDN_AB_PAD), F32),
                   jax.ShapeDtypeStruct((m, MEM_WIDTH), BF16),
                   jax.ShapeDtypeStruct((batch, SUBLANES, GDN_CONV_DIM), F32)],
        scratch_shapes=[pltpu.VMEM((tm + SUBLANES, GDN_CONV_DIM), F32)],
        compiler_params=_cparams(("parallel", "arbitrary")),
        name="gdn_proj",
    )(x, w_pad, conv_w, alog, dtb)


def _gdn_step_proj_kernel(x_ref, w_ref, cs_ref, cw_ref, alog_ref, dtb_ref,
                          pre_ref, q_ref, k_ref, v_ref, z_ref, gb_ref, qm_ref):
    proj = _dot(x_ref[...], w_ref[...])
    qkv = proj[:, :GDN_CONV_DIM]
    pre_ref[...] = qkv
    cw = cw_ref[...]
    conv = cs_ref[0] * cw[0:1]
    conv = conv + cs_ref[1] * cw[1:2]
    conv = conv + cs_ref[2] * cw[2:3]
    conv = conv + qkv * cw[3:4]
    _gdn_conv_post(conv, 0, q_ref, k_ref, v_ref)
    z_ref[...] = proj[:, GDN_Z0:GDN_AB0]
    _gdn_gates(proj[:, GDN_AB0:GDN_QM0], alog_ref, dtb_ref, gb_ref)
    qm_ref[...] = proj[:, GDN_QM0:].astype(qm_ref.dtype)


def _gdn_step_project(x, w_pad, conv_state, conv_w, a_log, dt_bias):
    m = x.shape[0]
    alog, dtb = _gdn_gate_params(a_log, dt_bias)
    cs = jnp.transpose(conv_state, (1, 0, 2))
    wide = jax.ShapeDtypeStruct((m, GDN_QK), F32)
    return pl.pallas_call(
        _gdn_step_proj_kernel,
        out_shape=[jax.ShapeDtypeStruct((m, GDN_CONV_DIM), F32), wide, wide, wide, wide,
                   jax.ShapeDtypeStruct((m, GDN_AB_PAD), F32), jax.ShapeDtypeStruct((m, MEM_WIDTH), F32)],
        compiler_params=pltpu.CompilerParams(vmem_limit_bytes=VMEM_LIMIT_MB * 1024 * 1024),
        name="gdn_step_proj",
    )(x, w_pad, cs, conv_w, alog, dtb)


def _gated_out(o, z, nw):
    on = o * lax.rsqrt(jnp.mean(o * o, axis=-1, keepdims=True) + RMS_EPS) * nw
    return on * jax.nn.silu(z)


GDN_SUPER = 2 * GDN_CHUNK
GDN_INTRA_ROWS = 256


def _gdn_intra_kernel(q_ref, k_ref, v_ref, gb_ref, u_ref, w_ref, qg_ref, kg_ref, qk_ref, egl_ref):
    n = GDN_SUPER
    c = GDN_CHUNK
    shift = c.bit_length() - 1
    r = lax.broadcasted_iota(jnp.int32, (n, n), 0)
    col = lax.broadcasted_iota(jnp.int32, (n, n), 1)
    same = (r >> shift) == (col >> shift)
    causal = same & (r >= col)
    strict = same & (r > col)
    eye = (r == col).astype(F32)
    tri = causal.astype(F32)
    ones = same.astype(F32)
    groups = [slice(n * j, n * (j + 1)) for j in range(q_ref.shape[0] // n)]
    gb = [gb_ref[rows, :] for rows in groups]
    gc_cols = [_dot_f32(tri, x) for x in gb]
    gc_rows = [lax.dot_general(x, tri, (((0,), (1,)), ((), ())), preferred_element_type=F32,
                               precision=lax.Precision.HIGHEST) for x in gb]
    gl_cols = [_dot_f32(ones, x) for x in gb]
    for j, rows in enumerate(groups):
        egl_ref[rows, :] = jnp.exp(gl_cols[j])
    parts = [(j, h) for j in range(len(groups)) for h in range(GDN_HEADS)]
    idx = [(groups[j], slice(GDN_DK * h, GDN_DK * (h + 1))) for j, h in parts]
    ps = range(len(parts))
    q = [q_ref[i] for i in idx]
    k = [k_ref[i] for i in idx]
    gcol = [gc_cols[j][:, h:h + 1] for j, h in parts]
    beta = [gb[j][:, GDN_HEADS + h:GDN_HEADS + h + 1] for j, h in parts]
    decay = [jnp.exp(jnp.where(causal, gcol[p] - gc_rows[j][h:h + 1, :], -jnp.inf)) for p, (j, h) in enumerate(parts)]
    kb = [k[p] * beta[p] for p in ps]
    a = [jnp.where(strict, _dot_nt(kb[p], k[p]) * decay[p], 0.0) for p in ps]
    qk = [jnp.where(causal, _dot_nt(q[p], k[p]) * decay[p], 0.0) for p in ps]
    qk = [jnp.concatenate([m[c * i:c * (i + 1), c * i:c * (i + 1)] for i in range(n // c)], axis=0) for m in qk]
    for j, rows in enumerate(groups):
        qk_ref[rows, :] = jnp.concatenate(qk[GDN_HEADS * j:GDN_HEADS * (j + 1)], axis=1).astype(qk_ref.dtype)
    tinv = [eye - a[p] for p in ps]
    apow = a
    for _ in range(shift - 1):
        apow = [_dot(apow[p], apow[p]) for p in ps]
        tinv = [_dot(tinv[p], eye + apow[p]) for p in ps]
    eg = [jnp.exp(gcol[p]) for p in ps]
    for p in ps:
        u_ref[idx[p]] = _dot(tinv[p], v_ref[idx[p]] * beta[p])
    for p in ps:
        w_ref[idx[p]] = _dot(tinv[p], kb[p] * eg[p]).astype(w_ref.dtype)
    for p, (j, h) in enumerate(parts):
        qg_ref[idx[p]] = (q[p] * eg[p]).astype(qg_ref.dtype)
        kg_ref[idx[p]] = (k[p] * jnp.exp(gl_cols[j][:, h:h + 1] - gcol[p])).astype(kg_ref.dtype)


def _gdn_scan_kernel(u_ref, w_ref, qg_ref, kg_ref, qk_ref, egl_ref, z_ref, nw_ref, o_ref, sfin_ref, s_ref):
    n = pl.program_id(0)
    c = GDN_CHUNK

    @pl.when(n == 0)
    def _():
        s_ref[...] = jnp.zeros_like(s_ref)

    nw = nw_ref[...]

    def per_sequence(b, carry):
        egl = egl_ref[b, 0:1, :]
        heads = range(GDN_HEADS)
        sls = [slice(GDN_DK * h, GDN_DK * (h + 1)) for h in heads]
        state = [s_ref[b, h] for h in heads]
        ws_qs = [_dot(jnp.concatenate([w_ref[b, :, sls[h]], qg_ref[b, :, sls[h]]], axis=0), state[h])
                 for h in heads]
        v_new = [(u_ref[b, :, sls[h]] - ws_qs[h][:c]).astype(BF16) for h in heads]
        pairs = [qk_ref[b, :, LANES * j:LANES * (j + 1)] for j in range(GDN_HEADS // 2)]
        o = [ws_qs[h][c:] + _dot(pairs[h // 2][:, c * (h % 2):c * (h % 2 + 1)], v_new[h]) for h in heads]
        upd = [_dot_tn(kg_ref[b, :, sls[h]], v_new[h]) for h in heads]
        for h in heads:
            s_ref[b, h] = state[h] * egl[:, h:h + 1] + upd[h]
        for h in heads:
            o_ref[b, :, sls[h]] = _gated_out(o[h], z_ref[b, :, sls[h]], nw).astype(o_ref.dtype)
        return carry

    lax.fori_loop(0, u_ref.shape[0], per_sequence, 0, unroll=4)

    @pl.when(n == pl.num_programs(0) - 1)
    def _():
        sfin_ref[...] = s_ref[...]


def _gdn_chunked(q, k, v, z, gb, norm_w, batch, seq):
    m = batch * seq
    row = lambda i: (i, 0)
    wide = pl.BlockSpec((GDN_INTRA_ROWS, GDN_QK), row)
    qk_w = GDN_HEADS * GDN_CHUNK
    u, w, qg, kg, qk, egl = pl.pallas_call(
        _gdn_intra_kernel,
        grid=(m // GDN_INTRA_ROWS,),
        in_specs=[wide, wide, wide, pl.BlockSpec((GDN_INTRA_ROWS, GDN_AB_PAD), row)],
        out_specs=[wide, wide, wide, wide, pl.BlockSpec((GDN_INTRA_ROWS, qk_w), row),
                   pl.BlockSpec((GDN_INTRA_ROWS, LANES), row)],
        out_shape=[jax.ShapeDtypeStruct((m, GDN_VW), F32), jax.ShapeDtypeStruct((m, GDN_QK), BF16),
                   jax.ShapeDtypeStruct((m, GDN_QK), BF16), jax.ShapeDtypeStruct((m, GDN_QK), BF16),
                   jax.ShapeDtypeStruct((m, qk_w), BF16), jax.ShapeDtypeStruct((m, LANES), F32)],
        compiler_params=_cparams(("parallel",)),
        name="gdn_intra",
    )(q, k, v, gb)
    per_seq = lambda a: a.reshape(batch, seq, a.shape[-1])
    chunk = lambda width: pl.BlockSpec((batch, GDN_CHUNK, width), lambda n: (0, n, 0))
    state_spec = pl.BlockSpec((batch, GDN_HEADS, GDN_DK, GDN_DV), lambda n: (0, 0, 0, 0))
    h, s_fin = pl.pallas_call(
        _gdn_scan_kernel,
        grid=(seq // GDN_CHUNK,),
        in_specs=[chunk(GDN_VW), chunk(GDN_QK), chunk(GDN_QK), chunk(GDN_QK), chunk(qk_w), chunk(LANES),
                  chunk(GDN_VW), pl.BlockSpec((1, GDN_DV), lambda n: (0, 0))],
        out_specs=[chunk(GDN_VW), state_spec],
        out_shape=[jax.ShapeDtypeStruct((batch, seq, GDN_VW), BF16),
                   jax.ShapeDtypeStruct((batch, GDN_HEADS, GDN_DK, GDN_DV), F32)],
        scratch_shapes=[pltpu.VMEM((batch, GDN_HEADS, GDN_DK, GDN_DV), F32)],
        compiler_params=_cparams(("arbitrary",)),
        name="gdn_scan",
    )(per_seq(u), per_seq(w), per_seq(qg), per_seq(kg), per_seq(qk), per_seq(egl), per_seq(z),
      norm_w.reshape(1, -1))
    return h.reshape(m, GDN_VW), s_fin


def _gdn_recurrent_kernel(q_ref, k_ref, v_ref, z_ref, gb_ref, nw_ref, s_ref, o_ref, so_ref):
    bs = q_ref.shape[0]
    gb = gb_ref[...]
    nw = nw_ref[...]
    for h in range(GDN_HEADS):
        sl = slice(GDN_DK * h, GDN_DK * (h + 1))
        q_t = q_ref[:, sl].T
        k_t = k_ref[:, sl].T
        v = v_ref[:, sl]
        rows = []
        for b in range(bs):
            state = s_ref[b, h] * jnp.exp(gb[b:b + 1, h:h + 1])
            kcol = k_t[:, b:b + 1]
            v_new = (v[b:b + 1, :] - jnp.sum(kcol * state, axis=0, keepdims=True)) \
                * gb[b:b + 1, GDN_HEADS + h:GDN_HEADS + h + 1]
            state = state + kcol * v_new
            so_ref[b, h] = state
            rows.append(jnp.sum(q_t[:, b:b + 1] * state, axis=0, keepdims=True))
        o_ref[:, sl] = _gated_out(jnp.concatenate(rows, axis=0), z_ref[:, sl], nw).astype(o_ref.dtype)


def _gdn_recurrent(q, k, v, z, gb, norm_w, state, bs):
    batch = q.shape[0]
    row = lambda i: (i, 0)
    wide = pl.BlockSpec((bs, GDN_QK), row)
    st = pl.BlockSpec((bs, GDN_HEADS, GDN_DK, GDN_DV), lambda i: (i, 0, 0, 0))
    return pl.pallas_call(
        _gdn_recurrent_kernel,
        grid=(batch // bs,),
        in_specs=[wide, wide, wide, wide, pl.BlockSpec((bs, GDN_AB_PAD), row),
                  pl.BlockSpec((1, GDN_DV), lambda i: (0, 0)), st],
        out_specs=[wide, st],
        out_shape=[jax.ShapeDtypeStruct((batch, GDN_VW), F32), jax.ShapeDtypeStruct(state.shape, F32)],
        compiler_params=_cparams(("parallel",)),
        name="gdn_recurrent",
    )(q, k, v, z, gb, norm_w.reshape(1, -1), state)


PROMPT_TM = 1024
GDN_TM = 512
FFN_TM = 1024
FFN_TF = 512
MOE_TMG = 1024
MOE_TM = 512
DECODE_BS = 8


def kernel(x_prompt, x_sample, cache_swa_k, cache_swa_v, state_gdn_conv, state_gdn_rec, cache_mem_k, cache_mem_v, mem_prompt, w_in_swa, swa_sinks, w_in_gdn, gdn_conv_w, gdn_a_log, gdn_dt_bias, gdn_norm_w, w_mem_kv, w_out, ln1_g, ln1_b, ln2_g, ln2_b, ffn_w_gu, ffn_w_down, router_w, moe_w_gu, moe_w_down):
    batch, seq, _ = x_prompt.shape
    dec = x_sample.shape[0]
    assert x_sample.shape[1] == 1
    yp = x_prompt.reshape(batch * seq, D_MODEL)
    ys = x_sample.reshape(dec, D_MODEL)
    mem_flat = mem_prompt.reshape(batch * N_MEM, D_MODEL)
    mem3 = lambda a: a.reshape(-1, N_MEM, MEM_WIDTH)

    pair_perm = _swa_pair_perm()
    w_in0 = jnp.concatenate([w_in_swa[0][:, :SWA_QW][:, pair_perm], w_in_swa[0][:, SWA_QW:]], axis=1).astype(BF16)
    w_out0 = w_out[0].astype(BF16)
    w_out0_paired = jnp.concatenate([w_out0[:SWA_QW][pair_perm], w_out0[SWA_QW:]], axis=0)
    w_gu0 = ffn_w_gu[0:1].astype(BF16)
    w_dn0 = ffn_w_down[0:1].astype(BF16)
    mkv = _matmul(mem_flat, w_mem_kv[0].astype(BF16), PROMPT_TM)
    mk0, mv0 = mkv[:, :MEM_WIDTH], mkv[:, MEM_WIDTH:]
    cos_p, sin_p = _rope_tables(jnp.arange(seq, dtype=jnp.int32))
    q, k, v, qm, k_tail, v_tail = _swa_project(yp, w_in0, cos_p, sin_p, PROMPT_TM)
    h = _swa_attention(q, k, v, swa_sinks[0], batch, seq)
    ma = _mem_attention(qm, mem3(mk0), mem3(mv0), batch, seq, PROMPT_TM)
    swa_kp = k_tail.reshape(batch, WINDOW, SWA_KV_HEADS, HEAD_DIM)
    swa_vp = v_tail.reshape(batch, WINDOW, SWA_KV_HEADS, HEAD_DIM)
    yp = _out_project(yp, h, ma, w_out0_paired, ln1_g[0], ln1_b[0], PROMPT_TM)
    yp = _ffn(yp, w_gu0, w_dn0, ln2_g[0], ln2_b[0], FFN_TM, FFN_TF)

    cos_s, sin_s = _rope_tables(jnp.full((dec,), PAST_LEN, jnp.int32))
    q, k, v, qm, _, _ = _swa_project(ys, w_in0, cos_s, sin_s, dec)
    q = q[:, np.argsort(pair_perm)]
    qg = q.reshape(dec, SWA_KV_HEADS, SWA_GROUP, HEAD_DIM).transpose(2, 0, 1, 3).reshape(SWA_GROUP, dec, SWA_KVW)
    sink_g = swa_sinks[0].reshape(SWA_KV_HEADS, SWA_GROUP).T
    mem_kt, mem_vt = _feature_major(cache_mem_k), _feature_major(cache_mem_v)
    og, swa_ks, swa_vs = _decode_attention(
        qg, _feature_major(cache_swa_k), _feature_major(cache_swa_v),
        1.0, HEAD_DIM, DECODE_BS, new_k=k, new_v=v, sinks=sink_g)
    h = og.reshape(SWA_GROUP, dec, SWA_KV_HEADS, HEAD_DIM).transpose(1, 2, 0, 3).reshape(dec, SWA_QW)
    ma, = _decode_attention(qm.reshape(1, dec, MEM_WIDTH), mem_kt, mem_vt,
                            MEM_HEAD_DIM ** -0.5, MEM_HEAD_DIM, DECODE_BS, layer=0)
    ys = _out_project(ys, h, ma[0], w_out0, ln1_g[0], ln1_b[0], dec)
    ys = _ffn(ys, w_gu0, w_dn0, ln2_g[0], ln2_b[0], dec, FFN_TF)

    w_in1 = _gdn_pad_w_in(w_in_gdn[0]).astype(BF16)
    w_out1 = w_out[1].astype(BF16)
    w_gu1 = moe_w_gu[0]
    w_dn1 = moe_w_down[0]
    mkv = _matmul(mem_flat, w_mem_kv[1].astype(BF16), PROMPT_TM)
    mk1, mv1 = mkv[:, :MEM_WIDTH], mkv[:, MEM_WIDTH:]
    q, k, v, z, gb, qm, conv_tail = _gdn_project(yp, w_in1, gdn_conv_w[0], gdn_a_log[0], gdn_dt_bias[0],
                                                 batch, seq, GDN_TM)
    h, rec_p = _gdn_chunked(q, k, v, z, gb, gdn_norm_w[0], batch, seq)
    ma = _mem_attention(qm, mem3(mk1), mem3(mv1), batch, seq, PROMPT_TM)
    yp, gtop, meta, cnt = _out_project(yp, h, ma, w_out1, ln1_g[1], ln1_b[1], PROMPT_TM, router_w=router_w[0])
    yp = _moe(yp, gtop, meta, cnt, w_gu1, w_dn1, ln2_g[1], ln2_b[1], MOE_TMG, FFN_TF, MOE_TM)

    pre, q, k, v, z, gb, qm = _gdn_step_project(ys, w_in1, state_gdn_conv[0], gdn_conv_w[0],
                                                gdn_a_log[0], gdn_dt_bias[0])
    h, rec_s = _gdn_recurrent(q, k, v, z, gb, gdn_norm_w[0], state_gdn_rec[0], DECODE_BS)
    ma, = _decode_attention(qm.reshape(1, dec, MEM_WIDTH), mem_kt, mem_vt,
                            MEM_HEAD_DIM ** -0.5, MEM_HEAD_DIM, DECODE_BS, layer=1)
    ys = _out_project(ys, h, ma[0], w_out1, ln1_g[1], ln1_b[1], dec)
    ys = _ffn(ys, w_gu1, w_dn1, ln2_g[1], ln2_b[1], dec, FFN_TF, router_w=router_w[0])

    mem_shape = (batch, N_MEM, MEM_HEADS, MEM_HEAD_DIM)
    conv_s = jnp.concatenate([state_gdn_conv[0][:, 1:], pre[:, None, :]], axis=1)
    return (yp.reshape(batch, seq, D_MODEL), ys.reshape(dec, 1, D_MODEL),
            swa_kp[None], swa_vp[None],
            conv_tail[None, :, SUBLANES - (GDN_CONV_W - 1):, :], rec_p[None],
            jnp.stack([mk0.reshape(mem_shape), mk1.reshape(mem_shape)]),
            jnp.stack([mv0.reshape(mem_shape), mv1.reshape(mem_shape)]),
            _key_major(swa_ks, SWA_KV_HEADS)[None], _key_major(swa_vs, SWA_KV_HEADS)[None],
            conv_s[None], rec_s[None])
```

```python
import functools

import jax
import jax.numpy as jnp
import numpy as np
from jax import lax
from jax.experimental import pallas as pl
from jax.experimental.pallas import tpu as pltpu

F32 = jnp.float32
BF16 = jnp.bfloat16

D_MODEL = 1024
DEPTH = 2
PAST_LEN = 16384
SWA_HEADS = 12
SWA_KV_HEADS = 4
SWA_GROUP = SWA_HEADS // SWA_KV_HEADS
HEAD_DIM = 64
WINDOW = 128
SWA_BLOCK = 128
ROPE_THETA = 10000.0
SWA_QW = SWA_HEADS * HEAD_DIM
SWA_KVW = SWA_KV_HEADS * HEAD_DIM
GDN_HEADS = 6
GDN_DK = 128
GDN_DV = 128
GDN_CONV_W = 4
GDN_CHUNK = 64
GDN_QK = GDN_HEADS * GDN_DK
GDN_VW = GDN_HEADS * GDN_DV
GDN_CONV_DIM = 2 * GDN_QK + GDN_VW
N_MEM = 256
MEM_HEADS = 4
MEM_HEAD_DIM = 64
MEM_WIDTH = MEM_HEADS * MEM_HEAD_DIM
D_FF = 3584
N_EXPERTS = 8
DEEPNORM_ALPHA = (2 * DEPTH) ** 0.25
LN_EPS = 1e-5
RMS_EPS = 1e-6
L2_EPS = 1e-6

LANES = 128
SUBLANES = 8
BF16_ROWS = 16
GDN_AB_PAD = LANES
VMEM_LIMIT_MB = 56


def _cparams(sem, vmem_mb=VMEM_LIMIT_MB):
    return pltpu.CompilerParams(dimension_semantics=sem, vmem_limit_bytes=vmem_mb * 1024 * 1024)


def _bf16_round(x):
    return x.astype(BF16).astype(F32)


def _dot(a, b):
    return jnp.dot(a.astype(BF16), b.astype(BF16), preferred_element_type=F32)


def _dot_nt(a, b):
    return lax.dot_general(a.astype(BF16), b.astype(BF16), (((1,), (1,)), ((), ())),
                           preferred_element_type=F32)


def _dot_tn(a, b):
    return lax.dot_general(a.astype(BF16), b.astype(BF16), (((0,), (0,)), ((), ())),
                           preferred_element_type=F32)


def _dot_f32(a, b):
    return jnp.dot(a, b, preferred_element_type=F32, precision=lax.Precision.HIGHEST)


def _layer_norm(t, g, b):
    mu = jnp.mean(t, axis=-1, keepdims=True)
    d = t - mu
    var = jnp.mean(d * d, axis=-1, keepdims=True)
    return d * lax.rsqrt(var + LN_EPS) * g + b


def _mm_kernel(x_ref, w_ref, o_ref):
    o_ref[...] = _dot(x_ref[...], w_ref[...])


def _matmul(x, w, tm):
    m, k = x.shape
    n = w.shape[1]
    return pl.pallas_call(
        _mm_kernel,
        grid=(m // tm,),
        in_specs=[pl.BlockSpec((tm, k), lambda i: (i, 0)),
                  pl.BlockSpec((k, n), lambda i: (0, 0))],
        out_specs=pl.BlockSpec((tm, n), lambda i: (i, 0)),
        out_shape=jax.ShapeDtypeStruct((m, n), F32),
        compiler_params=_cparams(("parallel",)),
        name="matmul",
    )(x, w)


def _swa_proj_kernel(x_ref, w_ref, cos_ref, sin_ref, q_ref, k_ref, v_ref, qm_ref, kt_ref, vt_ref):
    proj = _dot(x_ref[...], w_ref[...])
    cos = cos_ref[...]
    sin = sin_ref[...]
    lane = lax.broadcasted_iota(jnp.int32, cos.shape, 1)
    first_half = (lane & (HEAD_DIM - 1)) < HEAD_DIM // 2

    def rope(xb):
        partner = jnp.where(first_half, pltpu.roll(xb, LANES - HEAD_DIM // 2, 1),
                            pltpu.roll(xb, HEAD_DIM // 2, 1))
        return xb * cos + partner * sin

    for j in range(SWA_QW // LANES):
        sl = slice(LANES * j, LANES * (j + 1))
        q_ref[:, sl] = (rope(proj[:, sl]) * HEAD_DIM ** -0.5).astype(q_ref.dtype)
    for j in range(SWA_KVW // LANES):
        k_ref[:, LANES * j:LANES * (j + 1)] = rope(proj[:, SWA_QW + LANES * j:SWA_QW + LANES * (j + 1)])
    v_ref[...] = proj[:, SWA_QW + SWA_KVW:SWA_QW + 2 * SWA_KVW]
    qm_ref[...] = proj[:, SWA_QW + 2 * SWA_KVW:].astype(qm_ref.dtype)
    tm = x_ref.shape[0]
    kt_ref[...] = k_ref[tm - WINDOW:, :]
    vt_ref[...] = v_ref[tm - WINDOW:, :]


def _rope_tables(pos):
    half = HEAD_DIM // 2
    inv = ROPE_THETA ** (-jnp.arange(half, dtype=F32) / half)
    ang = pos.astype(F32)[:, None] * inv[None, :]
    cos = jnp.cos(ang)
    sin = jnp.sin(ang)
    reps = LANES // HEAD_DIM
    return jnp.tile(cos, (1, 2 * reps)), jnp.tile(jnp.concatenate([-sin, sin], axis=1), (1, reps))


def _swa_project(x, w, cos, sin, tm):
    m = x.shape[0]
    n_in = w.shape[1]
    tab_blocks = cos.shape[0] // tm
    n_seq = m // (tm * tab_blocks)
    assert tm >= WINDOW
    row = lambda i: (i, 0)
    tail = lambda i: (i // tab_blocks, 0)
    return pl.pallas_call(
        _swa_proj_kernel,
        grid=(m // tm,),
        in_specs=[pl.BlockSpec((tm, D_MODEL), row),
                  pl.BlockSpec((D_MODEL, n_in), lambda i: (0, 0)),
                  pl.BlockSpec((tm, LANES), lambda i: (i % tab_blocks, 0)),
                  pl.BlockSpec((tm, LANES), lambda i: (i % tab_blocks, 0))],
        out_specs=[pl.BlockSpec((tm, SWA_QW), row), pl.BlockSpec((tm, SWA_KVW), row),
                   pl.BlockSpec((tm, SWA_KVW), row), pl.BlockSpec((tm, MEM_WIDTH), row),
                   pl.BlockSpec((WINDOW, SWA_KVW), tail), pl.BlockSpec((WINDOW, SWA_KVW), tail)],
        out_shape=[jax.ShapeDtypeStruct((m, SWA_QW), BF16),
                   jax.ShapeDtypeStruct((m, SWA_KVW), F32),
                   jax.ShapeDtypeStruct((m, SWA_KVW), F32),
                   jax.ShapeDtypeStruct((m, MEM_WIDTH), BF16),
                   jax.ShapeDtypeStruct((n_seq * WINDOW, SWA_KVW), F32),
                   jax.ShapeDtypeStruct((n_seq * WINDOW, SWA_KVW), F32)],
        compiler_params=_cparams(("arbitrary",)),
        name="swa_proj",
    )(x, w, cos, sin)


def _swa_pair_heads():
    return [(SWA_GROUP * (2 * p) + g, SWA_GROUP * (2 * p + 1) + g)
            for p in range(SWA_KV_HEADS // 2) for g in range(SWA_GROUP)]


def _swa_pair_perm():
    cols = []
    for a, b in _swa_pair_heads():
        cols += list(range(HEAD_DIM * a, HEAD_DIM * (a + 1))) + list(range(HEAD_DIM * b, HEAD_DIM * (b + 1)))
    return np.asarray(cols, np.int32)


SWA_QBLOCKS = 8


def _swa_attn_kernel(sink_ref, q_ref, kp_ref, kc_ref, vp_ref, vc_ref, o_ref):
    i = pl.program_id(1)
    blk = SWA_BLOCK
    nq = q_ref.shape[0] // blk
    kwin = jnp.concatenate([kp_ref[...], kc_ref[...]], axis=0).astype(BF16)
    vwin = jnp.concatenate([vp_ref[...], vc_ref[...]], axis=0).astype(BF16)
    qi = lax.broadcasted_iota(jnp.int32, (blk, 2 * blk), 0)
    kj = lax.broadcasted_iota(jnp.int32, (blk, 2 * blk), 1)
    band = (kj >= qi) & (kj <= qi + WINDOW)
    lo = lax.broadcasted_iota(jnp.int32, (blk, LANES), 1) < HEAD_DIM
    pair_heads = _swa_pair_heads()
    zero = jnp.zeros((blk, LANES), q_ref.dtype)
    ones = jnp.ones((2 * blk, LANES), BF16)
    parts = [(j, p) for j in range(nq) for p in range(SWA_KV_HEADS // 2)]
    ps = range(len(parts))
    qs, sink, mask, kslab, vslab = [], [], [], [], []
    for j, p in parts:
        ms = range(SWA_GROUP * p, SWA_GROUP * (p + 1))
        blocks = [q_ref[blk * j:blk * (j + 1), LANES * m:LANES * (m + 1)] for m in ms]
        qs.append(jnp.concatenate([jnp.where(lo, x, zero) for x in blocks]
                                  + [jnp.where(lo, zero, x) for x in blocks], axis=0))
        heads = [pair_heads[m][0] for m in ms] + [pair_heads[m][1] for m in ms]
        sink.append(jnp.concatenate([jnp.full((blk, 1), sink_ref[h], F32) for h in heads], axis=0))
        valid = band & ((kj >= blk) | (i > 0)) if j == 0 else band
        mask.append(jnp.concatenate([valid] * (2 * SWA_GROUP), axis=0))
        kslab.append(kwin[blk * j:blk * (j + 2), LANES * p:LANES * (p + 1)])
        vslab.append(jnp.concatenate([vwin[blk * j:blk * (j + 2), LANES * p:LANES * (p + 1)], ones], axis=1))
    s = [jnp.where(mask[t], _dot_nt(qs[t], kslab[t]), -jnp.inf) for t in ps]
    m = [jnp.maximum(jnp.max(s[t], axis=-1, keepdims=True), sink[t]) for t in ps]
    e = [jnp.exp(s[t] - m[t]) for t in ps]
    ov = [_dot(e[t], vslab[t]) for t in ps]
    o = [ov[t][:, :LANES] * (1.0 / (ov[t][:, LANES:] + jnp.exp(sink[t] - m[t]))) for t in ps]
    for j in range(nq):
        outs = [jnp.where(lo, o[t][blk * g:blk * (g + 1)], o[t][blk * (SWA_GROUP + g):blk * (SWA_GROUP + g + 1)])
                for t in ps if parts[t][0] == j for g in range(SWA_GROUP)]
        o_ref[blk * j:blk * (j + 1), :] = jnp.concatenate(outs, axis=1).astype(o_ref.dtype)


def _swa_attention(q, k, v, sinks, batch, seq):
    step = SWA_QBLOCKS * SWA_BLOCK
    ns = seq // step
    cur = lambda b, i: (b * ns + i, 0)
    prev = lambda b, i: (b * ns * SWA_QBLOCKS + jnp.maximum(i * SWA_QBLOCKS - 1, 0), 0)
    return pl.pallas_call(
        _swa_attn_kernel,
        grid=(batch, ns),
        in_specs=[pl.BlockSpec(memory_space=pltpu.SMEM),
                  pl.BlockSpec((step, SWA_QW), cur),
                  pl.BlockSpec((SWA_BLOCK, SWA_KVW), prev),
                  pl.BlockSpec((step, SWA_KVW), cur),
                  pl.BlockSpec((SWA_BLOCK, SWA_KVW), prev),
                  pl.BlockSpec((step, SWA_KVW), cur)],
        out_specs=pl.BlockSpec((step, SWA_QW), cur),
        out_shape=jax.ShapeDtypeStruct((batch * seq, SWA_QW), BF16),
        compiler_params=_cparams(("parallel", "parallel")),
        name="swa_attn",
    )(sinks, q, k, k, v, v)


def _mem_attn_kernel(q_ref, k_ref, v_ref, o_ref):
    q = q_ref[...]
    k = k_ref[0].astype(BF16)
    v = v_ref[0].astype(BF16)
    tq = q.shape[0]
    lo = lax.broadcasted_iota(jnp.int32, (tq, LANES), 1) < MEM_HEAD_DIM
    zero = jnp.zeros((tq, LANES), q.dtype)
    slabs = range(MEM_WIDTH // LANES)
    sls = [slice(LANES * j, LANES * (j + 1)) for j in slabs]
    qs = [jnp.concatenate([jnp.where(lo, q[:, sl], zero), jnp.where(lo, zero, q[:, sl])], axis=0) for sl in sls]
    s = [_dot_nt(qs[j], k[:, sls[j]]) * MEM_HEAD_DIM ** -0.5 for j in slabs]
    e = [jnp.exp(s[j] - jnp.max(s[j], axis=-1, keepdims=True)) for j in slabs]
    ones = jnp.ones((k.shape[0], LANES), BF16)
    ov = [_dot(e[j], jnp.concatenate([v[:, sls[j]], ones], axis=1)) for j in slabs]
    o = [ov[j][:, :LANES] * (1.0 / ov[j][:, LANES:]) for j in slabs]
    o_ref[...] = jnp.concatenate([jnp.where(lo, o[j][:tq], o[j][tq:]) for j in slabs], axis=1).astype(o_ref.dtype)


def _mem_attention(qm, mem_k, mem_v, batch, seq, tq):
    nq = seq // tq
    return pl.pallas_call(
        _mem_attn_kernel,
        grid=(batch, nq),
        in_specs=[pl.BlockSpec((tq, MEM_WIDTH), lambda b, i: (b * nq + i, 0)),
                  pl.BlockSpec((1, N_MEM, MEM_WIDTH), lambda b, i: (b, 0, 0)),
                  pl.BlockSpec((1, N_MEM, MEM_WIDTH), lambda b, i: (b, 0, 0))],
        out_specs=pl.BlockSpec((tq, MEM_WIDTH), lambda b, i: (b * nq + i, 0)),
        out_shape=jax.ShapeDtypeStruct((batch * seq, MEM_WIDTH), BF16),
        compiler_params=_cparams(("parallel", "parallel")),
        name="mem_attn",
    )(qm, mem_k, mem_v)


def _decode_attn_kernel(*refs, scale, head_dim, with_self):
    if with_self:
        q_ref, kt_ref, vt_ref, kn_ref, vn_ref, knc_ref, vnc_ref, sink_ref, o_ref, ko_ref, vo_ref = refs
    else:
        q_ref, kt_ref, vt_ref, o_ref = refs
    bs, width, n_keys = kt_ref.shape
    n_groups = q_ref.shape[0]
    hp = SUBLANES
    shift = head_dim.bit_length() - 1
    own = lax.broadcasted_iota(jnp.int32, (hp, width), 0) == (lax.broadcasted_iota(jnp.int32, (hp, width), 1) >> shift)
    samples = range(bs)
    kt_r = [kt_ref[b].astype(BF16) for b in samples]
    vt_r = [vt_ref[b].astype(BF16) for b in samples]
    qh = [jnp.concatenate([jnp.where(own, _bf16_round(q_ref[g, b:b + 1, :].astype(F32) * scale), 0.0)
                           for g in range(n_groups)], axis=0) for b in samples]
    s = [_dot(qh[b], kt_r[b]) for b in samples]
    m = [jnp.max(s[b], axis=-1, keepdims=True) for b in samples]
    if with_self:
        kn = kn_ref[...]
        vn = vn_ref[...]
        sink = sink_ref[...]
        s_self = [jnp.sum(qh[b] * _bf16_round(kn[b:b + 1, :]), axis=-1, keepdims=True) for b in samples]
        m = [jnp.maximum(jnp.maximum(m[b], s_self[b]), sink) for b in samples]
    e = [jnp.exp(s[b] - m[b]) for b in samples]
    den = [jnp.sum(e[b], axis=-1, keepdims=True) for b in samples]
    if with_self:
        e_self = [jnp.exp(s_self[b] - m[b]) for b in samples]
        den = [den[b] + e_self[b] + jnp.exp(sink - m[b]) for b in samples]
    inv = [1.0 / den[b] for b in samples]
    ov = [_dot_nt(e[b] * inv[b], vt_r[b]) for b in samples]
    if with_self:
        ov = [ov[b] + _bf16_round(e_self[b] * inv[b]) * _bf16_round(vn[b:b + 1, :]) for b in samples]
    for b in samples:
        for g in range(n_groups):
            o_ref[g, b:b + 1, :] = jnp.sum(jnp.where(own, ov[b][hp * g:hp * (g + 1)], 0.0), axis=0, keepdims=True)
    if with_self:
        last_key = lax.broadcasted_iota(jnp.int32, (width, n_keys), 1) == n_keys - 1
        knc = knc_ref[0]
        vnc = vnc_ref[0]
        for b in samples:
            ko_ref[b] = jnp.where(last_key, knc[:, b:b + 1], pltpu.roll(kt_ref[b], n_keys - 1, 1))
            vo_ref[b] = jnp.where(last_key, vnc[:, b:b + 1], pltpu.roll(vt_ref[b], n_keys - 1, 1))


def _sample_columns(x, bs):
    batch, width = x.shape
    return jnp.swapaxes(x.reshape(batch // bs, bs, width), 1, 2)


def _decode_attention(q_groups, cache_kt, cache_vt, scale, head_dim, bs, layer=0, new_k=None, new_v=None, sinks=None):
    n_groups, batch, width = q_groups.shape
    n_keys = cache_kt.shape[2]
    n_heads = width // head_dim
    assert n_heads <= SUBLANES
    nblk = batch // bs
    first = layer * nblk
    with_self = new_k is not None
    cache3 = pl.BlockSpec((bs, width, n_keys), lambda i: (first + i, 0, 0))
    qspec = pl.BlockSpec((n_groups, bs, width), lambda i: (0, i, 0))
    in_specs = [qspec, cache3, cache3]
    args = [q_groups, cache_kt, cache_vt]
    out_specs = [qspec]
    out_shape = [jax.ShapeDtypeStruct((n_groups, batch, width), F32)]
    if with_self:
        row = pl.BlockSpec((bs, width), lambda i: (i, 0))
        col = pl.BlockSpec((1, width, bs), lambda i: (i, 0, 0))
        sink_col = jnp.pad(sinks, ((0, 0), (0, SUBLANES - n_heads))).reshape(n_groups * SUBLANES, 1)
        in_specs += [row, row, col, col, pl.BlockSpec(sink_col.shape, lambda i: (0, 0))]
        args += [new_k, new_v, _sample_columns(new_k, bs), _sample_columns(new_v, bs), sink_col]
        blk3 = pl.BlockSpec((bs, width, n_keys), lambda i: (i, 0, 0))
        out_specs += [blk3, blk3]
        out_shape += [jax.ShapeDtypeStruct((batch, width, n_keys), F32)] * 2
    return pl.pallas_call(
        functools.partial(_decode_attn_kernel, scale=scale, head_dim=head_dim, with_self=with_self),
        grid=(nblk,),
        in_specs=in_specs, out_specs=out_specs, out_shape=out_shape,
        compiler_params=_cparams(("parallel",)),
        name="decode_attn",
    )(*args)


def _feature_major(cache):
    layers, batch, n_keys, heads, hd = cache.shape
    return jnp.transpose(cache, (0, 1, 3, 4, 2)).reshape(layers * batch, heads * hd, n_keys)


def _key_major(cache_t, heads):
    batch, width, n_keys = cache_t.shape
    return jnp.transpose(cache_t.reshape(batch, heads, width // heads, n_keys), (0, 3, 1, 2))


def _outproj_kernel(x_ref, h_ref, m_ref, w1_ref, w2_ref, g_ref, b_ref, *rest, route):
    t = _dot(h_ref[...], w1_ref[...]) + _dot(m_ref[...], w2_ref[...])
    y = _layer_norm(DEEPNORM_ALPHA * x_ref[...] + t, g_ref[...], b_ref[...])
    if route:
        rw_ref, o_ref, gtop_ref, meta_ref, cnt_ref, carry_ref = rest
        _route_tile(y, rw_ref, gtop_ref, meta_ref, cnt_ref, carry_ref)
    else:
        o_ref, = rest
    o_ref[...] = y


def _out_project(x, h, mem, w_out, g, b, tm, router_w=None):
    m = x.shape[0]
    hw = h.shape[1]
    w1, w2 = w_out[:hw], w_out[hw:]
    row = lambda i: (i, 0)
    fixed = lambda i: (0, 0)
    route = router_w is not None
    in_specs = [pl.BlockSpec((tm, D_MODEL), row), pl.BlockSpec((tm, hw), row),
                pl.BlockSpec((tm, MEM_WIDTH), row), pl.BlockSpec(w1.shape, fixed),
                pl.BlockSpec(w2.shape, fixed), pl.BlockSpec((1, D_MODEL), fixed),
                pl.BlockSpec((1, D_MODEL), fixed)]
    args = [x, h, mem, w1, w2, g.reshape(1, -1), b.reshape(1, -1)]
    out_specs = [pl.BlockSpec((tm, D_MODEL), row)]
    out_shape = [jax.ShapeDtypeStruct((m, D_MODEL), F32)]
    scratch = []
    if route:
        rw_t = jnp.pad(router_w.T, ((0, BF16_ROWS - N_EXPERTS), (0, 0))).astype(BF16)
        in_specs.append(pl.BlockSpec(rw_t.shape, fixed))
        args.append(rw_t)
        out_specs += [pl.BlockSpec((tm, LANES), row), pl.BlockSpec((SUBLANES, tm), lambda i: (0, i)),
                      pl.BlockSpec((SUBLANES, LANES), fixed)]
        out_shape += [jax.ShapeDtypeStruct((m, LANES), F32), jax.ShapeDtypeStruct((SUBLANES, m), jnp.int32),
                      jax.ShapeDtypeStruct((SUBLANES, LANES), jnp.int32)]
        scratch = [pltpu.VMEM((N_EXPERTS, LANES), F32)]
    outs = pl.pallas_call(
        functools.partial(_outproj_kernel, route=route),
        grid=(m // tm,),
        in_specs=in_specs, out_specs=out_specs, out_shape=out_shape, scratch_shapes=scratch,
        compiler_params=_cparams(("arbitrary",) if route else ("parallel",)),
        name="out_proj_route" if route else "out_proj",
    )(*args)
    return outs if route else outs[0]


def _top2(logits):
    lane = lax.broadcasted_iota(jnp.int32, logits.shape, 1)
    valid = lane < N_EXPERTS
    lg = jnp.where(valid, logits, -jnp.inf)
    ex = jnp.exp(lg - jnp.max(lg, axis=-1, keepdims=True))
    probs = ex / jnp.sum(ex, axis=-1, keepdims=True)
    cand = jnp.where(valid, probs, -1.0)
    p1 = jnp.max(cand, axis=-1, keepdims=True)
    i1 = jnp.min(jnp.where(cand == p1, lane, LANES), axis=-1, keepdims=True)
    cand = jnp.where(lane == i1, -1.0, cand)
    p2 = jnp.max(cand, axis=-1, keepdims=True)
    i2 = jnp.min(jnp.where(cand == p2, lane, LANES), axis=-1, keepdims=True)
    tot = p1 + p2
    return p1 / tot, i1, p2 / tot, i2


def _router_gates(xb, rw):
    g1, i1, g2, i2 = _top2(_dot(xb, rw))
    lane = lax.broadcasted_iota(jnp.int32, (xb.shape[0], LANES), 1)
    return jnp.where(lane == i1, g1, 0.0) + jnp.where(lane == i2, g2, 0.0)


def _pad_router(router_w):
    return jnp.pad(router_w, ((0, 0), (0, LANES - N_EXPERTS))).astype(BF16)


def _ffn_kernel(*refs, moe):
    if moe:
        x_ref, wg_ref, wu_ref, wd_ref, rw_ref, g_ref, b_ref, o_ref, xb_ref, acc_ref, gate_ref = refs
    else:
        x_ref, wg_ref, wu_ref, wd_ref, g_ref, b_ref, o_ref, xb_ref, acc_ref = refs
    e = pl.program_id(1)
    f = pl.program_id(2)

    @pl.when((e == 0) & (f == 0))
    def _():
        xb_ref[...] = x_ref[...].astype(xb_ref.dtype)
        acc_ref[...] = jnp.zeros_like(acc_ref)
        if moe:
            gate_ref[...] = _router_gates(xb_ref[...], rw_ref[...])

    xb = xb_ref[...]
    gt = _dot(xb, wg_ref[0])
    up = _dot(xb, wu_ref[0])
    y = _dot(jax.nn.silu(gt) * up, wd_ref[0])
    if moe:
        lane = lax.broadcasted_iota(jnp.int32, gate_ref.shape, 1)
        y = y * jnp.sum(jnp.where(lane == e, gate_ref[...], 0.0), axis=-1, keepdims=True)
    acc_ref[...] += y

    @pl.when((e == pl.num_programs(1) - 1) & (f == pl.num_programs(2) - 1))
    def _():
        o_ref[...] = _layer_norm(DEEPNORM_ALPHA * x_ref[...] + acc_ref[...], g_ref[...], b_ref[...])


def _ffn(x, w_gu, w_down, g, b, tm, tf, router_w=None):
    m = x.shape[0]
    n_exp, _, two_f = w_gu.shape
    nf = two_f // 2 // tf
    moe = router_w is not None
    row = lambda i, e, f: (i, 0)
    fixed = lambda i, e, f: (0, 0)
    in_specs = [pl.BlockSpec((tm, D_MODEL), row),
                pl.BlockSpec((1, D_MODEL, tf), lambda i, e, f: (e, 0, f)),
                pl.BlockSpec((1, D_MODEL, tf), lambda i, e, f: (e, 0, nf + f)),
                pl.BlockSpec((1, tf, D_MODEL), lambda i, e, f: (e, f, 0))]
    args = [x, w_gu, w_gu, w_down]
    scratch = [pltpu.VMEM((tm, D_MODEL), BF16), pltpu.VMEM((tm, D_MODEL), F32)]
    if moe:
        in_specs.append(pl.BlockSpec((D_MODEL, LANES), fixed))
        args.append(_pad_router(router_w))
        scratch.append(pltpu.VMEM((tm, LANES), F32))
    in_specs += [pl.BlockSpec((1, D_MODEL), fixed), pl.BlockSpec((1, D_MODEL), fixed)]
    args += [g.reshape(1, -1), b.reshape(1, -1)]
    return pl.pallas_call(
        functools.partial(_ffn_kernel, moe=moe),
        grid=(m // tm, n_exp, nf),
        in_specs=in_specs,
        out_specs=pl.BlockSpec((tm, D_MODEL), row),
        out_shape=jax.ShapeDtypeStruct((m, D_MODEL), F32),
        scratch_shapes=scratch,
        compiler_params=_cparams(("parallel", "arbitrary", "arbitrary")),
        name="moe_ffn" if moe else "ffn",
    )(*args)


META_E1, META_E2, META_R1, META_R2 = 0, 1, 2, 3
ZERO_ROWS = 256


def _route_tile(y, rw_ref, gtop_ref, meta_ref, cnt_ref, carry_ref):
    i = pl.program_id(0)
    tm = y.shape[0]

    @pl.when(i == 0)
    def _():
        carry_ref[...] = jnp.zeros_like(carry_ref)

    logits = _dot_nt(rw_ref[...], y)[:N_EXPERTS]
    row = lax.broadcasted_iota(jnp.int32, logits.shape, 0)
    ex = jnp.exp(logits - jnp.max(logits, axis=0, keepdims=True))
    probs = ex / jnp.sum(ex, axis=0, keepdims=True)
    p1 = jnp.max(probs, axis=0, keepdims=True)
    i1 = jnp.min(jnp.where(probs == p1, row, N_EXPERTS), axis=0, keepdims=True)
    cand = jnp.where(row == i1, -1.0, probs)
    p2 = jnp.max(cand, axis=0, keepdims=True)
    i2 = jnp.min(jnp.where(cand == p2, row, N_EXPERTS), axis=0, keepdims=True)
    tot = p1 + p2
    onehot = ((row == i1) | (row == i2)).astype(F32)
    r = lax.broadcasted_iota(jnp.int32, (tm, tm), 0)
    c = lax.broadcasted_iota(jnp.int32, (tm, tm), 1)
    before = _dot(onehot, (r < c).astype(F32)) + carry_ref[:, 0:1]
    rank1 = jnp.sum(jnp.where(row == i1, before, 0.0), axis=0, keepdims=True)
    rank2 = jnp.sum(jnp.where(row == i2, before, 0.0), axis=0, keepdims=True)
    meta_ref[...] = jnp.where(row == META_E1, i1.astype(F32), jnp.where(row == META_E2, i2.astype(F32), jnp.where(
        row == META_R1, rank1, jnp.where(row == META_R2, rank2, 0.0)))).astype(jnp.int32)
    gates = jnp.where(row == 0, p1 / tot, jnp.where(row == 1, p2 / tot, 0.0))
    gtop_ref[...] = jnp.concatenate([gates, jnp.zeros((LANES - N_EXPERTS, tm), F32)], axis=0).T
    carry_ref[...] += jnp.sum(onehot, axis=1, keepdims=True)
    cnt_ref[...] = carry_ref[...].astype(jnp.int32)


def _dispatch_plan(meta, cnt, tmg, n_tiles, tm):
    counts = cnt[:N_EXPERTS, 0]
    padded = (counts + tmg - 1) // tmg * tmg
    gend = jnp.cumsum(padded)
    gstart = gend - padded
    pos1 = gstart[meta[META_E1]] + meta[META_R1]
    pos2 = gstart[meta[META_E2]] + meta[META_R2]
    pos = jnp.concatenate([pos1.reshape(-1, 1, tm), pos2.reshape(-1, 1, tm)], axis=2).astype(jnp.int32)
    n_used = (gend[-1] // tmg).astype(jnp.int32).reshape(1)
    tile_start = jnp.arange(n_tiles, dtype=jnp.int32) * tmg
    tile_expert = jnp.minimum(jnp.sum(tile_start[:, None] >= gend[None, :], axis=1), N_EXPERTS - 1)
    tail = jnp.stack([gend[-1], (n_tiles * tmg - gend[-1]) // ZERO_ROWS])
    pads = jnp.concatenate([jnp.stack([gstart + counts, padded - counts]), tail[:, None]], axis=1).astype(jnp.int32)
    return pos, tile_expert.astype(jnp.int32), n_used, pads


def _row_copy(src_ref, src_row, dst_ref, dst_row, sem):
    return pltpu.make_async_copy(src_ref.at[pl.ds(src_row, 1)], dst_ref.at[pl.ds(dst_row, 1)], sem)


def _dispatch_kernel(pads_ref, pos_ref, x_ref, xs_ref, zero_ref, sem):
    i = pl.program_id(0)
    tm = x_ref.shape[0]

    def scatter(r, k):
        return _row_copy(x_ref, r, xs_ref, pos_ref[0, 0, k * tm + r], sem)

    def start(r, carry):
        scatter(r, 0).start()
        scatter(r, 1).start()
        return carry

    def wait(r, carry):
        scatter(r, 0).wait()
        scatter(r, 1).wait()
        return carry

    for r in range(tm):
        start(r, 0)
    lax.fori_loop(0, tm, wait, 0, unroll=8)

    @pl.when(i == pl.num_programs(0) - 1)
    def _():
        zero_ref[...] = jnp.zeros_like(zero_ref)
        for e in range(N_EXPERTS):
            first = pads_ref[0, e]
            n_pad = pads_ref[1, e]
            fill = lambda r: _row_copy(zero_ref, 0, xs_ref, first + r, sem)
            lax.fori_loop(0, n_pad, lambda r, c: (fill(r).start(), c)[1], 0)
            lax.fori_loop(0, n_pad, lambda r, c: (fill(r).wait(), c)[1], 0)
        tail_first = pads_ref[0, N_EXPERTS]
        n_blocks = pads_ref[1, N_EXPERTS]
        fill_tail = lambda r: pltpu.make_async_copy(
            zero_ref, xs_ref.at[pl.ds(pl.multiple_of(tail_first + r * ZERO_ROWS, ZERO_ROWS), ZERO_ROWS)], sem)
        lax.fori_loop(0, n_blocks, lambda r, c: (fill_tail(r).start(), c)[1], 0)
        lax.fori_loop(0, n_blocks, lambda r, c: (fill_tail(r).wait(), c)[1], 0)


def _dispatch(x, pos, pads, n_slots, tm):
    m = x.shape[0]
    return pl.pallas_call(
        _dispatch_kernel,
        grid_spec=pltpu.PrefetchScalarGridSpec(
            num_scalar_prefetch=1,
            grid=(m // tm,),
            in_specs=[pl.BlockSpec((1, 1, 2 * tm), lambda i, pads: (i, 0, 0), memory_space=pltpu.SMEM),
                      pl.BlockSpec((tm, D_MODEL), lambda i, pads: (i, 0))],
            out_specs=pl.BlockSpec(memory_space=pl.ANY),
            scratch_shapes=[pltpu.VMEM((ZERO_ROWS, D_MODEL), F32), pltpu.SemaphoreType.DMA(())]),
        out_shape=jax.ShapeDtypeStruct((n_slots, D_MODEL), F32),
        compiler_params=_cparams(("arbitrary",)),
        name="moe_dispatch",
    )(pads, pos, x)


def _grouped_ffn_kernel(te_ref, nu_ref, xs_ref, wg_ref, wu_ref, wd_ref, o_ref, xb_ref):
    j = pl.program_id(0)
    f = pl.program_id(1)
    used = j < nu_ref[0]

    @pl.when(used)
    def _():
        @pl.when(f == 0)
        def _():
            xb_ref[...] = xs_ref[...].astype(BF16)

        xb = xb_ref[...]
        y = _dot(jax.nn.silu(_dot(xb, wg_ref[0])) * _dot(xb, wu_ref[0]), wd_ref[0])

        @pl.when(f == 0)
        def _():
            o_ref[...] = y

        @pl.when(f > 0)
        def _():
            o_ref[...] += y

    @pl.when(jnp.logical_not(used) & (f == 0))
    def _():
        o_ref[...] = jnp.zeros_like(o_ref)


def _grouped_ffn(xs, w_gu, w_down, tile_expert, n_used, tmg, tf):
    n_slots = xs.shape[0]
    nf = w_down.shape[1] // tf
    tile = lambda j, f, te, nu: (jnp.minimum(j, nu[0] - 1), 0)
    chunk = lambda j, f, nu: jnp.where(j < nu[0], f, nf - 1)
    return pl.pallas_call(
        _grouped_ffn_kernel,
        grid_spec=pltpu.PrefetchScalarGridSpec(
            num_scalar_prefetch=2,
            grid=(n_slots // tmg, nf),
            in_specs=[pl.BlockSpec((tmg, D_MODEL), tile),
                      pl.BlockSpec((1, D_MODEL, tf), lambda j, f, te, nu: (te[j], 0, chunk(j, f, nu))),
                      pl.BlockSpec((1, D_MODEL, tf), lambda j, f, te, nu: (te[j], 0, nf + chunk(j, f, nu))),
                      pl.BlockSpec((1, tf, D_MODEL), lambda j, f, te, nu: (te[j], chunk(j, f, nu), 0))],
            out_specs=pl.BlockSpec((tmg, D_MODEL), lambda j, f, te, nu: (j, 0)),
            scratch_shapes=[pltpu.VMEM((tmg, D_MODEL), BF16)]),
        out_shape=jax.ShapeDtypeStruct((n_slots, D_MODEL), F32),
        compiler_params=_cparams(("arbitrary", "arbitrary")),
        name="moe_grouped",
    )(tile_expert, n_used, xs, w_gu, w_gu, w_down)


def _combine_kernel(pos_ref, posn_ref, x_ref, gt_ref, ys_ref, g_ref, b_ref, o_ref, ybuf, sem):
    i = pl.program_id(0)
    n = pl.num_programs(0)
    tm = x_ref.shape[0]
    slot = i % 2

    def gather(p_ref, s, r, k):
        return _row_copy(ys_ref, p_ref[0, 0, k * tm + r], ybuf.at[s, k], r, sem.at[s])

    def issue(p_ref, s):
        def body(r, carry):
            gather(p_ref, s, r, 0).start()
            gather(p_ref, s, r, 1).start()
            return carry
        for r in range(tm):
            body(r, 0)

    @pl.when(i == 0)
    def _():
        issue(pos_ref, 0)

    @pl.when(i + 1 < n)
    def _():
        issue(posn_ref, 1 - slot)

    def wait(r, carry):
        gather(pos_ref, slot, r, 0).wait()
        gather(pos_ref, slot, r, 1).wait()
        return carry

    lax.fori_loop(0, tm, wait, 0, unroll=8)
    gt = gt_ref[...]
    y = gt[:, 0:1] * ybuf[slot, 0] + gt[:, 1:2] * ybuf[slot, 1]
    o_ref[...] = _layer_norm(DEEPNORM_ALPHA * x_ref[...] + y, g_ref[...], b_ref[...])


def _combine(x, gtop, ys, pos, g, b, tm):
    m = x.shape[0]
    n = m // tm
    row = lambda i: (i, 0)
    fixed = lambda i: (0, 0)
    return pl.pallas_call(
        _combine_kernel,
        grid=(n,),
        in_specs=[pl.BlockSpec((1, 1, 2 * tm), lambda i: (i, 0, 0), memory_space=pltpu.SMEM),
                  pl.BlockSpec((1, 1, 2 * tm), lambda i: (jnp.minimum(i + 1, n - 1), 0, 0),
                               memory_space=pltpu.SMEM),
                  pl.BlockSpec((tm, D_MODEL), row), pl.BlockSpec((tm, LANES), row),
                  pl.BlockSpec(memory_space=pl.ANY),
                  pl.BlockSpec((1, D_MODEL), fixed), pl.BlockSpec((1, D_MODEL), fixed)],
        out_specs=pl.BlockSpec((tm, D_MODEL), row),
        out_shape=jax.ShapeDtypeStruct((m, D_MODEL), F32),
        scratch_shapes=[pltpu.VMEM((2, 2, tm, D_MODEL), F32), pltpu.SemaphoreType.DMA((2,))],
        compiler_params=_cparams(("arbitrary",)),
        name="moe_combine",
    )(pos, pos, x, gtop, ys, g.reshape(1, -1), b.reshape(1, -1))


def _moe(x, gtop, meta, cnt, w_gu, w_down, g, b, tmg, tf, tm):
    m = x.shape[0]
    n_tiles = -(-(2 * m + N_EXPERTS * (tmg - 1)) // tmg)
    pos, tile_expert, n_used, pads = _dispatch_plan(meta, cnt, tmg, n_tiles, tm)
    xs = _dispatch(x, pos, pads, n_tiles * tmg, tm)
    ys = _grouped_ffn(xs, w_gu, w_down, tile_expert, n_used, tmg, tf)
    return _combine(x, gtop, ys, pos, g, b, tm)


def _gdn_conv_post(conv, col0, q_ref, k_ref, v_ref):
    half = 0.5 * conv
    c = half + half * jnp.tanh(half)
    for j in range(conv.shape[1] // GDN_DK):
        col = col0 + GDN_DK * j
        x = c[:, GDN_DK * j:GDN_DK * (j + 1)]
        if col < GDN_QK:
            q_ref[:, col:col + GDN_DK] = x * lax.rsqrt(jnp.sum(x * x, axis=-1, keepdims=True) + L2_EPS) * GDN_DK ** -0.5
        elif col < 2 * GDN_QK:
            k_ref[:, col - GDN_QK:col - GDN_QK + GDN_DK] = x * lax.rsqrt(jnp.sum(x * x, axis=-1, keepdims=True) + L2_EPS)
        else:
            v_ref[:, col - 2 * GDN_QK:col - 2 * GDN_QK + GDN_DK] = x


def _gdn_gates(ab, alog_ref, dtb_ref, gb_ref):
    lane = lax.broadcasted_iota(jnp.int32, ab.shape, 1)
    decay = -jnp.exp(alog_ref[...]) * jax.nn.softplus(ab + dtb_ref[...])
    gb_ref[...] = jnp.where(lane < GDN_HEADS, decay, jax.nn.sigmoid(ab))


GDN_Z0 = GDN_CONV_DIM
GDN_AB0 = GDN_CONV_DIM + GDN_VW
GDN_QM0 = GDN_AB0 + GDN_AB_PAD
GDN_PROJ_GROUP = 2 * LANES


def _gdn_proj_kernel(x_ref, w_ref, cw_ref, alog_ref, dtb_ref,
                     q_ref, k_ref, v_ref, z_ref, gb_ref, qm_ref, cs_ref, buf_ref):
    i = pl.program_id(1)
    tm = x_ref.shape[0]
    pad = SUBLANES
    xb = x_ref[...].astype(BF16)

    @pl.when(i == 0)
    def _():
        buf_ref[0:pad, :] = jnp.zeros((pad, GDN_CONV_DIM), F32)

    def finish(c0, c1, d):
        cols = slice(c0, c1)
        if c1 <= GDN_CONV_DIM:
            buf_ref[pad:pad + tm, cols] = d
            conv = buf_ref[pad - 3:pad - 3 + tm, cols] * cw_ref[0:1, cols]
            conv = conv + buf_ref[pad - 2:pad - 2 + tm, cols] * cw_ref[1:2, cols]
            conv = conv + buf_ref[pad - 1:pad - 1 + tm, cols] * cw_ref[2:3, cols]
            conv = conv + d * cw_ref[3:4, cols]
            _gdn_conv_post(conv, c0, q_ref, k_ref, v_ref)
        elif c1 <= GDN_AB0:
            z_ref[:, c0 - GDN_Z0:c1 - GDN_Z0] = d
        else:
            _gdn_gates(d[:, :GDN_AB_PAD], alog_ref, dtb_ref, gb_ref)
            qm_ref[...] = d[:, GDN_AB_PAD:].astype(qm_ref.dtype)

    bounds = list(range(0, GDN_AB0, GDN_PROJ_GROUP)) + [GDN_AB0, w_ref.shape[1]]
    pending = None
    for c0, c1 in zip(bounds[:-1], bounds[1:]):
        d = _dot(xb, w_ref[:, c0:c1])
        if pending is not None:
            finish(*pending)
        pending = (c0, c1, d)
    finish(*pending)
    tail = buf_ref[tm:tm + pad, :]
    buf_ref[0:pad, :] = tail
    cs_ref[0] = tail


def _gdn_gate_params(a_log, dt_bias):
    padv = lambda v: jnp.pad(v.astype(F32), (0, GDN_AB_PAD - GDN_HEADS)).reshape(1, GDN_AB_PAD)
    return padv(a_log), padv(dt_bias)


def _gdn_pad_w_in(w_in):
    o3 = GDN_CONV_DIM + GDN_VW
    o4 = o3 + 2 * GDN_HEADS
    ab = jnp.pad(w_in[:, o3:o4], ((0, 0), (0, GDN_AB_PAD - 2 * GDN_HEADS)))
    return jnp.concatenate([w_in[:, :o3], ab, w_in[:, o4:]], axis=1)


def _gdn_project(x, w_pad, conv_w, a_log, dt_bias, batch, seq, tm):
    m = x.shape[0]
    nt = seq // tm
    row = lambda b, i: (b * nt + i, 0)
    fixed = lambda b, i: (0, 0)
    alog, dtb = _gdn_gate_params(a_log, dt_bias)
    wide = jax.ShapeDtypeStruct((m, GDN_QK), F32)
    return pl.pallas_call(
        _gdn_proj_kernel,
        grid=(batch, nt),
        in_specs=[pl.BlockSpec((tm, D_MODEL), row), pl.BlockSpec(w_pad.shape, fixed),
                  pl.BlockSpec(conv_w.shape, fixed), pl.BlockSpec(alog.shape, fixed),
                  pl.BlockSpec(dtb.shape, fixed)],
        out_specs=[pl.BlockSpec((tm, GDN_QK), row), pl.BlockSpec((tm, GDN_QK), row),
                   pl.BlockSpec((tm, GDN_VW), row), pl.BlockSpec((tm, GDN_VW), row),
                   pl.BlockSpec((tm, GDN_AB_PAD), row), pl.BlockSpec((tm, MEM_WIDTH), row),
                   pl.BlockSpec((1, SUBLANES, GDN_CONV_DIM), lambda b, i: (b, 0, 0))],
        out_shape=[wide, wide, wide, wide, jax.ShapeDtypeStruct((m, GDN_AB_PAD), F32),
                   jax.ShapeDtypeStruct((m, MEM_WIDTH), BF16),
                   jax.ShapeDtypeStruct((batch, SUBLANES, GDN_CONV_DIM), F32)],
        scratch_shapes=[pltpu.VMEM((tm + SUBLANES, GDN_CONV_DIM), F32)],
        compiler_params=_cparams(("parallel", "arbitrary")),
        name="gdn_proj",
    )(x, w_pad, conv_w, alog, dtb)


def _gdn_step_proj_kernel(x_ref, w_ref, cs_ref, cw_ref, alog_ref, dtb_ref,
                          pre_ref, q_ref, k_ref, v_ref, z_ref, gb_ref, qm_ref):
    proj = _dot(x_ref[...], w_ref[...])
    qkv = proj[:, :GDN_CONV_DIM]
    pre_ref[...] = qkv
    cw = cw_ref[...]
    conv = cs_ref[0] * cw[0:1]
    conv = conv + cs_ref[1] * cw[1:2]
    conv = conv + cs_ref[2] * cw[2:3]
    conv = conv + qkv * cw[3:4]
    _gdn_conv_post(conv, 0, q_ref, k_ref, v_ref)
    z_ref[...] = proj[:, GDN_Z0:GDN_AB0]
    _gdn_gates(proj[:, GDN_AB0:GDN_QM0], alog_ref, dtb_ref, gb_ref)
    qm_ref[...] = proj[:, GDN_QM0:].astype(qm_ref.dtype)


def _gdn_step_project(x, w_pad, conv_state, conv_w, a_log, dt_bias):
    m = x.shape[0]
    alog, dtb = _gdn_gate_params(a_log, dt_bias)
    cs = jnp.transpose(conv_state, (1, 0, 2))
    wide = jax.ShapeDtypeStruct((m, GDN_QK), F32)
    return pl.pallas_call(
        _gdn_step_proj_kernel,
        out_shape=[jax.ShapeDtypeStruct((m, GDN_CONV_DIM), F32), wide, wide, wide, wide,
                   jax.ShapeDtypeStruct((m, GDN_AB_PAD), F32), jax.ShapeDtypeStruct((m, MEM_WIDTH), F32)],
        compiler_params=pltpu.CompilerParams(vmem_limit_bytes=VMEM_LIMIT_MB * 1024 * 1024),
        name="gdn_step_proj",
    )(x, w_pad, cs, conv_w, alog, dtb)


def _gated_out(o, z, nw):
    on = o * lax.rsqrt(jnp.mean(o * o, axis=-1, keepdims=True) + RMS_EPS) * nw
    return on * jax.nn.silu(z)


GDN_SUPER = 2 * GDN_CHUNK
GDN_INTRA_ROWS = 256


def _gdn_intra_kernel(q_ref, k_ref, v_ref, gb_ref, u_ref, w_ref, qg_ref, kg_ref, qk_ref, egl_ref):
    n = GDN_SUPER
    c = GDN_CHUNK
    shift = c.bit_length() - 1
    r = lax.broadcasted_iota(jnp.int32, (n, n), 0)
    col = lax.broadcasted_iota(jnp.int32, (n, n), 1)
    same = (r >> shift) == (col >> shift)
    causal = same & (r >= col)
    strict = same & (r > col)
    eye = (r == col).astype(F32)
    tri = causal.astype(F32)
    groups = [slice(n * j, n * (j + 1)) for j in range(q_ref.shape[0] // n)]
    gb = [gb_ref[rows, :] for rows in groups]
    gc_cols = [_dot_f32(tri, x) for x in gb]
    gc_rows = [x.T for x in gc_cols]
    chunk_of_row = lax.broadcasted_iota(jnp.int32, (n, LANES), 0) >> shift
    gl_cols = []
    for x in gc_cols:
        total = x[n - 1:n, :]
        for i in reversed(range(n // c - 1)):
            total = jnp.where(chunk_of_row == i, x[c * (i + 1) - 1:c * (i + 1), :], total)
        gl_cols.append(total)
    for j, rows in enumerate(groups):
        egl_ref[rows, :] = jnp.exp(gl_cols[j])
    parts = [(j, h) for j in range(len(groups)) for h in range(GDN_HEADS)]
    idx = [(groups[j], slice(GDN_DK * h, GDN_DK * (h + 1))) for j, h in parts]
    ps = range(len(parts))
    q = [q_ref[i] for i in idx]
    k = [k_ref[i] for i in idx]
    gcol = [gc_cols[j][:, h:h + 1] for j, h in parts]
    beta = [gb[j][:, GDN_HEADS + h:GDN_HEADS + h + 1] for j, h in parts]
    decay = [jnp.exp(jnp.where(causal, gcol[p] - gc_rows[j][h:h + 1, :], -jnp.inf)) for p, (j, h) in enumerate(parts)]
    kb = [k[p] * beta[p] for p in ps]
    a = [jnp.where(strict, _dot_nt(kb[p], k[p]) * decay[p], 0.0) for p in ps]
    qk = [jnp.where(causal, _dot_nt(q[p], k[p]) * decay[p], 0.0) for p in ps]
    qk = [jnp.concatenate([m[c * i:c * (i + 1), c * i:c * (i + 1)] for i in range(n // c)], axis=0) for m in qk]
    for j, rows in enumerate(groups):
        qk_ref[rows, :] = jnp.concatenate(qk[GDN_HEADS * j:GDN_HEADS * (j + 1)], axis=1).astype(qk_ref.dtype)
    tinv = [eye - a[p] for p in ps]
    apow = a
    for _ in range(shift - 1):
        apow = [_dot(apow[p], apow[p]) for p in ps]
        tinv = [_dot(tinv[p], eye + apow[p]) for p in ps]
    eg = [jnp.exp(gcol[p]) for p in ps]
    for p in ps:
        u_ref[idx[p]] = _dot(tinv[p], v_ref[idx[p]] * beta[p])
    for p in ps:
        w_ref[idx[p]] = _dot(tinv[p], kb[p] * eg[p]).astype(w_ref.dtype)
    for p, (j, h) in enumerate(parts):
        qg_ref[idx[p]] = (q[p] * eg[p]).astype(qg_ref.dtype)
        kg_ref[idx[p]] = (k[p] * jnp.exp(gl_cols[j][:, h:h + 1] - gcol[p])).astype(kg_ref.dtype)


def _gdn_scan_kernel(u_ref, w_ref, qg_ref, kg_ref, qk_ref, egl_ref, z_ref, nw_ref, o_ref, sfin_ref, s_ref):
    n = pl.program_id(0)
    c = GDN_CHUNK

    @pl.when(n == 0)
    def _():
        s_ref[...] = jnp.zeros_like(s_ref)

    nw = nw_ref[...]

    def per_sequence(b, carry):
        egl = egl_ref[b, 0:1, :]
        heads = range(GDN_HEADS)
        sls = [slice(GDN_DK * h, GDN_DK * (h + 1)) for h in heads]
        state = [s_ref[b, h] for h in heads]
        ws_qs = [_dot(jnp.concatenate([w_ref[b, :, sls[h]], qg_ref[b, :, sls[h]]], axis=0), state[h])
                 for h in heads]
        v_new = [(u_ref[b, :, sls[h]] - ws_qs[h][:c]).astype(BF16) for h in heads]
        pairs = [qk_ref[b, :, LANES * j:LANES * (j + 1)] for j in range(GDN_HEADS // 2)]
        o = [ws_qs[h][c:] + _dot(pairs[h // 2][:, c * (h % 2):c * (h % 2 + 1)], v_new[h]) for h in heads]
        upd = [_dot_tn(kg_ref[b, :, sls[h]], v_new[h]) for h in heads]
        for h in heads:
            s_ref[b, h] = state[h] * egl[:, h:h + 1] + upd[h]
        for h in heads:
            o_ref[b, :, sls[h]] = _gated_out(o[h], z_ref[b, :, sls[h]], nw).astype(o_ref.dtype)
        return carry

    lax.fori_loop(0, u_ref.shape[0], per_sequence, 0, unroll=4)

    @pl.when(n == pl.num_programs(0) - 1)
    def _():
        sfin_ref[...] = s_ref[...]


def _gdn_chunked(q, k, v, z, gb, norm_w, batch, seq):
    m = batch * seq
    row = lambda i: (i, 0)
    wide = pl.BlockSpec((GDN_INTRA_ROWS, GDN_QK), row)
    qk_w = GDN_HEADS * GDN_CHUNK
    u, w, qg, kg, qk, egl = pl.pallas_call(
        _gdn_intra_kernel,
        grid=(m // GDN_INTRA_ROWS,),
        in_specs=[wide, wide, wide, pl.BlockSpec((GDN_INTRA_ROWS, GDN_AB_PAD), row)],
        out_specs=[wide, wide, wide, wide, pl.BlockSpec((GDN_INTRA_ROWS, qk_w), row),
                   pl.BlockSpec((GDN_INTRA_ROWS, LANES), row)],
        out_shape=[jax.ShapeDtypeStruct((m, GDN_VW), F32), jax.ShapeDtypeStruct((m, GDN_QK), BF16),
                   jax.ShapeDtypeStruct((m, GDN_QK), BF16), jax.ShapeDtypeStruct((m, GDN_QK), BF16),
                   jax.ShapeDtypeStruct((m, qk_w), BF16), jax.ShapeDtypeStruct((m, LANES), F32)],
        compiler_params=_cparams(("parallel",)),
        name="gdn_intra",
    )(q, k, v, gb)
    per_seq = lambda a: a.reshape(batch, seq, a.shape[-1])
    chunk = lambda width: pl.BlockSpec((batch, GDN_CHUNK, width), lambda n: (0, n, 0))
    state_spec = pl.BlockSpec((batch, GDN_HEADS, GDN_DK, GDN_DV), lambda n: (0, 0, 0, 0))
    h, s_fin = pl.pallas_call(
        _gdn_scan_kernel,
        grid=(seq // GDN_CHUNK,),
        in_specs=[chunk(GDN_VW), chunk(GDN_QK), chunk(GDN_QK), chunk(GDN_QK), chunk(qk_w), chunk(LANES),
                  chunk(GDN_VW), pl.BlockSpec((1, GDN_DV), lambda n: (0, 0))],
        out_specs=[chunk(GDN_VW), state_spec],
        out_shape=[jax.ShapeDtypeStruct((batch, seq, GDN_VW), BF16),
                   jax.ShapeDtypeStruct((batch, GDN_HEADS, GDN_DK, GDN_DV), F32)],
        scratch_shapes=[pltpu.VMEM((batch, GDN_HEADS, GDN_DK, GDN_DV), F32)],
        compiler_params=_cparams(("arbitrary",)),
        name="gdn_scan",
    )(per_seq(u), per_seq(w), per_seq(qg), per_seq(kg), per_seq(qk), per_seq(egl), per_seq(z),
      norm_w.reshape(1, -1))
    return h.reshape(m, GDN_VW), s_fin


def _gdn_recurrent_kernel(q_ref, k_ref, v_ref, z_ref, gb_ref, nw_ref, s_ref, o_ref, so_ref):
    bs = q_ref.shape[0]
    gb = gb_ref[...]
    nw = nw_ref[...]
    for h in range(GDN_HEADS):
        sl = slice(GDN_DK * h, GDN_DK * (h + 1))
        q_t = q_ref[:, sl].T
        k_t = k_ref[:, sl].T
        v = v_ref[:, sl]
        rows = []
        for b in range(bs):
            state = s_ref[b, h] * jnp.exp(gb[b:b + 1, h:h + 1])
            kcol = k_t[:, b:b + 1]
            v_new = (v[b:b + 1, :] - jnp.sum(kcol * state, axis=0, keepdims=True)) \
                * gb[b:b + 1, GDN_HEADS + h:GDN_HEADS + h + 1]
            state = state + kcol * v_new
            so_ref[b, h] = state
            rows.append(jnp.sum(q_t[:, b:b + 1] * state, axis=0, keepdims=True))
        o_ref[:, sl] = _gated_out(jnp.concatenate(rows, axis=0), z_ref[:, sl], nw).astype(o_ref.dtype)


def _gdn_recurrent(q, k, v, z, gb, norm_w, state, bs):
    batch = q.shape[0]
    row = lambda i: (i, 0)
    wide = pl.BlockSpec((bs, GDN_QK), row)
    st = pl.BlockSpec((bs, GDN_HEADS, GDN_DK, GDN_DV), lambda i: (i, 0, 0, 0))
    return pl.pallas_call(
        _gdn_recurrent_kernel,
        grid=(batch // bs,),
        in_specs=[wide, wide, wide, wide, pl.BlockSpec((bs, GDN_AB_PAD), row),
                  pl.BlockSpec((1, GDN_DV), lambda i: (0, 0)), st],
        out_specs=[wide, st],
        out_shape=[jax.ShapeDtypeStruct((batch, GDN_VW), F32), jax.ShapeDtypeStruct(state.shape, F32)],
        compiler_params=_cparams(("parallel",)),
        name="gdn_recurrent",
    )(q, k, v, z, gb, norm_w.reshape(1, -1), state)


PROMPT_TM = 1024
GDN_TM = 512
FFN_TM = 1024
FFN_TF = 512
MOE_TMG = 1024
MOE_TM = 512
DECODE_BS = 8


def kernel(x_prompt, x_sample, cache_swa_k, cache_swa_v, state_gdn_conv, state_gdn_rec, cache_mem_k, cache_mem_v, mem_prompt, w_in_swa, swa_sinks, w_in_gdn, gdn_conv_w, gdn_a_log, gdn_dt_bias, gdn_norm_w, w_mem_kv, w_out, ln1_g, ln1_b, ln2_g, ln2_b, ffn_w_gu, ffn_w_down, router_w, moe_w_gu, moe_w_down):
    batch, seq, _ = x_prompt.shape
    dec = x_sample.shape[0]
    assert x_sample.shape[1] == 1
    yp = x_prompt.reshape(batch * seq, D_MODEL)
    ys = x_sample.reshape(dec, D_MODEL)
    mem_flat = mem_prompt.reshape(batch * N_MEM, D_MODEL)
    mem3 = lambda a: a.reshape(-1, N_MEM, MEM_WIDTH)

    pair_perm = _swa_pair_perm()
    w_in0 = jnp.concatenate([w_in_swa[0][:, :SWA_QW][:, pair_perm], w_in_swa[0][:, SWA_QW:]], axis=1).astype(BF16)
    w_out0 = w_out[0].astype(BF16)
    w_out0_paired = jnp.concatenate([w_out0[:SWA_QW][pair_perm], w_out0[SWA_QW:]], axis=0)
    w_gu0 = ffn_w_gu[0:1].astype(BF16)
    w_dn0 = ffn_w_down[0:1].astype(BF16)
    mkv = _matmul(mem_flat, w_mem_kv[0].astype(BF16), PROMPT_TM)
    mk0, mv0 = mkv[:, :MEM_WIDTH], mkv[:, MEM_WIDTH:]
    cos_p, sin_p = _rope_tables(jnp.arange(seq, dtype=jnp.int32))
    q, k, v, qm, k_tail, v_tail = _swa_project(yp, w_in0, cos_p, sin_p, PROMPT_TM)
    h = _swa_attention(q, k, v, swa_sinks[0], batch, seq)
    ma = _mem_attention(qm, mem3(mk0), mem3(mv0), batch, seq, PROMPT_TM)
    swa_kp = k_tail.reshape(batch, WINDOW, SWA_KV_HEADS, HEAD_DIM)
    swa_vp = v_tail.reshape(batch, WINDOW, SWA_KV_HEADS, HEAD_DIM)
    yp = _out_project(yp, h, ma, w_out0_paired, ln1_g[0], ln1_b[0], PROMPT_TM)
    yp = _ffn(yp, w_gu0, w_dn0, ln2_g[0], ln2_b[0], FFN_TM, FFN_TF)

    cos_s, sin_s = _rope_tables(jnp.full((dec,), PAST_LEN, jnp.int32))
    q, k, v, qm, _, _ = _swa_project(ys, w_in0, cos_s, sin_s, dec)
    q = q[:, np.argsort(pair_perm)]
    qg = q.reshape(dec, SWA_KV_HEADS, SWA_GROUP, HEAD_DIM).transpose(2, 0, 1, 3).reshape(SWA_GROUP, dec, SWA_KVW)
    sink_g = swa_sinks[0].reshape(SWA_KV_HEADS, SWA_GROUP).T
    mem_kt, mem_vt = _feature_major(cache_mem_k), _feature_major(cache_mem_v)
    og, swa_ks, swa_vs = _decode_attention(
        qg, _feature_major(cache_swa_k), _feature_major(cache_swa_v),
        1.0, HEAD_DIM, DECODE_BS, new_k=k, new_v=v, sinks=sink_g)
    h = og.reshape(SWA_GROUP, dec, SWA_KV_HEADS, HEAD_DIM).transpose(1, 2, 0, 3).reshape(dec, SWA_QW)
    ma, = _decode_attention(qm.reshape(1, dec, MEM_WIDTH), mem_kt, mem_vt,
                            MEM_HEAD_DIM ** -0.5, MEM_HEAD_DIM, DECODE_BS, layer=0)
    ys = _out_project(ys, h, ma[0], w_out0, ln1_g[0], ln1_b[0], dec)
    ys = _ffn(ys, w_gu0, w_dn0, ln2_g[0], ln2_b[0], dec, FFN_TF)

    w_in1 = _gdn_pad_w_in(w_in_gdn[0]).astype(BF16)
    w_out1 = w_out[1].astype(BF16)
    w_gu1 = moe_w_gu[0]
    w_dn1 = moe_w_down[0]
    mkv = _matmul(mem_flat, w_mem_kv[1].astype(BF16), PROMPT_TM)
    mk1, mv1 = mkv[:, :MEM_WIDTH], mkv[:, MEM_WIDTH:]
    q, k, v, z, gb, qm, conv_tail = _gdn_project(yp, w_in1, gdn_conv_w[0], gdn_a_log[0], gdn_dt_bias[0],
                                                 batch, seq, GDN_TM)
    h, rec_p = _gdn_chunked(q, k, v, z, gb, gdn_norm_w[0], batch, seq)
    ma = _mem_attention(qm, mem3(mk1), mem3(mv1), batch, seq, PROMPT_TM)
    yp, gtop, meta, cnt = _out_project(yp, h, ma, w_out1, ln1_g[1], ln1_b[1], PROMPT_TM, router_w=router_w[0])
    yp = _moe(yp, gtop, meta, cnt, w_gu1, w_dn1, ln2_g[1], ln2_b[1], MOE_TMG, FFN_TF, MOE_TM)

    pre, q, k, v, z, gb, qm = _gdn_step_project(ys, w_in1, state_gdn_conv[0], gdn_conv_w[0],
                                                gdn_a_log[0], gdn_dt_bias[0])
    h, rec_s = _gdn_recurrent(q, k, v, z, gb, gdn_norm_w[0], state_gdn_rec[0], DECODE_BS)
    ma, = _decode_attention(qm.reshape(1, dec, MEM_WIDTH), mem_kt, mem_vt,
                            MEM_HEAD_DIM ** -0.5, MEM_HEAD_DIM, DECODE_BS, layer=1)
    ys = _out_project(ys, h, ma[0], w_out1, ln1_g[1], ln1_b[1], dec)
    ys = _ffn(ys, w_gu1, w_dn1, ln2_g[1], ln2_b[1], dec, FFN_TF, router_w=router_w[0])

    mem_shape = (batch, N_MEM, MEM_HEADS, MEM_HEAD_DIM)
    conv_s = jnp.concatenate([state_gdn_conv[0][:, 1:], pre[:, None, :]], axis=1)
    return (yp.reshape(batch, seq, D_MODEL), ys.reshape(dec, 1, D_MODEL),
            swa_kp[None], swa_vp[None],
            conv_tail[None, :, SUBLANES - (GDN_CONV_W - 1):, :], rec_p[None],
            jnp.stack([mk0.reshape(mem_shape), mk1.reshape(mem_shape)]),
            jnp.stack([mv0.reshape(mem_shape), mv1.reshape(mem_shape)]),
            _key_major(swa_ks, SWA_KV_HEADS)[None], _key_major(swa_vs, SWA_KV_HEADS)[None],
            conv_s[None], rec_s[None])
```

```python
import functools

import jax
import jax.numpy as jnp
import numpy as np
from jax import lax
from jax.experimental import pallas as pl
from jax.experimental.pallas import tpu as pltpu

F32 = jnp.float32
BF16 = jnp.bfloat16

D_MODEL = 1024
DEPTH = 2
PAST_LEN = 16384
SWA_HEADS = 12
SWA_KV_HEADS = 4
SWA_GROUP = SWA_HEADS // SWA_KV_HEADS
HEAD_DIM = 64
WINDOW = 128
SWA_BLOCK = 128
ROPE_THETA = 10000.0
SWA_QW = SWA_HEADS * HEAD_DIM
SWA_KVW = SWA_KV_HEADS * HEAD_DIM
GDN_HEADS = 6
GDN_DK = 128
GDN_DV = 128
GDN_CONV_W = 4
GDN_CHUNK = 64
GDN_QK = GDN_HEADS * GDN_DK
GDN_VW = GDN_HEADS * GDN_DV
GDN_CONV_DIM = 2 * GDN_QK + GDN_VW
N_MEM = 256
MEM_HEADS = 4
MEM_HEAD_DIM = 64
MEM_WIDTH = MEM_HEADS * MEM_HEAD_DIM
D_FF = 3584
N_EXPERTS = 8
DEEPNORM_ALPHA = (2 * DEPTH) ** 0.25
LN_EPS = 1e-5
RMS_EPS = 1e-6
L2_EPS = 1e-6

LANES = 128
SUBLANES = 8
BF16_ROWS = 16
GDN_AB_PAD = LANES
VMEM_LIMIT_MB = 56


def _cparams(sem, vmem_mb=VMEM_LIMIT_MB):
    return pltpu.CompilerParams(dimension_semantics=sem, vmem_limit_bytes=vmem_mb * 1024 * 1024)


def _bf16_round(x):
    return x.astype(BF16).astype(F32)


def _dot(a, b):
    return jnp.dot(a.astype(BF16), b.astype(BF16), preferred_element_type=F32)


def _dot_nt(a, b):
    return lax.dot_general(a.astype(BF16), b.astype(BF16), (((1,), (1,)), ((), ())),
                           preferred_element_type=F32)


def _dot_tn(a, b):
    return lax.dot_general(a.astype(BF16), b.astype(BF16), (((0,), (0,)), ((), ())),
                           preferred_element_type=F32)


def _dot_f32(a, b):
    return jnp.dot(a, b, preferred_element_type=F32, precision=lax.Precision.HIGHEST)


def _layer_norm(t, g, b):
    mu = jnp.mean(t, axis=-1, keepdims=True)
    d = t - mu
    var = jnp.mean(d * d, axis=-1, keepdims=True)
    return d * lax.rsqrt(var + LN_EPS) * g + b


def _mm_kernel(x_ref, w_ref, o_ref):
    o_ref[...] = _dot(x_ref[...], w_ref[...])


def _matmul(x, w, tm):
    m, k = x.shape
    n = w.shape[1]
    return pl.pallas_call(
        _mm_kernel,
        grid=(m // tm,),
        in_specs=[pl.BlockSpec((tm, k), lambda i: (i, 0)),
                  pl.BlockSpec((k, n), lambda i: (0, 0))],
        out_specs=pl.BlockSpec((tm, n), lambda i: (i, 0)),
        out_shape=jax.ShapeDtypeStruct((m, n), F32),
        compiler_params=_cparams(("parallel",)),
        name="matmul",
    )(x, w)


def _swa_proj_kernel(x_ref, w_ref, cos_ref, sin_ref, q_ref, k_ref, v_ref, qm_ref, kt_ref, vt_ref):
    proj = _dot(x_ref[...], w_ref[...])
    cos = cos_ref[...]
    sin = sin_ref[...]
    lane = lax.broadcasted_iota(jnp.int32, cos.shape, 1)
    first_half = (lane & (HEAD_DIM - 1)) < HEAD_DIM // 2

    def rope(xb):
        partner = jnp.where(first_half, pltpu.roll(xb, LANES - HEAD_DIM // 2, 1),
                            pltpu.roll(xb, HEAD_DIM // 2, 1))
        return xb * cos + partner * sin

    for j in range(SWA_QW // LANES):
        sl = slice(LANES * j, LANES * (j + 1))
        q_ref[:, sl] = (rope(proj[:, sl]) * HEAD_DIM ** -0.5).astype(q_ref.dtype)
    for j in range(SWA_KVW // LANES):
        k_ref[:, LANES * j:LANES * (j + 1)] = rope(proj[:, SWA_QW + LANES * j:SWA_QW + LANES * (j + 1)])
    v_ref[...] = proj[:, SWA_QW + SWA_KVW:SWA_QW + 2 * SWA_KVW]
    qm_ref[...] = proj[:, SWA_QW + 2 * SWA_KVW:].astype(qm_ref.dtype)
    tm = x_ref.shape[0]
    kt_ref[...] = k_ref[tm - WINDOW:, :]
    vt_ref[...] = v_ref[tm - WINDOW:, :]


def _rope_tables(pos):
    half = HEAD_DIM // 2
    inv = ROPE_THETA ** (-jnp.arange(half, dtype=F32) / half)
    ang = pos.astype(F32)[:, None] * inv[None, :]
    cos = jnp.cos(ang)
    sin = jnp.sin(ang)
    reps = LANES // HEAD_DIM
    return jnp.tile(cos, (1, 2 * reps)), jnp.tile(jnp.concatenate([-sin, sin], axis=1), (1, reps))


def _swa_project(x, w, cos, sin, tm):
    m = x.shape[0]
    n_in = w.shape[1]
    tab_blocks = cos.shape[0] // tm
    n_seq = m // (tm * tab_blocks)
    assert tm >= WINDOW
    row = lambda i: (i, 0)
    tail = lambda i: (i // tab_blocks, 0)
    return pl.pallas_call(
        _swa_proj_kernel,
        grid=(m // tm,),
        in_specs=[pl.BlockSpec((tm, D_MODEL), row),
                  pl.BlockSpec((D_MODEL, n_in), lambda i: (0, 0)),
                  pl.BlockSpec((tm, LANES), lambda i: (i % tab_blocks, 0)),
                  pl.BlockSpec((tm, LANES), lambda i: (i % tab_blocks, 0))],
        out_specs=[pl.BlockSpec((tm, SWA_QW), row), pl.BlockSpec((tm, SWA_KVW), row),
                   pl.BlockSpec((tm, SWA_KVW), row), pl.BlockSpec((tm, MEM_WIDTH), row),
                   pl.BlockSpec((WINDOW, SWA_KVW), tail), pl.BlockSpec((WINDOW, SWA_KVW), tail)],
        out_shape=[jax.ShapeDtypeStruct((m, SWA_QW), BF16),
                   jax.ShapeDtypeStruct((m, SWA_KVW), F32),
                   jax.ShapeDtypeStruct((m, SWA_KVW), F32),
                   jax.ShapeDtypeStruct((m, MEM_WIDTH), BF16),
                   jax.ShapeDtypeStruct((n_seq * WINDOW, SWA_KVW), F32),
                   jax.ShapeDtypeStruct((n_seq * WINDOW, SWA_KVW), F32)],
        compiler_params=_cparams(("arbitrary",)),
        name="swa_proj",
    )(x, w, cos, sin)


def _swa_pair_heads():
    return [(SWA_GROUP * (2 * p) + g, SWA_GROUP * (2 * p + 1) + g)
            for p in range(SWA_KV_HEADS // 2) for g in range(SWA_GROUP)]


def _swa_pair_perm():
    cols = []
    for a, b in _swa_pair_heads():
        cols += list(range(HEAD_DIM * a, HEAD_DIM * (a + 1))) + list(range(HEAD_DIM * b, HEAD_DIM * (b + 1)))
    return np.asarray(cols, np.int32)


SWA_QBLOCKS = 8


def _swa_attn_kernel(sink_ref, q_ref, kp_ref, kc_ref, vp_ref, vc_ref, o_ref):
    i = pl.program_id(1)
    blk = SWA_BLOCK
    nq = q_ref.shape[0] // blk
    kwin = jnp.concatenate([kp_ref[...], kc_ref[...]], axis=0).astype(BF16)
    vwin = jnp.concatenate([vp_ref[...], vc_ref[...]], axis=0).astype(BF16)
    qi = lax.broadcasted_iota(jnp.int32, (blk, 2 * blk), 0)
    kj = lax.broadcasted_iota(jnp.int32, (blk, 2 * blk), 1)
    band = (kj >= qi) & (kj <= qi + WINDOW)
    lo = lax.broadcasted_iota(jnp.int32, (blk, LANES), 1) < HEAD_DIM
    pair_heads = _swa_pair_heads()
    zero = jnp.zeros((blk, LANES), q_ref.dtype)
    ones = jnp.ones((2 * blk, LANES), BF16)
    parts = [(j, p) for j in range(nq) for p in range(SWA_KV_HEADS // 2)]
    ps = range(len(parts))
    qs, sink, mask, kslab, vslab = [], [], [], [], []
    for j, p in parts:
        ms = range(SWA_GROUP * p, SWA_GROUP * (p + 1))
        blocks = [q_ref[blk * j:blk * (j + 1), LANES * m:LANES * (m + 1)] for m in ms]
        qs.append(jnp.concatenate([jnp.where(lo, x, zero) for x in blocks]
                                  + [jnp.where(lo, zero, x) for x in blocks], axis=0))
        heads = [pair_heads[m][0] for m in ms] + [pair_heads[m][1] for m in ms]
        sink.append(jnp.concatenate([jnp.full((blk, 1), sink_ref[h], F32) for h in heads], axis=0))
        valid = band & ((kj >= blk) | (i > 0)) if j == 0 else band
        mask.append(jnp.concatenate([valid] * (2 * SWA_GROUP), axis=0))
        kslab.append(kwin[blk * j:blk * (j + 2), LANES * p:LANES * (p + 1)])
        vslab.append(jnp.concatenate([vwin[blk * j:blk * (j + 2), LANES * p:LANES * (p + 1)], ones], axis=1))
    s = [jnp.where(mask[t], _dot_nt(qs[t], kslab[t]), -jnp.inf) for t in ps]
    m = [jnp.maximum(jnp.max(s[t], axis=-1, keepdims=True), sink[t]) for t in ps]
    e = [jnp.exp(s[t] - m[t]) for t in ps]
    ov = [_dot(e[t], vslab[t]) for t in ps]
    o = [ov[t][:, :LANES] * (1.0 / (ov[t][:, LANES:] + jnp.exp(sink[t] - m[t]))) for t in ps]
    for j in range(nq):
        outs = [jnp.where(lo, o[t][blk * g:blk * (g + 1)], o[t][blk * (SWA_GROUP + g):blk * (SWA_GROUP + g + 1)])
                for t in ps if parts[t][0] == j for g in range(SWA_GROUP)]
        o_ref[blk * j:blk * (j + 1), :] = jnp.concatenate(outs, axis=1).astype(o_ref.dtype)


def _swa_attention(q, k, v, sinks, batch, seq):
    step = SWA_QBLOCKS * SWA_BLOCK
    ns = seq // step
    cur = lambda b, i: (b * ns + i, 0)
    prev = lambda b, i: (b * ns * SWA_QBLOCKS + jnp.maximum(i * SWA_QBLOCKS - 1, 0), 0)
    return pl.pallas_call(
        _swa_attn_kernel,
        grid=(batch, ns),
        in_specs=[pl.BlockSpec(memory_space=pltpu.SMEM),
                  pl.BlockSpec((step, SWA_QW), cur),
                  pl.BlockSpec((SWA_BLOCK, SWA_KVW), prev),
                  pl.BlockSpec((step, SWA_KVW), cur),
                  pl.BlockSpec((SWA_BLOCK, SWA_KVW), prev),
                  pl.BlockSpec((step, SWA_KVW), cur)],
        out_specs=pl.BlockSpec((step, SWA_QW), cur),
        out_shape=jax.ShapeDtypeStruct((batch * seq, SWA_QW), BF16),
        compiler_params=_cparams(("parallel", "parallel")),
        name="swa_attn",
    )(sinks, q, k, k, v, v)


def _mem_attn_kernel(q_ref, k_ref, v_ref, o_ref):
    q = q_ref[...]
    k = k_ref[0].astype(BF16)
    v = v_ref[0].astype(BF16)
    tq = q.shape[0]
    lo = lax.broadcasted_iota(jnp.int32, (tq, LANES), 1) < MEM_HEAD_DIM
    zero = jnp.zeros((tq, LANES), q.dtype)
    slabs = range(MEM_WIDTH // LANES)
    sls = [slice(LANES * j, LANES * (j + 1)) for j in slabs]
    qs = [jnp.concatenate([jnp.where(lo, q[:, sl], zero), jnp.where(lo, zero, q[:, sl])], axis=0) for sl in sls]
    s = [_dot_nt(qs[j], k[:, sls[j]]) * MEM_HEAD_DIM ** -0.5 for j in slabs]
    e = [jnp.exp(s[j] - jnp.max(s[j], axis=-1, keepdims=True)) for j in slabs]
    ones = jnp.ones((k.shape[0], LANES), BF16)
    ov = [_dot(e[j], jnp.concatenate([v[:, sls[j]], ones], axis=1)) for j in slabs]
    o = [ov[j][:, :LANES] * (1.0 / ov[j][:, LANES:]) for j in slabs]
    o_ref[...] = jnp.concatenate([jnp.where(lo, o[j][:tq], o[j][tq:]) for j in slabs], axis=1).astype(o_ref.dtype)


def _mem_attention(qm, mem_k, mem_v, batch, seq, tq):
    nq = seq // tq
    return pl.pallas_call(
        _mem_attn_kernel,
        grid=(batch, nq),
        in_specs=[pl.BlockSpec((tq, MEM_WIDTH), lambda b, i: (b * nq + i, 0)),
                  pl.BlockSpec((1, N_MEM, MEM_WIDTH), lambda b, i: (b, 0, 0)),
                  pl.BlockSpec((1, N_MEM, MEM_WIDTH), lambda b, i: (b, 0, 0))],
        out_specs=pl.BlockSpec((tq, MEM_WIDTH), lambda b, i: (b * nq + i, 0)),
        out_shape=jax.ShapeDtypeStruct((batch * seq, MEM_WIDTH), BF16),
        compiler_params=_cparams(("parallel", "parallel")),
        name="mem_attn",
    )(qm, mem_k, mem_v)


def _decode_attn_kernel(*refs, scale, head_dim, with_self):
    if with_self:
        q_ref, kt_ref, vt_ref, kn_ref, vn_ref, knc_ref, vnc_ref, sink_ref, o_ref, ko_ref, vo_ref = refs
    else:
        q_ref, kt_ref, vt_ref, o_ref = refs
    bs, width, n_keys = kt_ref.shape
    n_groups = q_ref.shape[0]
    hp = SUBLANES
    shift = head_dim.bit_length() - 1
    own = lax.broadcasted_iota(jnp.int32, (hp, width), 0) == (lax.broadcasted_iota(jnp.int32, (hp, width), 1) >> shift)
    samples = range(bs)
    kt_r = [kt_ref[b].astype(BF16) for b in samples]
    vt_r = [vt_ref[b].astype(BF16) for b in samples]
    qh = [jnp.concatenate([jnp.where(own, _bf16_round(q_ref[g, b:b + 1, :].astype(F32) * scale), 0.0)
                           for g in range(n_groups)], axis=0) for b in samples]
    s = [_dot(qh[b], kt_r[b]) for b in samples]
    m = [jnp.max(s[b], axis=-1, keepdims=True) for b in samples]
    if with_self:
        kn = kn_ref[...]
        vn = vn_ref[...]
        sink = sink_ref[...]
        s_self = [jnp.sum(qh[b] * _bf16_round(kn[b:b + 1, :]), axis=-1, keepdims=True) for b in samples]
        m = [jnp.maximum(jnp.maximum(m[b], s_self[b]), sink) for b in samples]
    e = [jnp.exp(s[b] - m[b]) for b in samples]
    den = [jnp.sum(e[b], axis=-1, keepdims=True) for b in samples]
    if with_self:
        e_self = [jnp.exp(s_self[b] - m[b]) for b in samples]
        den = [den[b] + e_self[b] + jnp.exp(sink - m[b]) for b in samples]
    inv = [1.0 / den[b] for b in samples]
    ov = [_dot_nt(e[b] * inv[b], vt_r[b]) for b in samples]
    if with_self:
        ov = [ov[b] + _bf16_round(e_self[b] * inv[b]) * _bf16_round(vn[b:b + 1, :]) for b in samples]
    for b in samples:
        for g in range(n_groups):
            o_ref[g, b:b + 1, :] = jnp.sum(jnp.where(own, ov[b][hp * g:hp * (g + 1)], 0.0), axis=0, keepdims=True)
    if with_self:
        last_key = lax.broadcasted_iota(jnp.int32, (width, n_keys), 1) == n_keys - 1
        knc = knc_ref[0]
        vnc = vnc_ref[0]
        for b in samples:
            ko_ref[b] = jnp.where(last_key, knc[:, b:b + 1], pltpu.roll(kt_ref[b], n_keys - 1, 1))
            vo_ref[b] = jnp.where(last_key, vnc[:, b:b + 1], pltpu.roll(vt_ref[b], n_keys - 1, 1))


def _sample_columns(x, bs):
    batch, width = x.shape
    return jnp.swapaxes(x.reshape(batch // bs, bs, width), 1, 2)


def _decode_attention(q_groups, cache_kt, cache_vt, scale, head_dim, bs, layer=0, new_k=None, new_v=None, sinks=None):
    n_groups, batch, width = q_groups.shape
    n_keys = cache_kt.shape[2]
    n_heads = width // head_dim
    assert n_heads <= SUBLANES
    nblk = batch // bs
    first = layer * nblk
    with_self = new_k is not None
    cache3 = pl.BlockSpec((bs, width, n_keys), lambda i: (first + i, 0, 0))
    qspec = pl.BlockSpec((n_groups, bs, width), lambda i: (0, i, 0))
    in_specs = [qspec, cache3, cache3]
    args = [q_groups, cache_kt, cache_vt]
    out_specs = [qspec]
    out_shape = [jax.ShapeDtypeStruct((n_groups, batch, width), F32)]
    if with_self:
        row = pl.BlockSpec((bs, width), lambda i: (i, 0))
        col = pl.BlockSpec((1, width, bs), lambda i: (i, 0, 0))
        sink_col = jnp.pad(sinks, ((0, 0), (0, SUBLANES - n_heads))).reshape(n_groups * SUBLANES, 1)
        in_specs += [row, row, col, col, pl.BlockSpec(sink_col.shape, lambda i: (0, 0))]
        args += [new_k, new_v, _sample_columns(new_k, bs), _sample_columns(new_v, bs), sink_col]
        blk3 = pl.BlockSpec((bs, width, n_keys), lambda i: (i, 0, 0))
        out_specs += [blk3, blk3]
        out_shape += [jax.ShapeDtypeStruct((batch, width, n_keys), F32)] * 2
    return pl.pallas_call(
        functools.partial(_decode_attn_kernel, scale=scale, head_dim=head_dim, with_self=with_self),
        grid=(nblk,),
        in_specs=in_specs, out_specs=out_specs, out_shape=out_shape,
        compiler_params=_cparams(("parallel",)),
        name="decode_attn",
    )(*args)


def _feature_major(cache):
    layers, batch, n_keys, heads, hd = cache.shape
    return jnp.transpose(cache, (0, 1, 3, 4, 2)).reshape(layers * batch, heads * hd, n_keys)


def _key_major(cache_t, heads):
    batch, width, n_keys = cache_t.shape
    return jnp.transpose(cache_t.reshape(batch, heads, width // heads, n_keys), (0, 3, 1, 2))


def _outproj_kernel(x_ref, h_ref, m_ref, w1_ref, w2_ref, g_ref, b_ref, *rest, route):
    t = _dot(h_ref[...], w1_ref[...]) + _dot(m_ref[...], w2_ref[...])
    y = _layer_norm(DEEPNORM_ALPHA * x_ref[...] + t, g_ref[...], b_ref[...])
    if route:
        rw_ref, o_ref, gtop_ref, meta_ref, cnt_ref, carry_ref = rest
        _route_tile(y, rw_ref, gtop_ref, meta_ref, cnt_ref, carry_ref)
    else:
        o_ref, = rest
    o_ref[...] = y


def _out_project(x, h, mem, w_out, g, b, tm, router_w=None):
    m = x.shape[0]
    hw = h.shape[1]
    w1, w2 = w_out[:hw], w_out[hw:]
    row = lambda i: (i, 0)
    fixed = lambda i: (0, 0)
    route = router_w is not None
    in_specs = [pl.BlockSpec((tm, D_MODEL), row), pl.BlockSpec((tm, hw), row),
                pl.BlockSpec((tm, MEM_WIDTH), row), pl.BlockSpec(w1.shape, fixed),
                pl.BlockSpec(w2.shape, fixed), pl.BlockSpec((1, D_MODEL), fixed),
                pl.BlockSpec((1, D_MODEL), fixed)]
    args = [x, h, mem, w1, w2, g.reshape(1, -1), b.reshape(1, -1)]
    out_specs = [pl.BlockSpec((tm, D_MODEL), row)]
    out_shape = [jax.ShapeDtypeStruct((m, D_MODEL), F32)]
    scratch = []
    if route:
        rw_t = jnp.pad(router_w.T, ((0, BF16_ROWS - N_EXPERTS), (0, 0))).astype(BF16)
        in_specs.append(pl.BlockSpec(rw_t.shape, fixed))
        args.append(rw_t)
        out_specs += [pl.BlockSpec((tm, LANES), row), pl.BlockSpec((SUBLANES, tm), lambda i: (0, i)),
                      pl.BlockSpec((SUBLANES, LANES), fixed)]
        out_shape += [jax.ShapeDtypeStruct((m, LANES), F32), jax.ShapeDtypeStruct((SUBLANES, m), jnp.int32),
                      jax.ShapeDtypeStruct((SUBLANES, LANES), jnp.int32)]
        scratch = [pltpu.VMEM((N_EXPERTS, LANES), F32)]
    outs = pl.pallas_call(
        functools.partial(_outproj_kernel, route=route),
        grid=(m // tm,),
        in_specs=in_specs, out_specs=out_specs, out_shape=out_shape, scratch_shapes=scratch,
        compiler_params=_cparams(("arbitrary",) if route else ("parallel",)),
        name="out_proj_route" if route else "out_proj",
    )(*args)
    return outs if route else outs[0]


def _top2(logits):
    lane = lax.broadcasted_iota(jnp.int32, logits.shape, 1)
    valid = lane < N_EXPERTS
    lg = jnp.where(valid, logits, -jnp.inf)
    ex = jnp.exp(lg - jnp.max(lg, axis=-1, keepdims=True))
    probs = ex / jnp.sum(ex, axis=-1, keepdims=True)
    cand = jnp.where(valid, probs, -1.0)
    p1 = jnp.max(cand, axis=-1, keepdims=True)
    i1 = jnp.min(jnp.where(cand == p1, lane, LANES), axis=-1, keepdims=True)
    cand = jnp.where(lane == i1, -1.0, cand)
    p2 = jnp.max(cand, axis=-1, keepdims=True)
    i2 = jnp.min(jnp.where(cand == p2, lane, LANES), axis=-1, keepdims=True)
    tot = p1 + p2
    return p1 / tot, i1, p2 / tot, i2


def _router_gates(xb, rw):
    g1, i1, g2, i2 = _top2(_dot(xb, rw))
    lane = lax.broadcasted_iota(jnp.int32, (xb.shape[0], LANES), 1)
    return jnp.where(lane == i1, g1, 0.0) + jnp.where(lane == i2, g2, 0.0)


def _pad_router(router_w):
    return jnp.pad(router_w, ((0, 0), (0, LANES - N_EXPERTS))).astype(BF16)


def _ffn_kernel(*refs, moe):
    if moe:
        x_ref, wg_ref, wu_ref, wd_ref, rw_ref, g_ref, b_ref, o_ref, xb_ref, acc_ref, gate_ref = refs
    else:
        x_ref, wg_ref, wu_ref, wd_ref, g_ref, b_ref, o_ref, xb_ref, acc_ref = refs
    e = pl.program_id(1)
    f = pl.program_id(2)

    @pl.when((e == 0) & (f == 0))
    def _():
        xb_ref[...] = x_ref[...].astype(xb_ref.dtype)
        acc_ref[...] = jnp.zeros_like(acc_ref)
        if moe:
            gate_ref[...] = _router_gates(xb_ref[...], rw_ref[...])

    xb = xb_ref[...]
    gt = _dot(xb, wg_ref[0])
    up = _dot(xb, wu_ref[0])
    y = _dot(jax.nn.silu(gt) * up, wd_ref[0])
    if moe:
        lane = lax.broadcasted_iota(jnp.int32, gate_ref.shape, 1)
        y = y * jnp.sum(jnp.where(lane == e, gate_ref[...], 0.0), axis=-1, keepdims=True)
    acc_ref[...] += y

    @pl.when((e == pl.num_programs(1) - 1) & (f == pl.num_programs(2) - 1))
    def _():
        o_ref[...] = _layer_norm(DEEPNORM_ALPHA * x_ref[...] + acc_ref[...], g_ref[...], b_ref[...])


def _ffn(x, w_gu, w_down, g, b, tm, tf, router_w=None):
    m = x.shape[0]
    n_exp, _, two_f = w_gu.shape
    nf = two_f // 2 // tf
    moe = router_w is not None
    row = lambda i, e, f: (i, 0)
    fixed = lambda i, e, f: (0, 0)
    in_specs = [pl.BlockSpec((tm, D_MODEL), row),
                pl.BlockSpec((1, D_MODEL, tf), lambda i, e, f: (e, 0, f)),
                pl.BlockSpec((1, D_MODEL, tf), lambda i, e, f: (e, 0, nf + f)),
                pl.BlockSpec((1, tf, D_MODEL), lambda i, e, f: (e, f, 0))]
    args = [x, w_gu, w_gu, w_down]
    scratch = [pltpu.VMEM((tm, D_MODEL), BF16), pltpu.VMEM((tm, D_MODEL), F32)]
    if moe:
        in_specs.append(pl.BlockSpec((D_MODEL, LANES), fixed))
        args.append(_pad_router(router_w))
        scratch.append(pltpu.VMEM((tm, LANES), F32))
    in_specs += [pl.BlockSpec((1, D_MODEL), fixed), pl.BlockSpec((1, D_MODEL), fixed)]
    args += [g.reshape(1, -1), b.reshape(1, -1)]
    return pl.pallas_call(
        functools.partial(_ffn_kernel, moe=moe),
        grid=(m // tm, n_exp, nf),
        in_specs=in_specs,
        out_specs=pl.BlockSpec((tm, D_MODEL), row),
        out_shape=jax.ShapeDtypeStruct((m, D_MODEL), F32),
        scratch_shapes=scratch,
        compiler_params=_cparams(("parallel", "arbitrary", "arbitrary")),
        name="moe_ffn" if moe else "ffn",
    )(*args)


META_E1, META_E2, META_R1, META_R2 = 0, 1, 2, 3
ZERO_ROWS = 256


def _route_tile(y, rw_ref, gtop_ref, meta_ref, cnt_ref, carry_ref):
    i = pl.program_id(0)
    tm = y.shape[0]

    @pl.when(i == 0)
    def _():
        carry_ref[...] = jnp.zeros_like(carry_ref)

    logits = _dot_nt(rw_ref[...], y)[:N_EXPERTS]
    row = lax.broadcasted_iota(jnp.int32, logits.shape, 0)
    ex = jnp.exp(logits - jnp.max(logits, axis=0, keepdims=True))
    probs = ex / jnp.sum(ex, axis=0, keepdims=True)
    p1 = jnp.max(probs, axis=0, keepdims=True)
    i1 = jnp.min(jnp.where(probs == p1, row, N_EXPERTS), axis=0, keepdims=True)
    cand = jnp.where(row == i1, -1.0, probs)
    p2 = jnp.max(cand, axis=0, keepdims=True)
    i2 = jnp.min(jnp.where(cand == p2, row, N_EXPERTS), axis=0, keepdims=True)
    tot = p1 + p2
    onehot = ((row == i1) | (row == i2)).astype(F32)
    r = lax.broadcasted_iota(jnp.int32, (tm, tm), 0)
    c = lax.broadcasted_iota(jnp.int32, (tm, tm), 1)
    before = _dot(onehot, (r < c).astype(F32)) + carry_ref[:, 0:1]
    rank1 = jnp.sum(jnp.where(row == i1, before, 0.0), axis=0, keepdims=True)
    rank2 = jnp.sum(jnp.where(row == i2, before, 0.0), axis=0, keepdims=True)
    meta_ref[...] = jnp.where(row == META_E1, i1.astype(F32), jnp.where(row == META_E2, i2.astype(F32), jnp.where(
        row == META_R1, rank1, jnp.where(row == META_R2, rank2, 0.0)))).astype(jnp.int32)
    gates = jnp.where(row == 0, p1 / tot, jnp.where(row == 1, p2 / tot, 0.0))
    gtop_ref[...] = jnp.concatenate([gates, jnp.zeros((LANES - N_EXPERTS, tm), F32)], axis=0).T
    carry_ref[...] += jnp.sum(onehot, axis=1, keepdims=True)
    cnt_ref[...] = carry_ref[...].astype(jnp.int32)


def _dispatch_plan(meta, cnt, tmg, n_tiles, tm):
    counts = cnt[:N_EXPERTS, 0]
    padded = (counts + tmg - 1) // tmg * tmg
    gend = jnp.cumsum(padded)
    gstart = gend - padded
    pos1 = gstart[meta[META_E1]] + meta[META_R1]
    pos2 = gstart[meta[META_E2]] + meta[META_R2]
    pos = jnp.concatenate([pos1.reshape(-1, 1, tm), pos2.reshape(-1, 1, tm)], axis=2).astype(jnp.int32)
    n_used = (gend[-1] // tmg).astype(jnp.int32).reshape(1)
    tile_start = jnp.arange(n_tiles, dtype=jnp.int32) * tmg
    tile_expert = jnp.minimum(jnp.sum(tile_start[:, None] >= gend[None, :], axis=1), N_EXPERTS - 1)
    tail = jnp.stack([gend[-1], (n_tiles * tmg - gend[-1]) // ZERO_ROWS])
    pads = jnp.concatenate([jnp.stack([gstart + counts, padded - counts]), tail[:, None]], axis=1).astype(jnp.int32)
    return pos, tile_expert.astype(jnp.int32), n_used, pads


def _row_copy(src_ref, src_row, dst_ref, dst_row, sem):
    return pltpu.make_async_copy(src_ref.at[pl.ds(src_row, 1)], dst_ref.at[pl.ds(dst_row, 1)], sem)


def _dispatch_kernel(pads_ref, pos_ref, x_ref, xs_ref, zero_ref, sem):
    i = pl.program_id(0)
    tm = x_ref.shape[0]

    def scatter(r, k):
        return _row_copy(x_ref, r, xs_ref, pos_ref[0, 0, k * tm + r], sem)

    def start(r, carry):
        scatter(r, 0).start(priority=0)
        scatter(r, 1).start(priority=1)
        return carry

    def wait(r, carry):
        scatter(r, 0).wait()
        scatter(r, 1).wait()
        return carry

    for r in range(tm):
        start(r, 0)
    lax.fori_loop(0, tm, wait, 0, unroll=8)

    @pl.when(i == pl.num_programs(0) - 1)
    def _():
        zero_ref[...] = jnp.zeros_like(zero_ref)
        for e in range(N_EXPERTS):
            first = pads_ref[0, e]
            n_pad = pads_ref[1, e]
            fill = lambda r: _row_copy(zero_ref, 0, xs_ref, first + r, sem)
            lax.fori_loop(0, n_pad, lambda r, c: (fill(r).start(), c)[1], 0)
            lax.fori_loop(0, n_pad, lambda r, c: (fill(r).wait(), c)[1], 0)
        tail_first = pads_ref[0, N_EXPERTS]
        n_blocks = pads_ref[1, N_EXPERTS]
        fill_tail = lambda r: pltpu.make_async_copy(
            zero_ref, xs_ref.at[pl.ds(pl.multiple_of(tail_first + r * ZERO_ROWS, ZERO_ROWS), ZERO_ROWS)], sem)
        lax.fori_loop(0, n_blocks, lambda r, c: (fill_tail(r).start(), c)[1], 0)
        lax.fori_loop(0, n_blocks, lambda r, c: (fill_tail(r).wait(), c)[1], 0)


def _dispatch(x, pos, pads, n_slots, tm):
    m = x.shape[0]
    return pl.pallas_call(
        _dispatch_kernel,
        grid_spec=pltpu.PrefetchScalarGridSpec(
            num_scalar_prefetch=1,
            grid=(m // tm,),
            in_specs=[pl.BlockSpec((1, 1, 2 * tm), lambda i, pads: (i, 0, 0), memory_space=pltpu.SMEM),
                      pl.BlockSpec((tm, D_MODEL), lambda i, pads: (i, 0))],
            out_specs=pl.BlockSpec(memory_space=pl.ANY),
            scratch_shapes=[pltpu.VMEM((ZERO_ROWS, D_MODEL), F32), pltpu.SemaphoreType.DMA(())]),
        out_shape=jax.ShapeDtypeStruct((n_slots, D_MODEL), F32),
        compiler_params=_cparams(("arbitrary",)),
        name="moe_dispatch",
    )(pads, pos, x)


def _grouped_ffn_kernel(te_ref, nu_ref, xs_ref, wg_ref, wu_ref, wd_ref, o_ref, xb_ref):
    j = pl.program_id(0)
    f = pl.program_id(1)
    used = j < nu_ref[0]

    @pl.when(used)
    def _():
        @pl.when(f == 0)
        def _():
            xb_ref[...] = xs_ref[...].astype(BF16)

        xb = xb_ref[...]
        y = _dot(jax.nn.silu(_dot(xb, wg_ref[0])) * _dot(xb, wu_ref[0]), wd_ref[0])

        @pl.when(f == 0)
        def _():
            o_ref[...] = y

        @pl.when(f > 0)
        def _():
            o_ref[...] += y

    @pl.when(jnp.logical_not(used) & (f == 0))
    def _():
        o_ref[...] = jnp.zeros_like(o_ref)


def _grouped_ffn(xs, w_gu, w_down, tile_expert, n_used, tmg, tf):
    n_slots = xs.shape[0]
    nf = w_down.shape[1] // tf
    tile = lambda j, f, te, nu: (jnp.minimum(j, nu[0] - 1), 0)
    chunk = lambda j, f, nu: jnp.where(j < nu[0], f, nf - 1)
    return pl.pallas_call(
        _grouped_ffn_kernel,
        grid_spec=pltpu.PrefetchScalarGridSpec(
            num_scalar_prefetch=2,
            grid=(n_slots // tmg, nf),
            in_specs=[pl.BlockSpec((tmg, D_MODEL), tile),
                      pl.BlockSpec((1, D_MODEL, tf), lambda j, f, te, nu: (te[j], 0, chunk(j, f, nu))),
                      pl.BlockSpec((1, D_MODEL, tf), lambda j, f, te, nu: (te[j], 0, nf + chunk(j, f, nu))),
                      pl.BlockSpec((1, tf, D_MODEL), lambda j, f, te, nu: (te[j], chunk(j, f, nu), 0))],
            out_specs=pl.BlockSpec((tmg, D_MODEL), lambda j, f, te, nu: (j, 0)),
            scratch_shapes=[pltpu.VMEM((tmg, D_MODEL), BF16)]),
        out_shape=jax.ShapeDtypeStruct((n_slots, D_MODEL), F32),
        compiler_params=_cparams(("arbitrary", "arbitrary")),
        name="moe_grouped",
    )(tile_expert, n_used, xs, w_gu, w_gu, w_down)


def _combine_kernel(pos_ref, posn_ref, x_ref, gt_ref, ys_ref, g_ref, b_ref, o_ref, ybuf, sem):
    i = pl.program_id(0)
    n = pl.num_programs(0)
    tm = x_ref.shape[0]
    slot = i % 2

    def gather(p_ref, s, r, k):
        return _row_copy(ys_ref, p_ref[0, 0, k * tm + r], ybuf.at[s, k], r, sem.at[s])

    def issue(p_ref, s):
        def body(r, carry):
            gather(p_ref, s, r, 0).start(priority=0)
            gather(p_ref, s, r, 1).start(priority=1)
            return carry
        for r in range(tm):
            body(r, 0)

    @pl.when(i == 0)
    def _():
        issue(pos_ref, 0)

    @pl.when(i + 1 < n)
    def _():
        issue(posn_ref, 1 - slot)

    def wait(r, carry):
        gather(pos_ref, slot, r, 0).wait()
        gather(pos_ref, slot, r, 1).wait()
        return carry

    lax.fori_loop(0, tm, wait, 0, unroll=8)
    gt = gt_ref[...]
    y = gt[:, 0:1] * ybuf[slot, 0] + gt[:, 1:2] * ybuf[slot, 1]
    o_ref[...] = _layer_norm(DEEPNORM_ALPHA * x_ref[...] + y, g_ref[...], b_ref[...])


def _combine(x, gtop, ys, pos, g, b, tm):
    m = x.shape[0]
    n = m // tm
    row = lambda i: (i, 0)
    fixed = lambda i: (0, 0)
    return pl.pallas_call(
        _combine_kernel,
        grid=(n,),
        in_specs=[pl.BlockSpec((1, 1, 2 * tm), lambda i: (i, 0, 0), memory_space=pltpu.SMEM),
                  pl.BlockSpec((1, 1, 2 * tm), lambda i: (jnp.minimum(i + 1, n - 1), 0, 0),
                               memory_space=pltpu.SMEM),
                  pl.BlockSpec((tm, D_MODEL), row), pl.BlockSpec((tm, LANES), row),
                  pl.BlockSpec(memory_space=pl.ANY),
                  pl.BlockSpec((1, D_MODEL), fixed), pl.BlockSpec((1, D_MODEL), fixed)],
        out_specs=pl.BlockSpec((tm, D_MODEL), row),
        out_shape=jax.ShapeDtypeStruct((m, D_MODEL), F32),
        scratch_shapes=[pltpu.VMEM((2, 2, tm, D_MODEL), F32), pltpu.SemaphoreType.DMA((2,))],
        compiler_params=_cparams(("arbitrary",)),
        name="moe_combine",
    )(pos, pos, x, gtop, ys, g.reshape(1, -1), b.reshape(1, -1))


def _moe(x, gtop, meta, cnt, w_gu, w_down, g, b, tmg, tf, tm):
    m = x.shape[0]
    n_tiles = -(-(2 * m + N_EXPERTS * (tmg - 1)) // tmg)
    pos, tile_expert, n_used, pads = _dispatch_plan(meta, cnt, tmg, n_tiles, tm)
    xs = _dispatch(x, pos, pads, n_tiles * tmg, tm)
    ys = _grouped_ffn(xs, w_gu, w_down, tile_expert, n_used, tmg, tf)
    return _combine(x, gtop, ys, pos, g, b, tm)


def _gdn_conv_post(conv, col0, q_ref, k_ref, v_ref):
    half = 0.5 * conv
    c = half + half * jnp.tanh(half)
    for j in range(conv.shape[1] // GDN_DK):
        col = col0 + GDN_DK * j
        x = c[:, GDN_DK * j:GDN_DK * (j + 1)]
        if col < GDN_QK:
            q_ref[:, col:col + GDN_DK] = x * lax.rsqrt(jnp.sum(x * x, axis=-1, keepdims=True) + L2_EPS) * GDN_DK ** -0.5
        elif col < 2 * GDN_QK:
            k_ref[:, col - GDN_QK:col - GDN_QK + GDN_DK] = x * lax.rsqrt(jnp.sum(x * x, axis=-1, keepdims=True) + L2_EPS)
        else:
            v_ref[:, col - 2 * GDN_QK:col - 2 * GDN_QK + GDN_DK] = x


def _gdn_gates(ab, alog_ref, dtb_ref, gb_ref):
    lane = lax.broadcasted_iota(jnp.int32, ab.shape, 1)
    decay = -jnp.exp(alog_ref[...]) * jax.nn.softplus(ab + dtb_ref[...])
    gb_ref[...] = jnp.where(lane < GDN_HEADS, decay, jax.nn.sigmoid(ab))


GDN_Z0 = GDN_CONV_DIM
GDN_AB0 = GDN_CONV_DIM + GDN_VW
GDN_QM0 = GDN_AB0 + GDN_AB_PAD
GDN_PROJ_GROUP = 2 * LANES


def _gdn_proj_kernel(x_ref, w_ref, cw_ref, alog_ref, dtb_ref,
                     q_ref, k_ref, v_ref, z_ref, gb_ref, qm_ref, cs_ref, buf_ref):
    i = pl.program_id(1)
    tm = x_ref.shape[0]
    pad = SUBLANES
    xb = x_ref[...].astype(BF16)

    @pl.when(i == 0)
    def _():
        buf_ref[0:pad, :] = jnp.zeros((pad, GDN_CONV_DIM), F32)

    def finish(c0, c1, d):
        cols = slice(c0, c1)
        if c1 <= GDN_CONV_DIM:
            buf_ref[pad:pad + tm, cols] = d
            conv = buf_ref[pad - 3:pad - 3 + tm, cols] * cw_ref[0:1, cols]
            conv = conv + buf_ref[pad - 2:pad - 2 + tm, cols] * cw_ref[1:2, cols]
            conv = conv + buf_ref[pad - 1:pad - 1 + tm, cols] * cw_ref[2:3, cols]
            conv = conv + d * cw_ref[3:4, cols]
            _gdn_conv_post(conv, c0, q_ref, k_ref, v_ref)
        elif c1 <= GDN_AB0:
            z_ref[:, c0 - GDN_Z0:c1 - GDN_Z0] = d
        else:
            _gdn_gates(d[:, :GDN_AB_PAD], alog_ref, dtb_ref, gb_ref)
            qm_ref[...] = d[:, GDN_AB_PAD:].astype(qm_ref.dtype)

    bounds = list(range(0, GDN_AB0, GDN_PROJ_GROUP)) + [GDN_AB0, w_ref.shape[1]]
    pending = None
    for c0, c1 in zip(bounds[:-1], bounds[1:]):
        d = _dot(xb, w_ref[:, c0:c1])
        if pending is not None:
            finish(*pending)
        pending = (c0, c1, d)
    finish(*pending)
    tail = buf_ref[tm:tm + pad, :]
    buf_ref[0:pad, :] = tail
    cs_ref[0] = tail


def _gdn_gate_params(a_log, dt_bias):
    padv = lambda v: jnp.pad(v.astype(F32), (0, GDN_AB_PAD - GDN_HEADS)).reshape(1, GDN_AB_PAD)
    return padv(a_log), padv(dt_bias)


def _gdn_pad_w_in(w_in):
    o3 = GDN_CONV_DIM + GDN_VW
    o4 = o3 + 2 * GDN_HEADS
    ab = jnp.pad(w_in[:, o3:o4], ((0, 0), (0, GDN_AB_PAD - 2 * GDN_HEADS)))
    return jnp.concatenate([w_in[:, :o3], ab, w_in[:, o4:]], axis=1)


def _gdn_project(x, w_pad, conv_w, a_log, dt_bias, batch, seq, tm):
    m = x.shape[0]
    nt = seq // tm
    row = lambda b, i: (b * nt + i, 0)
    fixed = lambda b, i: (0, 0)
    alog, dtb = _gdn_gate_params(a_log, dt_bias)
    wide = jax.ShapeDtypeStruct((m, GDN_QK), F32)
    return pl.pallas_call(
        _gdn_proj_kernel,
        grid=(batch, nt),
        in_specs=[pl.BlockSpec((tm, D_MODEL), row), pl.BlockSpec(w_pad.shape, fixed),
                  pl.BlockSpec(conv_w.shape, fixed), pl.BlockSpec(alog.shape, fixed),
                  pl.BlockSpec(dtb.shape, fixed)],
        out_specs=[pl.BlockSpec((tm, GDN_QK), row), pl.BlockSpec((tm, GDN_QK), row),
                   pl.BlockSpec((tm, GDN_VW), row), pl.BlockSpec((tm, GDN_VW), row),
                   pl.BlockSpec((tm, GDN_AB_PAD), row), pl.BlockSpec((tm, MEM_WIDTH), row),
                   pl.BlockSpec((1, SUBLANES, GDN_CONV_DIM), lambda b, i: (b, 0, 0))],
        out_shape=[wide, wide, wide, wide, jax.ShapeDtypeStruct((m, GDN_AB_PAD), F32),
                   jax.ShapeDtypeStruct((m, MEM_WIDTH), BF16),
                   jax.ShapeDtypeStruct((batch, SUBLANES, GDN_CONV_DIM), F32)],
        scratch_shapes=[pltpu.VMEM((tm + SUBLANES, GDN_CONV_DIM), F32)],
        compiler_params=_cparams(("parallel", "arbitrary")),
        name="gdn_proj",
    )(x, w_pad, conv_w, alog, dtb)


def _gdn_step_proj_kernel(x_ref, w_ref, cs_ref, cw_ref, alog_ref, dtb_ref,
                          pre_ref, q_ref, k_ref, v_ref, z_ref, gb_ref, qm_ref):
    proj = _dot(x_ref[...], w_ref[...])
    qkv = proj[:, :GDN_CONV_DIM]
    pre_ref[...] = qkv
    cw = cw_ref[...]
    conv = cs_ref[0] * cw[0:1]
    conv = conv + cs_ref[1] * cw[1:2]
    conv = conv + cs_ref[2] * cw[2:3]
    conv = conv + qkv * cw[3:4]
    _gdn_conv_post(conv, 0, q_ref, k_ref, v_ref)
    z_ref[...] = proj[:, GDN_Z0:GDN_AB0]
    _gdn_gates(proj[:, GDN_AB0:GDN_QM0], alog_ref, dtb_ref, gb_ref)
    qm_ref[...] = proj[:, GDN_QM0:].astype(qm_ref.dtype)


def _gdn_step_project(x, w_pad, conv_state, conv_w, a_log, dt_bias):
    m = x.shape[0]
    alog, dtb = _gdn_gate_params(a_log, dt_bias)
    cs = jnp.transpose(conv_state, (1, 0, 2))
    wide = jax.ShapeDtypeStruct((m, GDN_QK), F32)
    return pl.pallas_call(
        _gdn_step_proj_kernel,
        out_shape=[jax.ShapeDtypeStruct((m, GDN_CONV_DIM), F32), wide, wide, wide, wide,
                   jax.ShapeDtypeStruct((m, GDN_AB_PAD), F32), jax.ShapeDtypeStruct((m, MEM_WIDTH), F32)],
        compiler_params=pltpu.CompilerParams(vmem_limit_bytes=VMEM_LIMIT_MB * 1024 * 1024),
        name="gdn_step_proj",
    )(x, w_pad, cs, conv_w, alog, dtb)


def _gated_out(o, z, nw):
    on = o * lax.rsqrt(jnp.mean(o * o, axis=-1, keepdims=True) + RMS_EPS) * nw
    return on * jax.nn.silu(z)


GDN_SUPER = 2 * GDN_CHUNK
GDN_INTRA_ROWS = 256


def _gdn_intra_kernel(q_ref, k_ref, v_ref, gb_ref, u_ref, w_ref, qg_ref, kg_ref, qk_ref, egl_ref):
    n = GDN_SUPER
    c = GDN_CHUNK
    shift = c.bit_length() - 1
    r = lax.broadcasted_iota(jnp.int32, (n, n), 0)
    col = lax.broadcasted_iota(jnp.int32, (n, n), 1)
    same = (r >> shift) == (col >> shift)
    causal = same & (r >= col)
    strict = same & (r > col)
    eye = (r == col).astype(F32)
    tri = causal.astype(F32)
    groups = [slice(n * j, n * (j + 1)) for j in range(q_ref.shape[0] // n)]
    gb = [gb_ref[rows, :] for rows in groups]
    gc_cols = [_dot_f32(tri, x) for x in gb]
    gc_rows = [x.T for x in gc_cols]
    chunk_of_row = lax.broadcasted_iota(jnp.int32, (n, LANES), 0) >> shift
    gl_cols = []
    for x in gc_cols:
        total = x[n - 1:n, :]
        for i in reversed(range(n // c - 1)):
            total = jnp.where(chunk_of_row == i, x[c * (i + 1) - 1:c * (i + 1), :], total)
        gl_cols.append(total)
    for j, rows in enumerate(groups):
        egl_ref[rows, :] = jnp.exp(gl_cols[j])
    parts = [(j, h) for j in range(len(groups)) for h in range(GDN_HEADS)]
    idx = [(groups[j], slice(GDN_DK * h, GDN_DK * (h + 1))) for j, h in parts]
    ps = range(len(parts))
    q = [q_ref[i] for i in idx]
    k = [k_ref[i] for i in idx]
    gcol = [gc_cols[j][:, h:h + 1] for j, h in parts]
    beta = [gb[j][:, GDN_HEADS + h:GDN_HEADS + h + 1] for j, h in parts]
    decay = [jnp.exp(jnp.where(causal, gcol[p] - gc_rows[j][h:h + 1, :], -jnp.inf)) for p, (j, h) in enumerate(parts)]
    kb = [k[p] * beta[p] for p in ps]
    a = [jnp.where(strict, _dot_nt(kb[p], k[p]) * decay[p], 0.0) for p in ps]
    qk = [jnp.where(causal, _dot_nt(q[p], k[p]) * decay[p], 0.0) for p in ps]
    qk = [jnp.concatenate([m[c * i:c * (i + 1), c * i:c * (i + 1)] for i in range(n // c)], axis=0) for m in qk]
    for j, rows in enumerate(groups):
        qk_ref[rows, :] = jnp.concatenate(qk[GDN_HEADS * j:GDN_HEADS * (j + 1)], axis=1).astype(qk_ref.dtype)
    tinv = [eye - a[p] for p in ps]
    apow = a
    for _ in range(shift - 1):
        apow = [_dot(apow[p], apow[p]) for p in ps]
        tinv = [_dot(tinv[p], eye + apow[p]) for p in ps]
    eg = [jnp.exp(gcol[p]) for p in ps]
    for p in ps:
        u_ref[idx[p]] = _dot(tinv[p], v_ref[idx[p]] * beta[p])
    for p in ps:
        w_ref[idx[p]] = _dot(tinv[p], kb[p] * eg[p]).astype(w_ref.dtype)
    for p, (j, h) in enumerate(parts):
        qg_ref[idx[p]] = (q[p] * eg[p]).astype(qg_ref.dtype)
        kg_ref[idx[p]] = (k[p] * jnp.exp(gl_cols[j][:, h:h + 1] - gcol[p])).astype(kg_ref.dtype)


def _gdn_scan_kernel(u_ref, w_ref, qg_ref, kg_ref, qk_ref, egl_ref, z_ref, nw_ref, o_ref, sfin_ref, s_ref):
    n = pl.program_id(0)
    c = GDN_CHUNK

    @pl.when(n == 0)
    def _():
        s_ref[...] = jnp.zeros_like(s_ref)

    nw = nw_ref[...]

    def per_sequence(b, carry):
        egl = egl_ref[b, 0:1, :]
        heads = range(GDN_HEADS)
        sls = [slice(GDN_DK * h, GDN_DK * (h + 1)) for h in heads]
        state = [s_ref[b, h] for h in heads]
        ws_qs = [_dot(jnp.concatenate([w_ref[b, :, sls[h]], qg_ref[b, :, sls[h]]], axis=0), state[h])
                 for h in heads]
        v_new = [(u_ref[b, :, sls[h]] - ws_qs[h][:c]).astype(BF16) for h in heads]
        pairs = [qk_ref[b, :, LANES * j:LANES * (j + 1)] for j in range(GDN_HEADS // 2)]
        o = [ws_qs[h][c:] + _dot(pairs[h // 2][:, c * (h % 2):c * (h % 2 + 1)], v_new[h]) for h in heads]
        upd = [_dot_tn(kg_ref[b, :, sls[h]], v_new[h]) for h in heads]
        for h in heads:
            s_ref[b, h] = state[h] * egl[:, h:h + 1] + upd[h]
        for h in heads:
            o_ref[b, :, sls[h]] = _gated_out(o[h], z_ref[b, :, sls[h]], nw).astype(o_ref.dtype)
        return carry

    lax.fori_loop(0, u_ref.shape[0], per_sequence, 0, unroll=4)

    @pl.when(n == pl.num_programs(0) - 1)
    def _():
        sfin_ref[...] = s_ref[...]


def _gdn_chunked(q, k, v, z, gb, norm_w, batch, seq):
    m = batch * seq
    row = lambda i: (i, 0)
    wide = pl.BlockSpec((GDN_INTRA_ROWS, GDN_QK), row)
    qk_w = GDN_HEADS * GDN_CHUNK
    u, w, qg, kg, qk, egl = pl.pallas_call(
        _gdn_intra_kernel,
        grid=(m // GDN_INTRA_ROWS,),
        in_specs=[wide, wide, wide, pl.BlockSpec((GDN_INTRA_ROWS, GDN_AB_PAD), row)],
        out_specs=[wide, wide, wide, wide, pl.BlockSpec((GDN_INTRA_ROWS, qk_w), row),
                   pl.BlockSpec((GDN_INTRA_ROWS, LANES), row)],
        out_shape=[jax.ShapeDtypeStruct((m, GDN_VW), F32), jax.ShapeDtypeStruct((m, GDN_QK), BF16),
                   jax.ShapeDtypeStruct((m, GDN_QK), BF16), jax.ShapeDtypeStruct((m, GDN_QK), BF16),
                   jax.ShapeDtypeStruct((m, qk_w), BF16), jax.ShapeDtypeStruct((m, LANES), F32)],
        compiler_params=_cparams(("parallel",)),
        name="gdn_intra",
    )(q, k, v, gb)
    per_seq = lambda a: a.reshape(batch, seq, a.shape[-1])
    chunk = lambda width: pl.BlockSpec((batch, GDN_CHUNK, width), lambda n: (0, n, 0))
    state_spec = pl.BlockSpec((batch, GDN_HEADS, GDN_DK, GDN_DV), lambda n: (0, 0, 0, 0))
    h, s_fin = pl.pallas_call(
        _gdn_scan_kernel,
        grid=(seq // GDN_CHUNK,),
        in_specs=[chunk(GDN_VW), chunk(GDN_QK), chunk(GDN_QK), chunk(GDN_QK), chunk(qk_w), chunk(LANES),
                  chunk(GDN_VW), pl.BlockSpec((1, GDN_DV), lambda n: (0, 0))],
        out_specs=[chunk(GDN_VW), state_spec],
        out_shape=[jax.ShapeDtypeStruct((batch, seq, GDN_VW), BF16),
                   jax.ShapeDtypeStruct((batch, GDN_HEADS, GDN_DK, GDN_DV), F32)],
        scratch_shapes=[pltpu.VMEM((batch, GDN_HEADS, GDN_DK, GDN_DV), F32)],
        compiler_params=_cparams(("arbitrary",)),
        name="gdn_scan",
    )(per_seq(u), per_seq(w), per_seq(qg), per_seq(kg), per_seq(qk), per_seq(egl), per_seq(z),
      norm_w.reshape(1, -1))
    return h.reshape(m, GDN_VW), s_fin


def _gdn_recurrent_kernel(q_ref, k_ref, v_ref, z_ref, gb_ref, nw_ref, s_ref, o_ref, so_ref):
    bs = q_ref.shape[0]
    gb = gb_ref[...]
    nw = nw_ref[...]
    for h in range(GDN_HEADS):
        sl = slice(GDN_DK * h, GDN_DK * (h + 1))
        q_t = q_ref[:, sl].T
        k_t = k_ref[:, sl].T
        v = v_ref[:, sl]
        rows = []
        for b in range(bs):
            state = s_ref[b, h] * jnp.exp(gb[b:b + 1, h:h + 1])
            kcol = k_t[:, b:b + 1]
            v_new = (v[b:b + 1, :] - jnp.sum(kcol * state, axis=0, keepdims=True)) \
                * gb[b:b + 1, GDN_HEADS + h:GDN_HEADS + h + 1]
            state = state + kcol * v_new
            so_ref[b, h] = state
            rows.append(jnp.sum(q_t[:, b:b + 1] * state, axis=0, keepdims=True))
        o_ref[:, sl] = _gated_out(jnp.concatenate(rows, axis=0), z_ref[:, sl], nw).astype(o_ref.dtype)


def _gdn_recurrent(q, k, v, z, gb, norm_w, state, bs):
    batch = q.shape[0]
    row = lambda i: (i, 0)
    wide = pl.BlockSpec((bs, GDN_QK), row)
    st = pl.BlockSpec((bs, GDN_HEADS, GDN_DK, GDN_DV), lambda i: (i, 0, 0, 0))
    return pl.pallas_call(
        _gdn_recurrent_kernel,
        grid=(batch // bs,),
        in_specs=[wide, wide, wide, wide, pl.BlockSpec((bs, GDN_AB_PAD), row),
                  pl.BlockSpec((1, GDN_DV), lambda i: (0, 0)), st],
        out_specs=[wide, st],
        out_shape=[jax.ShapeDtypeStruct((batch, GDN_VW), F32), jax.ShapeDtypeStruct(state.shape, F32)],
        compiler_params=_cparams(("parallel",)),
        name="gdn_recurrent",
    )(q, k, v, z, gb, norm_w.reshape(1, -1), state)


PROMPT_TM = 1024
GDN_TM = 512
FFN_TM = 1024
FFN_TF = 512
MOE_TMG = 1024
MOE_TM = 512
DECODE_BS = 8


def kernel(x_prompt, x_sample, cache_swa_k, cache_swa_v, state_gdn_conv, state_gdn_rec, cache_mem_k, cache_mem_v, mem_prompt, w_in_swa, swa_sinks, w_in_gdn, gdn_conv_w, gdn_a_log, gdn_dt_bias, gdn_norm_w, w_mem_kv, w_out, ln1_g, ln1_b, ln2_g, ln2_b, ffn_w_gu, ffn_w_down, router_w, moe_w_gu, moe_w_down):
    batch, seq, _ = x_prompt.shape
    dec = x_sample.shape[0]
    assert x_sample.shape[1] == 1
    yp = x_prompt.reshape(batch * seq, D_MODEL)
    ys = x_sample.reshape(dec, D_MODEL)
    mem_flat = mem_prompt.reshape(batch * N_MEM, D_MODEL)
    mem3 = lambda a: a.reshape(-1, N_MEM, MEM_WIDTH)

    pair_perm = _swa_pair_perm()
    w_in0 = jnp.concatenate([w_in_swa[0][:, :SWA_QW][:, pair_perm], w_in_swa[0][:, SWA_QW:]], axis=1).astype(BF16)
    w_out0 = w_out[0].astype(BF16)
    w_out0_paired = jnp.concatenate([w_out0[:SWA_QW][pair_perm], w_out0[SWA_QW:]], axis=0)
    w_gu0 = ffn_w_gu[0:1].astype(BF16)
    w_dn0 = ffn_w_down[0:1].astype(BF16)
    mkv = _matmul(mem_flat, w_mem_kv[0].astype(BF16), PROMPT_TM)
    mk0, mv0 = mkv[:, :MEM_WIDTH], mkv[:, MEM_WIDTH:]
    cos_p, sin_p = _rope_tables(jnp.arange(seq, dtype=jnp.int32))
    q, k, v, qm, k_tail, v_tail = _swa_project(yp, w_in0, cos_p, sin_p, PROMPT_TM)
    h = _swa_attention(q, k, v, swa_sinks[0], batch, seq)
    ma = _mem_attention(qm, mem3(mk0), mem3(mv0), batch, seq, PROMPT_TM)
    swa_kp = k_tail.reshape(batch, WINDOW, SWA_KV_HEADS, HEAD_DIM)
    swa_vp = v_tail.reshape(batch, WINDOW, SWA_KV_HEADS, HEAD_DIM)
    yp = _out_project(yp, h, ma, w_out0_paired, ln1_g[0], ln1_b[0], PROMPT_TM)
    yp = _ffn(yp, w_gu0, w_dn0, ln2_g[0], ln2_b[0], FFN_TM, FFN_TF)

    cos_s, sin_s = _rope_tables(jnp.full((dec,), PAST_LEN, jnp.int32))
    q, k, v, qm, _, _ = _swa_project(ys, w_in0, cos_s, sin_s, dec)
    q = q[:, np.argsort(pair_perm)]
    qg = q.reshape(dec, SWA_KV_HEADS, SWA_GROUP, HEAD_DIM).transpose(2, 0, 1, 3).reshape(SWA_GROUP, dec, SWA_KVW)
    sink_g = swa_sinks[0].reshape(SWA_KV_HEADS, SWA_GROUP).T
    mem_kt, mem_vt = _feature_major(cache_mem_k), _feature_major(cache_mem_v)
    og, swa_ks, swa_vs = _decode_attention(
        qg, _feature_major(cache_swa_k), _feature_major(cache_swa_v),
        1.0, HEAD_DIM, DECODE_BS, new_k=k, new_v=v, sinks=sink_g)
    h = og.reshape(SWA_GROUP, dec, SWA_KV_HEADS, HEAD_DIM).transpose(1, 2, 0, 3).reshape(dec, SWA_QW)
    ma, = _decode_attention(qm.reshape(1, dec, MEM_WIDTH), mem_kt, mem_vt,
                            MEM_HEAD_DIM ** -0.5, MEM_HEAD_DIM, DECODE_BS, layer=0)
    ys = _out_project(ys, h, ma[0], w_out0, ln1_g[0], ln1_b[0], dec)
    ys = _ffn(ys, w_gu0, w_dn0, ln2_g[0], ln2_b[0], dec, FFN_TF)

    w_in1 = _gdn_pad_w_in(w_in_gdn[0]).astype(BF16)
    w_out1 = w_out[1].astype(BF16)
    w_gu1 = moe_w_gu[0]
    w_dn1 = moe_w_down[0]
    mkv = _matmul(mem_flat, w_mem_kv[1].astype(BF16), PROMPT_TM)
    mk1, mv1 = mkv[:, :MEM_WIDTH], mkv[:, MEM_WIDTH:]
    q, k, v, z, gb, qm, conv_tail = _gdn_project(yp, w_in1, gdn_conv_w[0], gdn_a_log[0], gdn_dt_bias[0],
                                                 batch, seq, GDN_TM)
    h, rec_p = _gdn_chunked(q, k, v, z, gb, gdn_norm_w[0], batch, seq)
    ma = _mem_attention(qm, mem3(mk1), mem3(mv1), batch, seq, PROMPT_TM)
    yp, gtop, meta, cnt = _out_project(yp, h, ma, w_out1, ln1_g[1], ln1_b[1], PROMPT_TM, router_w=router_w[0])
    yp = _moe(yp, gtop, meta, cnt, w_gu1, w_dn1, ln2_g[1], ln2_b[1], MOE_TMG, FFN_TF, MOE_TM)

    pre, q, k, v, z, gb, qm = _gdn_step_project(ys, w_in1, state_gdn_conv[0], gdn_conv_w[0],
                                                gdn_a_log[0], gdn_dt_bias[0])
    h, rec_s = _gdn_recurrent(q, k, v, z, gb, gdn_norm_w[0], state_gdn_rec[0], DECODE_BS)
    ma, = _decode_attention(qm.reshape(1, dec, MEM_WIDTH), mem_kt, mem_vt,
                            MEM_HEAD_DIM ** -0.5, MEM_HEAD_DIM, DECODE_BS, layer=1)
    ys = _out_project(ys, h, ma[0], w_out1, ln1_g[1], ln1_b[1], dec)
    ys = _ffn(ys, w_gu1, w_dn1, ln2_g[1], ln2_b[1], dec, FFN_TF, router_w=router_w[0])

    mem_shape = (batch, N_MEM, MEM_HEADS, MEM_HEAD_DIM)
    conv_s = jnp.concatenate([state_gdn_conv[0][:, 1:], pre[:, None, :]], axis=1)
    return (yp.reshape(batch, seq, D_MODEL), ys.reshape(dec, 1, D_MODEL),
            swa_kp[None], swa_vp[None],
            conv_tail[None, :, SUBLANES - (GDN_CONV_W - 1):, :], rec_p[None],
            jnp.stack([mk0.reshape(mem_shape), mk1.reshape(mem_shape)]),
            jnp.stack([mv0.reshape(mem_shape), mv1.reshape(mem_shape)]),
            _key_major(swa_ks, SWA_KV_HEADS)[None], _key_major(swa_vs, SWA_KV_HEADS)[None],
            conv_s[None], rec_s[None])
```
